```python
import math
import jax, jax.numpy as jnp
from jax import lax
import numpy as np

D_MODEL = 1024
BATCH = 16
SEQ = 2048
DEPTH = 1

HEAD_DIM = 64
D_MIX = D_MODEL
MOBA_HEADS = 8
MOBA_BLOCK = 256
MOBA_TOPK = 3
DIFF_HEADS = 4
DIFF_V_DIM = 2 * HEAD_DIM
Q_BLOCK = 128
MOBA_W = MOBA_HEADS * HEAD_DIM
DIFF_QK_W = DIFF_HEADS * 2 * HEAD_DIM
DIFF_V_W = DIFF_HEADS * DIFF_V_DIM
PROJ_COLS = 3 * MOBA_W + 2 * DIFF_QK_W + DIFF_V_W
SPLITS = (MOBA_W, 2 * MOBA_W, 3 * MOBA_W, 3 * MOBA_W + DIFF_QK_W, 3 * MOBA_W + 2 * DIFF_QK_W)
PEER_HEADS = 8
PEER_N_KEYS = 128
PEER_N_EXPERTS = PEER_N_KEYS * PEER_N_KEYS
PEER_D_QUERY = 256
PEER_D_HALF = PEER_D_QUERY // 2
PEER_TOPK = 16
PEER_CHUNK = 128
RMS_EPS = 1e-6

kernel_name = 'hybrid_moba_diffattn_peer_layer'


def rms_norm(x, g):
    xf = x.astype(jnp.float32)
    y = xf * lax.rsqrt(jnp.mean(xf * xf, axis=-1, keepdims=True) + RMS_EPS)
    return (y * g.astype(jnp.float32)).astype(x.dtype)


def alibi_slopes(n):
    return jnp.asarray(2.0 ** (-8.0 * np.arange(1, n + 1) / n), dtype=jnp.float32)


def lambda_init_fn(layer):
    return 0.8 - 0.6 * math.exp(-0.3 * layer)


def split_heads(t, n, dh):
    b, s, _ = t.shape
    return t.reshape(b, s, n, dh).transpose(0, 2, 1, 3)


def moba_query_block(q_blk, idx, k_blocks, v_blocks, q_start, slopes):
    n_h, n_q, d = q_blk.shape
    kk = idx.shape[-1]
    scale = d ** -0.5
    pos_q = q_start + jnp.arange(n_q, dtype=jnp.int32)
    own = q_start // MOBA_BLOCK
    blk_off = jnp.arange(MOBA_BLOCK, dtype=jnp.int32)
    h_ix = jnp.arange(n_h)[:, None, None]
    k_sel = k_blocks[h_ix, idx]
    v_sel = v_blocks[h_ix, idx]
    pos_sel = idx[..., None] * MOBA_BLOCK + blk_off
    s_sel = jnp.einsum('hqd,hqkjd->hqkj', q_blk, k_sel).astype(jnp.float32) * scale
    s_sel = s_sel - slopes[:, None, None, None] * (pos_q[None, :, None, None] - pos_sel).astype(jnp.float32)
    s_sel = jnp.where((idx < own)[..., None], s_sel, -jnp.inf)
    k_own = lax.dynamic_index_in_dim(k_blocks, own, axis=1, keepdims=False)
    v_own = lax.dynamic_index_in_dim(v_blocks, own, axis=1, keepdims=False)
    dist_own = pos_q[:, None] - (own * MOBA_BLOCK + blk_off)[None, :]
    s_own = jnp.einsum('hqd,hjd->hqj', q_blk, k_own).astype(jnp.float32) * scale
    s_own = s_own - slopes[:, None, None] * dist_own.astype(jnp.float32)
    s_own = jnp.where(dist_own >= 0, s_own, -jnp.inf)
    s = jnp.concatenate([s_sel.reshape(n_h, n_q, kk * MOBA_BLOCK), s_own], axis=-1)
    p = jax.nn.softmax(s, axis=-1).astype(v_own.dtype)
    p_sel = p[..., :kk * MOBA_BLOCK].reshape(n_h, n_q, kk, MOBA_BLOCK)
    p_own = p[..., kk * MOBA_BLOCK:]
    return jnp.einsum('hqkj,hqkjd->hqd', p_sel, v_sel) + jnp.einsum('hqj,hjd->hqd', p_own, v_own)


def moba_attention(q, k, v):
    b, n_h, s, d = q.shape
    nb = -(-s // MOBA_BLOCK)
    kk = min(MOBA_TOPK, nb)
    pad = nb * MOBA_BLOCK - s
    k_blocks = jnp.pad(k, ((0, 0), (0, 0), (0, pad), (0, 0))).reshape(b, n_h, nb, MOBA_BLOCK, d)
    v_blocks = jnp.pad(v, ((0, 0), (0, 0), (0, pad), (0, 0))).reshape(b, n_h, nb, MOBA_BLOCK, d)
    k_mean = jnp.mean(k_blocks.astype(jnp.float32), axis=3)
    gate = jnp.einsum('bhsd,bhnd->bhsn', q.astype(jnp.float32), k_mean)
    pos = jnp.arange(s, dtype=jnp.int32)
    past = jnp.arange(nb, dtype=jnp.int32)[None, :] < (pos // MOBA_BLOCK)[:, None]
    gate = jnp.where(past, gate, -jnp.inf)
    _, sel = lax.top_k(gate, kk)
    nqb = s // Q_BLOCK
    q_r = q.reshape(b, n_h, nqb, Q_BLOCK, d).transpose(0, 2, 1, 3, 4)
    sel_r = sel.reshape(b, n_h, nqb, Q_BLOCK, kk).transpose(0, 2, 1, 3, 4)
    starts = jnp.arange(nqb, dtype=jnp.int32) * Q_BLOCK
    slopes = alibi_slopes(n_h)

    def per_batch(args):
        q_b, sel_b, kb, vb = args
        return lax.map(lambda a: moba_query_block(a[0], a[1], kb, vb, a[2], slopes), (q_b, sel_b, starts))

    out = lax.map(per_batch, (q_r, sel_r, k_blocks, v_blocks))
    return out.transpose(0, 1, 3, 2, 4).reshape(b, s, n_h * d)


def diff_attention(q1, q2, k1, k2, v, lam):
    b, n_h, s, d = q1.shape
    scale = d ** -0.5
    slopes = alibi_slopes(n_h)[:, None, None]
    outs = []
    for i in range(s // Q_BLOCK):
        s0, e = i * Q_BLOCK, (i + 1) * Q_BLOCK
        dist = (jnp.arange(s0, e)[:, None] - jnp.arange(e)[None, :]).astype(jnp.float32)
        bias = jnp.where(dist >= 0, -slopes * dist, -jnp.inf)
        a1 = jax.nn.softmax(jnp.einsum('bhqd,bhkd->bhqk', q1[:, :, s0:e], k1[:, :, :e]).astype(jnp.float32) * scale + bias, axis=-1)
        a2 = jax.nn.softmax(jnp.einsum('bhqd,bhkd->bhqk', q2[:, :, s0:e], k2[:, :, :e]).astype(jnp.float32) * scale + bias, axis=-1)
        a = (a1 - lam * a2).astype(v.dtype)
        outs.append(jnp.einsum('bhqk,bhkd->bhqd', a, v[:, :, :e]))
    return jnp.concatenate(outs, axis=2)


def peer_ffn(x, w_query, sub_keys, u, v):
    b, s, dm = x.shape
    xt = x.reshape(b * s // PEER_CHUNK, PEER_CHUNK, dm)

    def chunk(xc):
        q = (xc @ w_query).reshape(PEER_CHUNK, PEER_HEADS, 2, PEER_D_HALF)
        sc = jnp.einsum('chpd,hpnd->chpn', q, sub_keys).astype(jnp.float32)
        s1, i1 = lax.top_k(sc[:, :, 0], PEER_TOPK)
        s2, i2 = lax.top_k(sc[:, :, 1], PEER_TOPK)
        cand = (s1[..., :, None] + s2[..., None, :]).reshape(PEER_CHUNK, PEER_HEADS, PEER_TOPK * PEER_TOPK)
        cand_idx = (i1[..., :, None] * PEER_N_KEYS + i2[..., None, :]).reshape(PEER_CHUNK, PEER_HEADS, PEER_TOPK * PEER_TOPK)
        best, pos = lax.top_k(cand, PEER_TOPK)
        experts = jnp.take_along_axis(cand_idx, pos, axis=-1)
        g = jax.nn.softmax(best, axis=-1)
        u_sel = u[experts]
        v_sel = v[experts]
        h = jax.nn.gelu(jnp.einsum('cd,chkd->chk', xc, u_sel), approximate=False)
        act = (h.astype(jnp.float32) * g).astype(v_sel.dtype)
        return jnp.einsum('chk,chkd->cd', act, v_sel)

    return lax.map(chunk, xt).reshape(b, s, dm)


def setup_inputs(seed: int = 0) -> dict:
    key = jax.random.key(seed)
    ks = jax.random.split(key, 13)
    f32 = jnp.float32

    def nrm(k, shape, scale):
        return jax.random.normal(k, shape, f32) * scale

    def gain(k, shape):
        return 1.0 + 0.02 * jax.random.normal(k, shape, f32)

    return {
        'x': nrm(ks[0], (BATCH, SEQ, D_MODEL), 1.0),
        'norm1_g': gain(ks[1], (DEPTH, D_MODEL)),
        'w_in': nrm(ks[2], (DEPTH, D_MODEL, PROJ_COLS), D_MODEL ** -0.5),
        'moba_qk_norm_g': gain(ks[3], (DEPTH, 2, HEAD_DIM)),
        'diff_qk_norm_g': gain(ks[4], (DEPTH, 2, HEAD_DIM)),
        'diff_lambda': nrm(ks[5], (DEPTH, 4, HEAD_DIM), 0.1),
        'diff_subln_g': gain(ks[6], (DEPTH, DIFF_V_DIM)),
        'w_out': nrm(ks[7], (DEPTH, D_MIX, D_MODEL), D_MIX ** -0.5),
        'norm2_g': gain(ks[8], (DEPTH, D_MODEL)),
        'peer_w_query': nrm(ks[9], (DEPTH, D_MODEL, PEER_HEADS * PEER_D_QUERY), D_MODEL ** -0.5),
        'peer_sub_keys': nrm(ks[10], (DEPTH, PEER_HEADS, 2, PEER_N_KEYS, PEER_D_HALF), PEER_D_HALF ** -0.5),
        'peer_u': nrm(ks[11], (DEPTH, PEER_N_EXPERTS, D_MODEL), D_MODEL ** -0.5),
        'peer_v': nrm(ks[12], (DEPTH, PEER_N_EXPERTS, D_MODEL), 0.3),
    }


def reference(x, norm1_g, w_in, moba_qk_norm_g, diff_qk_norm_g, diff_lambda, diff_subln_g, w_out, norm2_g, peer_w_query, peer_sub_keys, peer_u, peer_v):
    b, s, _ = x.shape
    for l in range(DEPTH):
        lam_init = lambda_init_fn(l)
        h = rms_norm(x, norm1_g[l])
        proj = h @ w_in[l]
        mq, mk, mv, dq, dk, dv = jnp.split(proj, SPLITS, axis=-1)
        mq = rms_norm(split_heads(mq, MOBA_HEADS, HEAD_DIM), moba_qk_norm_g[l, 0])
        mk = rms_norm(split_heads(mk, MOBA_HEADS, HEAD_DIM), moba_qk_norm_g[l, 1])
        moba_out = moba_attention(mq, mk, split_heads(mv, MOBA_HEADS, HEAD_DIM))
        dq = rms_norm(dq.reshape(b, s, DIFF_HEADS, 2, HEAD_DIM).transpose(0, 2, 3, 1, 4), diff_qk_norm_g[l, 0])
        dk = rms_norm(dk.reshape(b, s, DIFF_HEADS, 2, HEAD_DIM).transpose(0, 2, 3, 1, 4), diff_qk_norm_g[l, 1])
        lam_p = diff_lambda[l].astype(jnp.float32)
        lam = jnp.exp(jnp.dot(lam_p[0], lam_p[1])) - jnp.exp(jnp.dot(lam_p[2], lam_p[3])) + lam_init
        diff_o = diff_attention(dq[:, :, 0], dq[:, :, 1], dk[:, :, 0], dk[:, :, 1], split_heads(dv, DIFF_HEADS, DIFF_V_DIM), lam)
        diff_o = rms_norm(diff_o, diff_subln_g[l]) * (1.0 - lam_init)
        diff_out = diff_o.transpose(0, 2, 1, 3).reshape(b, s, DIFF_V_W)
        x = x + jnp.concatenate([moba_out, diff_out], axis=-1) @ w_out[l]
        x = x + peer_ffn(rms_norm(x, norm2_g[l]), peer_w_query[l], peer_sub_keys[l], peer_u[l], peer_v[l])
    return x
```

```python
import functools
import math

import jax
import jax.numpy as jnp
import numpy as np
from jax import lax
from jax.experimental import pallas as pl
from jax.experimental.pallas import tpu as pltpu

HEAD_DIM = 64
MOBA_HEADS = 8
MOBA_BLOCK = 256
MOBA_TOPK = 3
DIFF_HEADS = 4
PEER_HEADS = 8
PEER_N_KEYS = 128
PEER_TOPK = 16
RMS_EPS = 1e-6

LANES = 128
SUBLANES = 8
VMEM_LIMIT = 48 * 1024 * 1024

PROJ_TM = 512
MIX_TM = 256
PEER_TT = 64
PACK_ROWS = 512

NEG_INF = float("-inf")
_NT = (((1,), (1,)), ((), ()))


def _dot(a, b):
    return jnp.dot(a, b, preferred_element_type=jnp.float32)


def _dot_nt(a, b):
    return lax.dot_general(a, b, _NT, preferred_element_type=jnp.float32)


def _split_bf16(x):
    hi = x.astype(jnp.bfloat16)
    lo = (x - hi.astype(jnp.float32)).astype(jnp.bfloat16)
    return hi, lo


def _proj_kernel(x_ref, g1_ref, wqk_ref, wvt_ref, gqk_ref, gsum_ref,
                 mq_ref, mk_ref, dq_ref, dk_ref, mvt_ref, dvt_ref, kmean_ref):
    x = x_ref[...]
    ms = jnp.mean(x * x, axis=-1, keepdims=True)
    h = (x * lax.rsqrt(ms + RMS_EPS) * g1_ref[...]).astype(jnp.bfloat16)
    gsum = gsum_ref[...]
    outs = (mq_ref, mk_ref, dq_ref, dk_ref)
    for sec in range(4):
        for half in range(2):
            c0 = sec * 512 + half * 256
            p = _dot(h, wqk_ref[:, c0:c0 + 256])
            for s in range(2):
                ps = p[:, s * LANES:(s + 1) * LANES]
                sq_hi, sq_lo = _split_bf16(ps * ps)
                ssq = _dot(sq_hi, gsum) + _dot(sq_lo, gsum)
                g = gqk_ref[:, c0 + s * LANES:c0 + (s + 1) * LANES]
                y = ps * lax.rsqrt(ssq * (1.0 / HEAD_DIM) + RMS_EPS) * g
                l0 = half * 256 + s * LANES
                outs[sec][:, l0:l0 + LANES] = y.astype(jnp.bfloat16)
                if sec == 1:
                    for blk in range(PROJ_TM // MOBA_BLOCK):
                        kmean_ref[blk:blk + 1, l0:l0 + LANES] = jnp.mean(
                            y[blk * MOBA_BLOCK:(blk + 1) * MOBA_BLOCK], axis=0, keepdims=True)
    kmean_ref[PROJ_TM // MOBA_BLOCK:, :] = jnp.zeros((SUBLANES - PROJ_TM // MOBA_BLOCK, 512), jnp.float32)
    vt = _dot_nt(wvt_ref[...], h).astype(jnp.bfloat16)
    for blk in range(PROJ_TM // MOBA_BLOCK):
        mvt_ref[blk] = vt[0:512, blk * MOBA_BLOCK:(blk + 1) * MOBA_BLOCK]
        dvt_ref[blk] = vt[512:1024, blk * MOBA_BLOCK:(blk + 1) * MOBA_BLOCK]


def _project(x2, g1, wqk, wvt, gqk, gsum):
    T = x2.shape[0]
    nt = T // PROJ_TM
    nblk = PROJ_TM // MOBA_BLOCK
    row = lambda i: (i, 0)
    const = lambda i: (0, 0)
    qk_shape = jax.ShapeDtypeStruct((T, 512), jnp.bfloat16)
    vt_shape = jax.ShapeDtypeStruct((T // MOBA_BLOCK, 512, MOBA_BLOCK), jnp.bfloat16)
    return pl.pallas_call(
        _proj_kernel,
        grid=(nt,),
        in_specs=[
            pl.BlockSpec((PROJ_TM, 1024), row),
            pl.BlockSpec((1, 1024), const),
            pl.BlockSpec((1024, 2048), const),
            pl.BlockSpec((1024, 1024), const),
            pl.BlockSpec((1, 2048), const),
            pl.BlockSpec((LANES, LANES), const),
        ],
        out_specs=[
            pl.BlockSpec((PROJ_TM, 512), row),
            pl.BlockSpec((PROJ_TM, 512), row),
            pl.BlockSpec((PROJ_TM, 512), row),
            pl.BlockSpec((PROJ_TM, 512), row),
            pl.BlockSpec((nblk, 512, MOBA_BLOCK), lambda i: (i, 0, 0)),
            pl.BlockSpec((nblk, 512, MOBA_BLOCK), lambda i: (i, 0, 0)),
            pl.BlockSpec((SUBLANES, 512), row),
        ],
        out_shape=[qk_shape, qk_shape, qk_shape, qk_shape, vt_shape, vt_shape,
                   jax.ShapeDtypeStruct((nt * SUBLANES, 512), jnp.float32)],
        compiler_params=pltpu.CompilerParams(dimension_semantics=("arbitrary",), vmem_limit_bytes=VMEM_LIMIT),
        name="proj",
    )(x2, g1, wqk, wvt, gqk, gsum)


def _rel_qk():
    qc = lax.broadcasted_iota(jnp.int32, (MOBA_BLOCK, MOBA_BLOCK), 1)
    kr = lax.broadcasted_iota(jnp.int32, (MOBA_BLOCK, MOBA_BLOCK), 0)
    return qc - kr


def _flash_head(q_h, k_ref, vt_load, i, slope, rel, rowbias_ref):
    relf = rel.astype(jnp.float32)
    nb = relf * (-slope)
    k_own = k_ref[pl.ds(pl.multiple_of(i * MOBA_BLOCK, MOBA_BLOCK), MOBA_BLOCK), :]
    s = _dot_nt(k_own, q_h) + nb
    s = jnp.where(rel >= 0, s, NEG_INF)
    m = jnp.max(s, axis=0, keepdims=True)
    p = jnp.exp(s - m)
    l = jnp.sum(p, axis=0, keepdims=True)
    acc = _dot(vt_load(i), p.astype(jnp.bfloat16))

    def body(j, carry):
        m, l, acc = carry
        k_j = k_ref[pl.ds(pl.multiple_of(j * MOBA_BLOCK, MOBA_BLOCK), MOBA_BLOCK), :]
        c = (i - j).astype(jnp.float32) * (-slope * MOBA_BLOCK)
        if rowbias_ref is None:
            s = _dot_nt(k_j, q_h) + (nb + c)
        else:
            s = _dot_nt(k_j, q_h) + nb + (rowbias_ref[pl.ds(j, 1), :] + c)
        m_new = jnp.maximum(m, jnp.max(s, axis=0, keepdims=True))
        alpha = jnp.exp(m - m_new)
        p = jnp.exp(s - m_new)
        l = l * alpha + jnp.sum(p, axis=0, keepdims=True)
        acc = acc * alpha + _dot(vt_load(j), p.astype(jnp.bfloat16))
        return m_new, l, acc

    m, l, acc = lax.fori_loop(0, i, body, (m, l, acc))
    return acc, l


def _moba_kernel(slopes_ref, q_ref, k_ref, vt_ref, kmean_ref, o_ref, selbias_ref):
    pair = pl.program_id(1)
    i = pl.program_id(2)
    q2 = q_ref[...]
    lane = lax.broadcasted_iota(jnp.int32, q2.shape, 1)
    rel = _rel_qk()
    km_hi, km_lo = _split_bf16(kmean_ref[...])
    blk = lax.broadcasted_iota(jnp.int32, (SUBLANES, MOBA_BLOCK), 0)
    outs = []
    for hh in range(2):
        in_head = (lane >= hh * HEAD_DIM) & (lane < (hh + 1) * HEAD_DIM)
        q_h = jnp.where(in_head, q2, jnp.zeros_like(q2))
        gate = _dot_nt(km_hi, q_h) + _dot_nt(km_lo, q_h)
        cnt = jnp.zeros((SUBLANES, MOBA_BLOCK), jnp.int32)
        for mblk in range(SUBLANES):
            gm = gate[mblk:mblk + 1, :]
            beats = (gm > gate) | ((gm == gate) & (mblk < blk))
            cnt = cnt + jnp.where(beats, jnp.where(mblk < i, 1, 0), 0)
        sel = (blk < i) & (cnt < MOBA_TOPK)
        selbias_ref[...] = jnp.where(sel, 0.0, NEG_INF)
        slope = slopes_ref[pair, hh]
        vt_load = lambda j, hh=hh: vt_ref[j, hh * HEAD_DIM:(hh + 1) * HEAD_DIM, :]
        acc, l = _flash_head(q_h, k_ref, vt_load, i, slope, rel, selbias_ref)
        outs.append(acc / l)
    o_t = jnp.concatenate(outs, axis=0)
    o_ref[...] = o_t.T.astype(o_ref.dtype)


def _diff_kernel(slopes_ref, lam_ref, q_ref, k_ref, vt_ref, gsub_ref, o_ref, *, lam_init):
    head = pl.program_id(1)
    i = pl.program_id(2)
    q2 = q_ref[...]
    lane = lax.broadcasted_iota(jnp.int32, q2.shape, 1)
    rel = _rel_qk()
    slope = slopes_ref[head, 0]
    lp = lam_ref[...]
    lam = (jnp.exp(jnp.sum(lp[0:1] * lp[1:2], axis=-1, keepdims=True))
           - jnp.exp(jnp.sum(lp[2:3] * lp[3:4], axis=-1, keepdims=True)) + lam_init)
    vt_load = lambda j: vt_ref[j]
    parts = []
    for hh in range(2):
        in_head = (lane >= hh * HEAD_DIM) & (lane < (hh + 1) * HEAD_DIM)
        q_h = jnp.where(in_head, q2, jnp.zeros_like(q2))
        acc, l = _flash_head(q_h, k_ref, vt_load, i, slope, rel, None)
        parts.append(acc / l)
    o_t = parts[0] - lam * parts[1]
    ms = jnp.mean(o_t * o_t, axis=0, keepdims=True)
    o_t = o_t * lax.rsqrt(ms + RMS_EPS) * gsub_ref[...] * (1.0 - lam_init)
    o_ref[...] = o_t.T.astype(o_ref.dtype)


def _attention_specs(nq):
    q_spec = pl.BlockSpec((MOBA_BLOCK, LANES), lambda b, p, i: (b * nq + i, p))
    k_spec = pl.BlockSpec((nq * MOBA_BLOCK, LANES), lambda b, p, i: (b, p))
    vt_spec = pl.BlockSpec((nq, LANES, MOBA_BLOCK), lambda b, p, i: (b, p, 0))
    return q_spec, k_spec, vt_spec


def _moba_attention(q, k, vt, kmean, slopes, batch, seq):
    nq = seq // MOBA_BLOCK
    q_spec, k_spec, vt_spec = _attention_specs(nq)
    return pl.pallas_call(
        _moba_kernel,
        grid=(batch, MOBA_HEADS // 2, nq),
        in_specs=[
            pl.BlockSpec(memory_space=pltpu.SMEM),
            q_spec, k_spec, vt_spec,
            pl.BlockSpec((SUBLANES, LANES), lambda b, p, i: (b, p)),
        ],
        out_specs=q_spec,
        out_shape=jax.ShapeDtypeStruct(q.shape, jnp.bfloat16),
        scratch_shapes=[pltpu.VMEM((SUBLANES, MOBA_BLOCK), jnp.float32)],
        compiler_params=pltpu.CompilerParams(
            dimension_semantics=("arbitrary", "arbitrary", "arbitrary"), vmem_limit_bytes=VMEM_LIMIT),
        name="moba",
    )(slopes, q, k, vt, kmean)


def _diff_attention(q, k, vt, lam_p, gsub, slopes, lam_init, batch, seq):
    nq = seq // MOBA_BLOCK
    q_spec, k_spec, vt_spec = _attention_specs(nq)
    return pl.pallas_call(
        functools.partial(_diff_kernel, lam_init=lam_init),
        grid=(batch, DIFF_HEADS, nq),
        in_specs=[
            pl.BlockSpec(memory_space=pltpu.SMEM),
            pl.BlockSpec((4, HEAD_DIM), lambda b, p, i: (0, 0)),
            q_spec, k_spec, vt_spec,
            pl.BlockSpec((LANES, MOBA_BLOCK), lambda b, p, i: (0, 0)),
        ],
        out_specs=q_spec,
        out_shape=jax.ShapeDtypeStruct(q.shape, jnp.bfloat16),
        compiler_params=pltpu.CompilerParams(
            dimension_semantics=("arbitrary", "arbitrary", "arbitrary"), vmem_limit_bytes=VMEM_LIMIT),
        name="diff",
    )(slopes, lam_p, q, k, vt, gsub)


def _topk_rows(sc, k):
    n = sc.shape[0]
    row = lax.broadcasted_iota(jnp.int32, sc.shape, 0).astype(jnp.float32)
    vals, idxs = [], []
    for _ in range(k):
        m = jnp.max(sc, axis=0, keepdims=True)
        idx = jnp.min(jnp.where(sc == m, row, float(n)), axis=0, keepdims=True)
        vals.append(m)
        idxs.append(idx)
        sc = jnp.where(row == idx, NEG_INF, sc)
    return vals, idxs


def _mix_kernel(x_ref, am_ref, ad_ref, wo_ref, g2_ref, wq_ref, keys_ref,
                x1_ref, xn_ref, exp_ref, gate_ref):
    x1 = x_ref[...] + _dot(am_ref[...], wo_ref[0:512, :]) + _dot(ad_ref[...], wo_ref[512:1024, :])
    x1_ref[...] = x1
    ms = jnp.mean(x1 * x1, axis=-1, keepdims=True)
    xn = x1 * lax.rsqrt(ms + RMS_EPS) * g2_ref[...]
    xn_ref[...] = xn
    qq = _dot(xn.astype(jnp.bfloat16), wq_ref[...]).astype(jnp.bfloat16)
    exp_rows, gate_rows = [], []
    for h in range(PEER_HEADS):
        tops = []
        for half in range(2):
            s = h * 2 + half
            sc = _dot_nt(keys_ref[s], qq[:, s * LANES:(s + 1) * LANES])
            tops.append(_topk_rows(sc, PEER_TOPK))
        (s1, i1), (s2, i2) = tops
        s2a = jnp.concatenate(s2, axis=0)
        i2a = jnp.concatenate(i2, axis=0)
        cand = jnp.concatenate([s1[a] + s2a for a in range(PEER_TOPK)], axis=0)
        cidx = jnp.concatenate([i1[a] * float(PEER_N_KEYS) + i2a for a in range(PEER_TOPK)], axis=0)
        row = lax.broadcasted_iota(jnp.int32, cand.shape, 0).astype(jnp.float32)
        best, experts = [], []
        for _ in range(PEER_TOPK):
            m = jnp.max(cand, axis=0, keepdims=True)
            pos = jnp.min(jnp.where(cand == m, row, float(PEER_TOPK * PEER_TOPK)), axis=0, keepdims=True)
            hit = row == pos
            best.append(m)
            experts.append(jnp.sum(jnp.where(hit, cidx, 0.0), axis=0, keepdims=True))
            cand = jnp.where(hit, NEG_INF, cand)
        b = jnp.concatenate(best, axis=0)
        e = jnp.exp(b - b[0:1])
        gate_rows.append(e / jnp.sum(e, axis=0, keepdims=True))
        exp_rows.append(jnp.concatenate(experts, axis=0))
    exp_ref[...] = jnp.concatenate(exp_rows, axis=0).T.astype(jnp.int32)
    gate_ref[...] = jnp.concatenate(gate_rows, axis=0).T


def _mix(x2, am, ad, wo, g2, wq, keys):
    T = x2.shape[0]
    row = lambda i: (i, 0)
    const = lambda i: (0, 0)
    return pl.pallas_call(
        _mix_kernel,
        grid=(T // MIX_TM,),
        in_specs=[
            pl.BlockSpec((MIX_TM, 1024), row),
            pl.BlockSpec((MIX_TM, 512), row),
            pl.BlockSpec((MIX_TM, 512), row),
            pl.BlockSpec((1024, 1024), const),
            pl.BlockSpec((1, 1024), const),
            pl.BlockSpec((1024, 2048), const),
            pl.BlockSpec((2 * PEER_HEADS, PEER_N_KEYS, LANES), lambda i: (0, 0, 0)),
        ],
        out_specs=[
            pl.BlockSpec((MIX_TM, 1024), row),
            pl.BlockSpec((MIX_TM, 1024), row),
            pl.BlockSpec((MIX_TM, LANES), row),
            pl.BlockSpec((MIX_TM, LANES), row),
        ],
        out_shape=[
            jax.ShapeDtypeStruct((T, 1024), jnp.float32),
            jax.ShapeDtypeStruct((T, 1024), jnp.float32),
            jax.ShapeDtypeStruct((T, LANES), jnp.int32),
            jax.ShapeDtypeStruct((T, LANES), jnp.float32),
        ],
        compiler_params=pltpu.CompilerParams(dimension_semantics=("arbitrary",), vmem_limit_bytes=VMEM_LIMIT),
        name="mix",
    )(x2, am, ad, wo, g2, wq, keys)


def _pack_kernel(t_ref, o_ref):
    t = t_ref[...]
    lo = pltpu.bitcast(t[:, 0:512].astype(jnp.bfloat16).astype(jnp.float32), jnp.int32)
    hi = pltpu.bitcast(t[:, 512:1024].astype(jnp.bfloat16).astype(jnp.float32), jnp.int32)
    o_ref[...] = lax.shift_right_logical(lo, 16) | (hi & jnp.int32(-65536))


def _pack_table(tab):
    n = tab.shape[0]
    packed = pl.pallas_call(
        _pack_kernel,
        grid=(n // PACK_ROWS,),
        in_specs=[pl.BlockSpec((PACK_ROWS, 1024), lambda i: (i, 0))],
        out_specs=pl.BlockSpec((PACK_ROWS, 512), lambda i: (i, 0)),
        out_shape=jax.ShapeDtypeStruct((n, 512), jnp.int32),
        compiler_params=pltpu.CompilerParams(dimension_semantics=("arbitrary",)),
        name="pack",
    )(tab)
    return packed.reshape(n, 4, LANES)


def _decode(w):
    lo = pltpu.bitcast(w << 16, jnp.float32)
    hi = pltpu.bitcast(w & jnp.int32(-65536), jnp.float32)
    return lo, hi


def _peer_u_kernel(exp_ref, tab_ref, x_ref, g_ref, act_ref, q_ref):
    nsel = exp_ref.shape[1]
    ones = jnp.ones((SUBLANES, LANES), jnp.bfloat16)

    def tok(t, carry):
        x = x_ref[t]
        xlo = x[0:4]
        xhi = x[4:8]
        for e in range(nsel):
            lo, hi = _decode(tab_ref[exp_ref[t, e]])
            q_ref[pl.ds(e, 1), :] = jnp.sum(lo * xlo + hi * xhi, axis=0, keepdims=True)
        q_hi, q_lo = _split_bf16(q_ref[...])
        h = (_dot_nt(ones, q_hi) + _dot_nt(ones, q_lo))[0:1]
        gelu = 0.5 * h * (1.0 + lax.erf(h * (1.0 / math.sqrt(2.0))))
        act_ref[pl.ds(t, 1), :] = gelu * g_ref[pl.ds(t, 1), :]
        return carry

    lax.fori_loop(0, x_ref.shape[0], tok, 0)


def _peer_v_kernel(exp_ref, act_ref, tab_ref, x_ref, o_ref):
    nsel = exp_ref.shape[1]

    def tok(t, carry):
        acc_lo = jnp.zeros((4, LANES), jnp.float32)
        acc_hi = jnp.zeros((4, LANES), jnp.float32)
        for e in range(nsel):
            a = act_ref[t, e]
            lo, hi = _decode(tab_ref[exp_ref[t, e]])
            acc_lo = acc_lo + a * lo
            acc_hi = acc_hi + a * hi
        o_ref[t] = x_ref[t] + jnp.concatenate([acc_lo, acc_hi], axis=0)
        return carry

    lax.fori_loop(0, x_ref.shape[0], tok, 0)


def _peer_experts(xn3, x13, experts, gates, u_packed, v_packed):
    T, nsel = experts.shape
    n_exp = u_packed.shape[0]
    grid = (T // PEER_TT,)
    smem_blk = lambda: pl.BlockSpec((PEER_TT, nsel), lambda i: (i, 0), memory_space=pltpu.SMEM)
    tab_spec = pl.BlockSpec((n_exp, 4, LANES), lambda i: (0, 0, 0), pipeline_mode=pl.Buffered(1))
    x_spec = pl.BlockSpec((PEER_TT, SUBLANES, LANES), lambda i: (i, 0, 0))
    cp = pltpu.CompilerParams(dimension_semantics=("arbitrary",), vmem_limit_bytes=VMEM_LIMIT)
    act = pl.pallas_call(
        _peer_u_kernel,
        grid=grid,
        in_specs=[smem_blk(), tab_spec, x_spec, pl.BlockSpec((PEER_TT, nsel), lambda i: (i, 0))],
        out_specs=pl.BlockSpec((PEER_TT, nsel), lambda i: (i, 0)),
        out_shape=jax.ShapeDtypeStruct((T, nsel), jnp.float32),
        scratch_shapes=[pltpu.VMEM((nsel, LANES), jnp.float32)],
        compiler_params=cp,
        name="peer_u",
    )(experts, u_packed, xn3, gates)
    return pl.pallas_call(
        _peer_v_kernel,
        grid=grid,
        in_specs=[smem_blk(), smem_blk(), tab_spec, x_spec],
        out_specs=x_spec,
        out_shape=jax.ShapeDtypeStruct((T, SUBLANES, LANES), jnp.float32),
        compiler_params=cp,
        name="peer_v",
    )(experts, act, v_packed, x13)


def _alibi_slopes(n):
    return 2.0 ** (-8.0 * np.arange(1, n + 1) / n)


def _lambda_init(layer):
    return 0.8 - 0.6 * math.exp(-0.3 * layer)


def kernel(x, norm1_g, w_in, moba_qk_norm_g, diff_qk_norm_g, diff_lambda, diff_subln_g, w_out, norm2_g,
           peer_w_query, peer_sub_keys, peer_u, peer_v):
    b, s, d = x.shape
    T = b * s
    depth = w_in.shape[0]
    mw = MOBA_HEADS * HEAD_DIM
    assert d == 1024 and s % PROJ_TM == 0 and mw == 512 and DIFF_HEADS * 2 * HEAD_DIM == 512
    f32, bf16 = jnp.float32, jnp.bfloat16
    scale = HEAD_DIM ** -0.5
    grp = lax.broadcasted_iota(jnp.int32, (LANES, LANES), 0) // HEAD_DIM
    gsum = (grp == grp.T).astype(bf16)
    moba_slopes = jnp.asarray(_alibi_slopes(MOBA_HEADS).reshape(MOBA_HEADS // 2, 2), f32)
    diff_slopes = jnp.asarray(np.repeat(_alibi_slopes(DIFF_HEADS)[:, None], 2, axis=1), f32)
    x2 = x.reshape(T, d)
    for l in range(depth):
        w = w_in[l]
        wqk = jnp.concatenate([w[:, 0:2 * mw], w[:, 3 * mw:3 * mw + 1024]], axis=1).astype(bf16)
        wvt = jnp.concatenate([w[:, 2 * mw:3 * mw], w[:, 3 * mw + 1024:]], axis=1).T.astype(bf16)
        mg, dg = moba_qk_norm_g[l].astype(f32), diff_qk_norm_g[l].astype(f32)
        gqk = jnp.concatenate([jnp.tile(mg[0] * scale, 8), jnp.tile(mg[1], 8),
                               jnp.tile(dg[0] * scale, 8), jnp.tile(dg[1], 8)])[None, :]
        mq, mk, dq, dk, mvt, dvt, kmean = _project(x2, norm1_g[l][None, :].astype(f32), wqk, wvt, gqk, gsum)
        nblk = PROJ_TM // MOBA_BLOCK
        kmean = kmean.reshape(T // PROJ_TM, SUBLANES, 512)[:, :nblk].reshape(T // MOBA_BLOCK, 512)
        moba_o = _moba_attention(mq, mk, mvt, kmean, moba_slopes, b, s)
        gsub = jnp.broadcast_to(diff_subln_g[l].astype(f32)[:, None], (LANES, MOBA_BLOCK))
        diff_o = _diff_attention(dq, dk, dvt, diff_lambda[l].astype(f32), gsub, diff_slopes,
                                 _lambda_init(l), b, s)
        keys = peer_sub_keys[l].reshape(2 * PEER_HEADS, PEER_N_KEYS, LANES).astype(bf16)
        x1, xn, experts, gates = _mix(x2, moba_o, diff_o, w_out[l].astype(bf16), norm2_g[l][None, :].astype(f32),
                                      peer_w_query[l].astype(bf16), keys)
        out3 = _peer_experts(xn.reshape(T, SUBLANES, LANES), x1.reshape(T, SUBLANES, LANES), experts, gates,
                             _pack_table(peer_u[l]), _pack_table(peer_v[l]))
        x2 = out3.reshape(T, d)
    return x2.reshape(b, s, d)
```

```python
import functools
import math

import jax
import jax.numpy as jnp
import numpy as np
from jax import lax
from jax.experimental import pallas as pl
from jax.experimental.pallas import tpu as pltpu

HEAD_DIM = 64
MOBA_HEADS = 8
MOBA_BLOCK = 256
MOBA_TOPK = 3
DIFF_HEADS = 4
PEER_HEADS = 8
PEER_N_KEYS = 128
PEER_TOPK = 16
RMS_EPS = 1e-6

LANES = 128
SUBLANES = 8
VMEM_LIMIT = 48 * 1024 * 1024

PROJ_TM = 512
MIX_TM = 256
PEER_TT = 64
PACK_ROWS = 512

NEG_INF = float("-inf")
_NT = (((1,), (1,)), ((), ()))


def _dot(a, b):
    return jnp.dot(a, b, preferred_element_type=jnp.float32)


def _dot_nt(a, b):
    return lax.dot_general(a, b, _NT, preferred_element_type=jnp.float32)


def _split_bf16(x):
    hi = x.astype(jnp.bfloat16)
    lo = (x - hi.astype(jnp.float32)).astype(jnp.bfloat16)
    return hi, lo


def _proj_kernel(x_ref, g1_ref, wqk_ref, wvt_ref, gqk_ref, gsum_ref,
                 mq_ref, mk_ref, dq_ref, dk_ref, mvt_ref, dvt_ref, kmean_ref):
    x = x_ref[...]
    ms = jnp.mean(x * x, axis=-1, keepdims=True)
    h = (x * lax.rsqrt(ms + RMS_EPS) * g1_ref[...]).astype(jnp.bfloat16)
    gsum = gsum_ref[...]
    outs = (mq_ref, mk_ref, dq_ref, dk_ref)
    for sec in range(4):
        for half in range(2):
            c0 = sec * 512 + half * 256
            p = _dot(h, wqk_ref[:, c0:c0 + 256])
            for s in range(2):
                ps = p[:, s * LANES:(s + 1) * LANES]
                sq_hi, sq_lo = _split_bf16(ps * ps)
                ssq = _dot(sq_hi, gsum) + _dot(sq_lo, gsum)
                g = gqk_ref[:, c0 + s * LANES:c0 + (s + 1) * LANES]
                y = ps * lax.rsqrt(ssq * (1.0 / HEAD_DIM) + RMS_EPS) * g
                l0 = half * 256 + s * LANES
                outs[sec][:, l0:l0 + LANES] = y.astype(jnp.bfloat16)
                if sec == 1:
                    for blk in range(PROJ_TM // MOBA_BLOCK):
                        kmean_ref[blk:blk + 1, l0:l0 + LANES] = jnp.mean(
                            y[blk * MOBA_BLOCK:(blk + 1) * MOBA_BLOCK], axis=0, keepdims=True)
    kmean_ref[PROJ_TM // MOBA_BLOCK:, :] = jnp.zeros((SUBLANES - PROJ_TM // MOBA_BLOCK, 512), jnp.float32)
    vt = _dot_nt(wvt_ref[...], h).astype(jnp.bfloat16)
    for blk in range(PROJ_TM // MOBA_BLOCK):
        mvt_ref[blk] = vt[0:512, blk * MOBA_BLOCK:(blk + 1) * MOBA_BLOCK]
        dvt_ref[blk] = vt[512:1024, blk * MOBA_BLOCK:(blk + 1) * MOBA_BLOCK]


def _project(x2, g1, wqk, wvt, gqk, gsum):
    T = x2.shape[0]
    nt = T // PROJ_TM
    nblk = PROJ_TM // MOBA_BLOCK
    row = lambda i: (i, 0)
    const = lambda i: (0, 0)
    qk_shape = jax.ShapeDtypeStruct((T, 512), jnp.bfloat16)
    vt_shape = jax.ShapeDtypeStruct((T // MOBA_BLOCK, 512, MOBA_BLOCK), jnp.bfloat16)
    return pl.pallas_call(
        _proj_kernel,
        grid=(nt,),
        in_specs=[
            pl.BlockSpec((PROJ_TM, 1024), row),
            pl.BlockSpec((1, 1024), const),
            pl.BlockSpec((1024, 2048), const),
            pl.BlockSpec((1024, 1024), const),
            pl.BlockSpec((1, 2048), const),
            pl.BlockSpec((LANES, LANES), const),
        ],
        out_specs=[
            pl.BlockSpec((PROJ_TM, 512), row),
            pl.BlockSpec((PROJ_TM, 512), row),
            pl.BlockSpec((PROJ_TM, 512), row),
            pl.BlockSpec((PROJ_TM, 512), row),
            pl.BlockSpec((nblk, 512, MOBA_BLOCK), lambda i: (i, 0, 0)),
            pl.BlockSpec((nblk, 512, MOBA_BLOCK), lambda i: (i, 0, 0)),
            pl.BlockSpec((SUBLANES, 512), row),
        ],
        out_shape=[qk_shape, qk_shape, qk_shape, qk_shape, vt_shape, vt_shape,
                   jax.ShapeDtypeStruct((nt * SUBLANES, 512), jnp.float32)],
        compiler_params=pltpu.CompilerParams(dimension_semantics=("arbitrary",), vmem_limit_bytes=VMEM_LIMIT),
        name="proj",
    )(x2, g1, wqk, wvt, gqk, gsum)


def _rel_qk():
    qc = lax.broadcasted_iota(jnp.int32, (MOBA_BLOCK, MOBA_BLOCK), 1)
    kr = lax.broadcasted_iota(jnp.int32, (MOBA_BLOCK, MOBA_BLOCK), 0)
    return qc - kr


def _attend(q_h, k_ref, vt_load, i, slope, rel, selbias):
    nb = rel.astype(jnp.float32) * (-slope)
    ts, rowbs, colmax = [], [], []
    for j in range(i + 1):
        t = _dot_nt(k_ref[j * MOBA_BLOCK:(j + 1) * MOBA_BLOCK, :], q_h) + nb
        if j == i:
            t = jnp.where(rel >= 0, t, NEG_INF)
            rowb = jnp.zeros((1, MOBA_BLOCK), jnp.float32)
        else:
            rowb = jnp.full((1, MOBA_BLOCK), (i - j) * MOBA_BLOCK, jnp.float32) * (-slope)
            if selbias is not None:
                rowb = rowb + selbias[j:j + 1, :]
        ts.append(t)
        rowbs.append(rowb)
        colmax.append(jnp.max(t, axis=0, keepdims=True) + rowb)
    m = functools.reduce(jnp.maximum, colmax)
    l, acc = None, None
    for j in range(i + 1):
        p = jnp.exp(ts[j] - (m - rowbs[j]))
        lj = jnp.sum(p, axis=0, keepdims=True)
        aj = _dot(vt_load(j), p.astype(jnp.bfloat16))
        l = lj if l is None else l + lj
        acc = aj if acc is None else acc + aj
    return acc / l


def _head_queries(q2):
    lane = lax.broadcasted_iota(jnp.int32, q2.shape, 1)
    return [jnp.where((lane >= hh * HEAD_DIM) & (lane < (hh + 1) * HEAD_DIM), q2, jnp.zeros_like(q2))
            for hh in range(2)]


def _moba_kernel(slopes_ref, q_ref, k_ref, vt_ref, kmean_ref, o_ref):
    pair = pl.program_id(1)
    i = pl.program_id(2)
    nq = vt_ref.shape[0]
    q_hs = _head_queries(q_ref[...])
    rel = _rel_qk()
    km_hi, km_lo = _split_bf16(kmean_ref[...])
    blk = lax.broadcasted_iota(jnp.int32, (SUBLANES, MOBA_BLOCK), 0)
    selbias = []
    for q_h in q_hs:
        gate = _dot_nt(km_hi, q_h) + _dot_nt(km_lo, q_h)
        cnt = jnp.zeros((SUBLANES, MOBA_BLOCK), jnp.int32)
        for mblk in range(SUBLANES):
            gm = gate[mblk:mblk + 1, :]
            beats = (gm > gate) | ((gm == gate) & (mblk < blk))
            cnt = cnt + jnp.where(beats, jnp.where(mblk < i, 1, 0), 0)
        selbias.append(jnp.where((blk < i) & (cnt < MOBA_TOPK), 0.0, NEG_INF))
    for ii in range(nq):
        @pl.when(i == ii)
        def _(ii=ii):
            outs = []
            for hh in range(2):
                vt_load = lambda j, hh=hh: vt_ref[j, hh * HEAD_DIM:(hh + 1) * HEAD_DIM, :]
                outs.append(_attend(q_hs[hh], k_ref, vt_load, ii, slopes_ref[pair, hh], rel, selbias[hh]))
            o_t = jnp.concatenate(outs, axis=0)
            o_ref[...] = o_t.T.astype(o_ref.dtype)


def _diff_kernel(slopes_ref, lam_ref, q_ref, k_ref, vt_ref, gsub_ref, o_ref, *, lam_init):
    head = pl.program_id(1)
    i = pl.program_id(2)
    nq = vt_ref.shape[0]
    q_hs = _head_queries(q_ref[...])
    rel = _rel_qk()
    slope = slopes_ref[head, 0]
    lp = lam_ref[...]
    lam = (jnp.exp(jnp.sum(lp[0:1] * lp[1:2], axis=-1, keepdims=True))
           - jnp.exp(jnp.sum(lp[2:3] * lp[3:4], axis=-1, keepdims=True)) + lam_init)
    vt_load = lambda j: vt_ref[j]
    for ii in range(nq):
        @pl.when(i == ii)
        def _(ii=ii):
            parts = [_attend(q_h, k_ref, vt_load, ii, slope, rel, None) for q_h in q_hs]
            o_t = parts[0] - lam * parts[1]
            ms = jnp.mean(o_t * o_t, axis=0, keepdims=True)
            o_t = o_t * lax.rsqrt(ms + RMS_EPS) * gsub_ref[...] * (1.0 - lam_init)
            o_ref[...] = o_t.T.astype(o_ref.dtype)


def _attention_specs(nq):
    q_spec = pl.BlockSpec((MOBA_BLOCK, LANES), lambda b, p, i: (b * nq + i, p))
    k_spec = pl.BlockSpec((nq * MOBA_BLOCK, LANES), lambda b, p, i: (b, p))
    vt_spec = pl.BlockSpec((nq, LANES, MOBA_BLOCK), lambda b, p, i: (b, p, 0))
    return q_spec, k_spec, vt_spec


def _moba_attention(q, k, vt, kmean, slopes, batch, seq):
    nq = seq // MOBA_BLOCK
    q_spec, k_spec, vt_spec = _attention_specs(nq)
    return pl.pallas_call(
        _moba_kernel,
        grid=(batch, MOBA_HEADS // 2, nq),
        in_specs=[
            pl.BlockSpec(memory_space=pltpu.SMEM),
            q_spec, k_spec, vt_spec,
            pl.BlockSpec((SUBLANES, LANES), lambda b, p, i: (b, p)),
        ],
        out_specs=q_spec,
        out_shape=jax.ShapeDtypeStruct(q.shape, jnp.bfloat16),
        compiler_params=pltpu.CompilerParams(
            dimension_semantics=("arbitrary", "arbitrary", "arbitrary"), vmem_limit_bytes=VMEM_LIMIT),
        name="moba",
    )(slopes, q, k, vt, kmean)


def _diff_attention(q, k, vt, lam_p, gsub, slopes, lam_init, batch, seq):
    nq = seq // MOBA_BLOCK
    q_spec, k_spec, vt_spec = _attention_specs(nq)
    return pl.pallas_call(
        functools.partial(_diff_kernel, lam_init=lam_init),
        grid=(batch, DIFF_HEADS, nq),
        in_specs=[
            pl.BlockSpec(memory_space=pltpu.SMEM),
            pl.BlockSpec((4, HEAD_DIM), lambda b, p, i: (0, 0)),
            q_spec, k_spec, vt_spec,
            pl.BlockSpec((LANES, MOBA_BLOCK), lambda b, p, i: (0, 0)),
        ],
        out_specs=q_spec,
        out_shape=jax.ShapeDtypeStruct(q.shape, jnp.bfloat16),
        compiler_params=pltpu.CompilerParams(
            dimension_semantics=("arbitrary", "arbitrary", "arbitrary"), vmem_limit_bytes=VMEM_LIMIT),
        name="diff",
    )(slopes, lam_p, q, k, vt, gsub)


def _knockout_topk(sc, k, payload=None):
    n = sc.shape[0]
    row = lax.broadcasted_iota(jnp.int32, sc.shape, 0).astype(jnp.float32)
    vals, picks = [], []
    for _ in range(k):
        m = jnp.max(sc, axis=0, keepdims=True)
        idx = jnp.min(jnp.where(sc == m, row, float(n)), axis=0, keepdims=True)
        hit = row == idx
        vals.append(m)
        picks.append(idx if payload is None else jnp.sum(jnp.where(hit, payload, 0.0), axis=0, keepdims=True))
        sc = jnp.where(hit, NEG_INF, sc)
    return vals, picks


def _mix_kernel(x_ref, am_ref, ad_ref, wo_ref, g2_ref, wq_ref, keys_ref,
                x1_ref, xn_ref, exp_ref, gate_ref):
    x1 = x_ref[...] + _dot(am_ref[...], wo_ref[0:512, :]) + _dot(ad_ref[...], wo_ref[512:1024, :])
    x1_ref[...] = x1
    ms = jnp.mean(x1 * x1, axis=-1, keepdims=True)
    xn = x1 * lax.rsqrt(ms + RMS_EPS) * g2_ref[...]
    xn_ref[...] = xn
    qq = _dot(xn.astype(jnp.bfloat16), wq_ref[...]).astype(jnp.bfloat16)
    tm = qq.shape[0]
    exp_rows, gate_rows = [], []
    for h in range(PEER_HEADS):
        tops = []
        for half in range(2):
            s = h * 2 + half
            sc = _dot_nt(keys_ref[s], qq[:, s * LANES:(s + 1) * LANES])
            tops.append(_knockout_topk(sc, PEER_TOPK))
        (s1, i1), (s2, i2) = tops
        s2a = jnp.concatenate(s2, axis=0)
        i2a = jnp.concatenate(i2, axis=0)
        cand, cidx = [], []
        for a in range(PEER_TOPK):
            nb = PEER_TOPK // (a + 1)
            cand.append(s1[a] + s2a[0:nb])
            cidx.append(i1[a] * float(PEER_N_KEYS) + i2a[0:nb])
        n_cand = sum(c.shape[0] for c in cand)
        pad = -n_cand % SUBLANES
        cand.append(jnp.full((pad, tm), NEG_INF, jnp.float32))
        cidx.append(jnp.zeros((pad, tm), jnp.float32))
        best, experts = _knockout_topk(jnp.concatenate(cand, axis=0), PEER_TOPK,
                                       payload=jnp.concatenate(cidx, axis=0))
        b = jnp.concatenate(best, axis=0)
        e = jnp.exp(b - b[0:1])
        gate_rows.append(e / jnp.sum(e, axis=0, keepdims=True))
        exp_rows.append(jnp.concatenate(experts, axis=0))
    exp_ref[...] = jnp.concatenate(exp_rows, axis=0).T.astype(jnp.int32)
    gate_ref[...] = jnp.concatenate(gate_rows, axis=0).T


def _mix(x2, am, ad, wo, g2, wq, keys):
    T = x2.shape[0]
    row = lambda i: (i, 0)
    const = lambda i: (0, 0)
    return pl.pallas_call(
        _mix_kernel,
        grid=(T // MIX_TM,),
        in_specs=[
            pl.BlockSpec((MIX_TM, 1024), row),
            pl.BlockSpec((MIX_TM, 512), row),
            pl.BlockSpec((MIX_TM, 512), row),
            pl.BlockSpec((1024, 1024), const),
            pl.BlockSpec((1, 1024), const),
            pl.BlockSpec((1024, 2048), const),
            pl.BlockSpec((2 * PEER_HEADS, PEER_N_KEYS, LANES), lambda i: (0, 0, 0)),
        ],
        out_specs=[
            pl.BlockSpec((MIX_TM, 1024), row),
            pl.BlockSpec((MIX_TM, 1024), row),
            pl.BlockSpec((MIX_TM, LANES), row),
            pl.BlockSpec((MIX_TM, LANES), row),
        ],
        out_shape=[
            jax.ShapeDtypeStruct((T, 1024), jnp.float32),
            jax.ShapeDtypeStruct((T, 1024), jnp.float32),
            jax.ShapeDtypeStruct((T, LANES), jnp.int32),
            jax.ShapeDtypeStruct((T, LANES), jnp.float32),
        ],
        compiler_params=pltpu.CompilerParams(dimension_semantics=("arbitrary",), vmem_limit_bytes=VMEM_LIMIT),
        name="mix",
    )(x2, am, ad, wo, g2, wq, keys)


def _pack_kernel(t_ref, o_ref):
    t = t_ref[...]
    lo = pltpu.bitcast(t[:, 0:512].astype(jnp.bfloat16).astype(jnp.float32), jnp.int32)
    hi = pltpu.bitcast(t[:, 512:1024].astype(jnp.bfloat16).astype(jnp.float32), jnp.int32)
    o_ref[...] = lax.shift_right_logical(lo, 16) | (hi & jnp.int32(-65536))


def _pack_table(tab):
    n = tab.shape[0]
    packed = pl.pallas_call(
        _pack_kernel,
        grid=(n // PACK_ROWS,),
        in_specs=[pl.BlockSpec((PACK_ROWS, 1024), lambda i: (i, 0))],
        out_specs=pl.BlockSpec((PACK_ROWS, 512), lambda i: (i, 0)),
        out_shape=jax.ShapeDtypeStruct((n, 512), jnp.int32),
        compiler_params=pltpu.CompilerParams(dimension_semantics=("arbitrary",)),
        name="pack",
    )(tab)
    return packed.reshape(n, 4, LANES)


def _decode(w):
    lo = pltpu.bitcast(w << 16, jnp.float32)
    hi = pltpu.bitcast(w & jnp.int32(-65536), jnp.float32)
    return lo, hi


def _peer_u_kernel(exp_ref, tab_ref, x_ref, g_ref, act_ref, q0_ref, q1_ref, qs_ref):
    ntok, nsel = exp_ref.shape
    ones = jnp.ones((SUBLANES, LANES), jnp.bfloat16)

    def products(t, q_ref):
        x = x_ref[t]
        xlo = x[0:4]
        xhi = x[4:8]
        for e in range(nsel):
            lo, hi = _decode(tab_ref[exp_ref[t, e]])
            q_ref[pl.ds(4 * e, 4), :] = lo * xlo + hi * xhi

    def fold(t, q_ref):
        q = q_ref[pl.ds(0, nsel, stride=4), :]
        for s in range(1, 4):
            q = q + q_ref[pl.ds(s, nsel, stride=4), :]
        qs_ref[pl.ds(pl.multiple_of(t * nsel, nsel), nsel), :] = q

    def tok2(i, carry):
        products(2 * i, q0_ref)
        fold(2 * i, q0_ref)
        products(2 * i + 1, q1_ref)
        fold(2 * i + 1, q1_ref)
        return carry

    lax.fori_loop(0, ntok // 2, tok2, 0)

    def group(g, carry):
        rows8 = SUBLANES * nsel
        q_hi, q_lo = _split_bf16(qs_ref[pl.ds(pl.multiple_of(g * rows8, rows8), rows8), :])
        row = (_dot_nt(ones, q_hi) + _dot_nt(ones, q_lo))[0:1]
        h = jnp.concatenate([row[:, k * nsel:(k + 1) * nsel] for k in range(SUBLANES)], axis=0)
        gelu = 0.5 * h * (1.0 + lax.erf(h * (1.0 / math.sqrt(2.0))))
        rows = pl.ds(pl.multiple_of(g * SUBLANES, SUBLANES), SUBLANES)
        act_ref[rows, :] = gelu * g_ref[rows, :]
        return carry

    lax.fori_loop(0, ntok // SUBLANES, group, 0)


def _peer_v_kernel(exp_ref, act_ref, tab_ref, x_ref, o_ref):
    nsel = exp_ref.shape[1]

    def tok(t, carry):
        acc_lo = jnp.zeros((4, LANES), jnp.float32)
        acc_hi = jnp.zeros((4, LANES), jnp.float32)
        for e in range(nsel):
            a = act_ref[t, e]
            lo, hi = _decode(tab_ref[exp_ref[t, e]])
            acc_lo = acc_lo + a * lo
            acc_hi = acc_hi + a * hi
        o_ref[t] = x_ref[t] + jnp.concatenate([acc_lo, acc_hi], axis=0)
        return carry

    lax.fori_loop(0, x_ref.shape[0], tok, 0)


def _peer_experts(xn3, x13, experts, gates, u_packed, v_packed):
    T, nsel = experts.shape
    n_exp = u_packed.shape[0]
    assert nsel == LANES and PEER_TT % SUBLANES == 0
    grid = (T // PEER_TT,)
    smem_blk = lambda: pl.BlockSpec((PEER_TT, nsel), lambda i: (i, 0), memory_space=pltpu.SMEM)
    tab_spec = pl.BlockSpec((n_exp, 4, LANES), lambda i: (0, 0, 0), pipeline_mode=pl.Buffered(1))
    x_spec = pl.BlockSpec((PEER_TT, SUBLANES, LANES), lambda i: (i, 0, 0))
    cp = pltpu.CompilerParams(dimension_semantics=("arbitrary",), vmem_limit_bytes=VMEM_LIMIT)
    act = pl.pallas_call(
        _peer_u_kernel,
        grid=grid,
        in_specs=[smem_blk(), tab_spec, x_spec, pl.BlockSpec((PEER_TT, nsel), lambda i: (i, 0))],
        out_specs=pl.BlockSpec((PEER_TT, nsel), lambda i: (i, 0)),
        out_shape=jax.ShapeDtypeStruct((T, nsel), jnp.float32),
        scratch_shapes=[pltpu.VMEM((4 * nsel, LANES), jnp.float32), pltpu.VMEM((4 * nsel, LANES), jnp.float32),
                        pltpu.VMEM((PEER_TT * nsel, LANES), jnp.float32)],
        compiler_params=cp,
        name="peer_u",
    )(experts, u_packed, xn3, gates)
    return pl.pallas_call(
        _peer_v_kernel,
        grid=grid,
        in_specs=[smem_blk(), smem_blk(), tab_spec, x_spec],
        out_specs=x_spec,
        out_shape=jax.ShapeDtypeStruct((T, SUBLANES, LANES), jnp.float32),
        compiler_params=cp,
        name="peer_v",
    )(experts, act, v_packed, x13)


def _alibi_slopes(n):
    return 2.0 ** (-8.0 * np.arange(1, n + 1) / n)


def _lambda_init(layer):
    return 0.8 - 0.6 * math.exp(-0.3 * layer)


def kernel(x, norm1_g, w_in, moba_qk_norm_g, diff_qk_norm_g, diff_lambda, diff_subln_g, w_out, norm2_g,
           peer_w_query, peer_sub_keys, peer_u, peer_v):
    b, s, d = x.shape
    T = b * s
    depth = w_in.shape[0]
    mw = MOBA_HEADS * HEAD_DIM
    assert d == 1024 and s % PROJ_TM == 0 and mw == 512 and DIFF_HEADS * 2 * HEAD_DIM == 512
    f32, bf16 = jnp.float32, jnp.bfloat16
    scale = HEAD_DIM ** -0.5
    grp = lax.broadcasted_iota(jnp.int32, (LANES, LANES), 0) // HEAD_DIM
    gsum = (grp == grp.T).astype(bf16)
    moba_slopes = jnp.asarray(_alibi_slopes(MOBA_HEADS).reshape(MOBA_HEADS // 2, 2), f32)
    diff_slopes = jnp.asarray(np.repeat(_alibi_slopes(DIFF_HEADS)[:, None], 2, axis=1), f32)
    x2 = x.reshape(T, d)
    for l in range(depth):
        w = w_in[l]
        wqk = jnp.concatenate([w[:, 0:2 * mw], w[:, 3 * mw:3 * mw + 1024]], axis=1).astype(bf16)
        wvt = jnp.concatenate([w[:, 2 * mw:3 * mw], w[:, 3 * mw + 1024:]], axis=1).T.astype(bf16)
        mg, dg = moba_qk_norm_g[l].astype(f32), diff_qk_norm_g[l].astype(f32)
        gqk = jnp.concatenate([jnp.tile(mg[0] * scale, 8), jnp.tile(mg[1], 8),
                               jnp.tile(dg[0] * scale, 8), jnp.tile(dg[1], 8)])[None, :]
        mq, mk, dq, dk, mvt, dvt, kmean = _project(x2, norm1_g[l][None, :].astype(f32), wqk, wvt, gqk, gsum)
        nblk = PROJ_TM // MOBA_BLOCK
        kmean = kmean.reshape(T // PROJ_TM, SUBLANES, 512)[:, :nblk].reshape(T // MOBA_BLOCK, 512)
        moba_o = _moba_attention(mq, mk, mvt, kmean, moba_slopes, b, s)
        gsub = jnp.broadcast_to(diff_subln_g[l].astype(f32)[:, None], (LANES, MOBA_BLOCK))
        diff_o = _diff_attention(dq, dk, dvt, diff_lambda[l].astype(f32), gsub, diff_slopes,
                                 _lambda_init(l), b, s)
        keys = peer_sub_keys[l].reshape(2 * PEER_HEADS, PEER_N_KEYS, LANES).astype(bf16)
        x1, xn, experts, gates = _mix(x2, moba_o, diff_o, w_out[l].astype(bf16), norm2_g[l][None, :].astype(f32),
                                      peer_w_query[l].astype(bf16), keys)
        out3 = _peer_experts(xn.reshape(T, SUBLANES, LANES), x1.reshape(T, SUBLANES, LANES), experts, gates,
                             _pack_table(peer_u[l]), _pack_table(peer_v[l]))
        x2 = out3.reshape(T, d)
    return x2.reshape(b, s, d)
```

```python
import functools
import math

import jax
import jax.numpy as jnp
import numpy as np
from jax import lax
from jax.experimental import pallas as pl
from jax.experimental.pallas import tpu as pltpu
from jax.experimental.pallas import tpu_sc as plsc

HEAD_DIM = 64
MOBA_HEADS = 8
MOBA_BLOCK = 256
MOBA_TOPK = 3
DIFF_HEADS = 4
PEER_HEADS = 8
PEER_N_KEYS = 128
PEER_TOPK = 16
RMS_EPS = 1e-6

LANES = 128
SUBLANES = 8
VMEM_LIMIT = 48 * 1024 * 1024

PROJ_TM = 512
MIX_TM = 256
PEER_TT = 64
PACK_ROWS = 512
ATT_QBLOCKS = 2
PEER_GROUP = 8
PEER_CHUNKS = 4
PEER_SC_CHUNKS = 2
SC_LANES = 16
SC_ROWS = 32
SC_TB = 8

NEG_INF = float("-inf")
_NT = (((1,), (1,)), ((), ()))


def _dot(a, b):
    return jnp.dot(a, b, preferred_element_type=jnp.float32)


def _dot_nt(a, b):
    return lax.dot_general(a, b, _NT, preferred_element_type=jnp.float32)


def _split_bf16(x):
    hi = x.astype(jnp.bfloat16)
    lo = (x - hi.astype(jnp.float32)).astype(jnp.bfloat16)
    return hi, lo


def _proj_kernel(x_ref, g1_ref, wqk_ref, wvt_ref, gqk_ref, gsum_ref,
                 mq_ref, mk_ref, dq_ref, dk_ref, mvt_ref, dvt_ref, kmean_ref):
    x = x_ref[...]
    ms = jnp.mean(x * x, axis=-1, keepdims=True)
    h = (x * lax.rsqrt(ms + RMS_EPS) * g1_ref[...]).astype(jnp.bfloat16)
    gsum = gsum_ref[...]
    outs = (mq_ref, mk_ref, dq_ref, dk_ref)
    for sec in range(4):
        for half in range(2):
            c0 = sec * 512 + half * 256
            p = _dot(h, wqk_ref[:, c0:c0 + 256])
            for s in range(2):
                ps = p[:, s * LANES:(s + 1) * LANES]
                sq_hi, sq_lo = _split_bf16(ps * ps)
                ssq = _dot(sq_hi, gsum) + _dot(sq_lo, gsum)
                g = gqk_ref[:, c0 + s * LANES:c0 + (s + 1) * LANES]
                y = ps * lax.rsqrt(ssq * (1.0 / HEAD_DIM) + RMS_EPS) * g
                l0 = half * 256 + s * LANES
                outs[sec][:, l0:l0 + LANES] = y.astype(jnp.bfloat16)
                if sec == 1:
                    for blk in range(PROJ_TM // MOBA_BLOCK):
                        kmean_ref[blk:blk + 1, l0:l0 + LANES] = jnp.mean(
                            y[blk * MOBA_BLOCK:(blk + 1) * MOBA_BLOCK], axis=0, keepdims=True)
    kmean_ref[PROJ_TM // MOBA_BLOCK:, :] = jnp.zeros((SUBLANES - PROJ_TM // MOBA_BLOCK, 512), jnp.float32)
    vt = _dot_nt(wvt_ref[...], h).astype(jnp.bfloat16)
    for blk in range(PROJ_TM // MOBA_BLOCK):
        mvt_ref[blk] = vt[0:512, blk * MOBA_BLOCK:(blk + 1) * MOBA_BLOCK]
        dvt_ref[blk] = vt[512:1024, blk * MOBA_BLOCK:(blk + 1) * MOBA_BLOCK]


def _project(x2, g1, wqk, wvt, gqk, gsum):
    T = x2.shape[0]
    nt = T // PROJ_TM
    nblk = PROJ_TM // MOBA_BLOCK
    row = lambda i: (i, 0)
    const = lambda i: (0, 0)
    qk_shape = jax.ShapeDtypeStruct((T, 512), jnp.bfloat16)
    vt_shape = jax.ShapeDtypeStruct((T // MOBA_BLOCK, 512, MOBA_BLOCK), jnp.bfloat16)
    return pl.pallas_call(
        _proj_kernel,
        grid=(nt,),
        in_specs=[
            pl.BlockSpec((PROJ_TM, 1024), row),
            pl.BlockSpec((1, 1024), const),
            pl.BlockSpec((1024, 2048), const),
            pl.BlockSpec((1024, 1024), const),
            pl.BlockSpec((1, 2048), const),
            pl.BlockSpec((LANES, LANES), const),
        ],
        out_specs=[
            pl.BlockSpec((PROJ_TM, 512), row),
            pl.BlockSpec((PROJ_TM, 512), row),
            pl.BlockSpec((PROJ_TM, 512), row),
            pl.BlockSpec((PROJ_TM, 512), row),
            pl.BlockSpec((nblk, 512, MOBA_BLOCK), lambda i: (i, 0, 0)),
            pl.BlockSpec((nblk, 512, MOBA_BLOCK), lambda i: (i, 0, 0)),
            pl.BlockSpec((SUBLANES, 512), row),
        ],
        out_shape=[qk_shape, qk_shape, qk_shape, qk_shape, vt_shape, vt_shape,
                   jax.ShapeDtypeStruct((nt * SUBLANES, 512), jnp.float32)],
        compiler_params=pltpu.CompilerParams(dimension_semantics=("arbitrary",), vmem_limit_bytes=VMEM_LIMIT),
        name="proj",
    )(x2, g1, wqk, wvt, gqk, gsum)


def _rel_qk():
    qc = lax.broadcasted_iota(jnp.int32, (MOBA_BLOCK, MOBA_BLOCK), 1)
    kr = lax.broadcasted_iota(jnp.int32, (MOBA_BLOCK, MOBA_BLOCK), 0)
    return qc - kr


def _attend(q_h, k_ref, vt_load, i, slope, rel, selbias):
    nb = rel.astype(jnp.float32) * (-slope)
    ts, rowbs, colmax = [], [], []
    for j in range(i + 1):
        t = _dot_nt(k_ref[j * MOBA_BLOCK:(j + 1) * MOBA_BLOCK, :], q_h) + nb
        if j == i:
            t = jnp.where(rel >= 0, t, NEG_INF)
            rowb = jnp.zeros((1, MOBA_BLOCK), jnp.float32)
        else:
            rowb = jnp.full((1, MOBA_BLOCK), (i - j) * MOBA_BLOCK, jnp.float32) * (-slope)
            if selbias is not None:
                rowb = rowb + selbias[j:j + 1, :]
        ts.append(t)
        rowbs.append(rowb)
        colmax.append(jnp.max(t, axis=0, keepdims=True) + rowb)
    m = functools.reduce(jnp.maximum, colmax)
    l, acc = None, None
    for j in range(i + 1):
        p = jnp.exp(ts[j] - (m - rowbs[j]))
        lj = jnp.sum(p, axis=0, keepdims=True)
        aj = _dot(vt_load(j), p.astype(jnp.bfloat16))
        l = lj if l is None else l + lj
        acc = aj if acc is None else acc + aj
    return acc / l


def _head_queries(q2):
    lane = lax.broadcasted_iota(jnp.int32, q2.shape, 1)
    return [jnp.where((lane >= hh * HEAD_DIM) & (lane < (hh + 1) * HEAD_DIM), q2, jnp.zeros_like(q2))
            for hh in range(2)]


def _block_selection(gate, i):
    blk = lax.broadcasted_iota(jnp.int32, gate.shape, 0)
    cnt = jnp.zeros(gate.shape, jnp.int32)
    for mblk in range(i):
        gm = gate[mblk:mblk + 1, :]
        beats = (gm > gate) | ((gm == gate) & (mblk < blk))
        cnt = cnt + jnp.where(beats, 1, 0)
    return jnp.where((blk < i) & (cnt < MOBA_TOPK), 0.0, NEG_INF)


def _moba_kernel(slopes_ref, q_ref, k_ref, vt_ref, kmean_ref, o_ref):
    pair = pl.program_id(1)
    step = pl.program_id(2)
    nq = vt_ref.shape[0]
    rel = _rel_qk()
    km_hi, km_lo = _split_bf16(kmean_ref[...])
    for st in range(nq // ATT_QBLOCKS):
        @pl.when(step == st)
        def _(st=st):
            for u in range(ATT_QBLOCKS):
                i = st * ATT_QBLOCKS + u
                rows = slice(u * MOBA_BLOCK, (u + 1) * MOBA_BLOCK)
                outs = []
                for hh, q_h in enumerate(_head_queries(q_ref[rows, :])):
                    gate = _dot_nt(km_hi, q_h) + _dot_nt(km_lo, q_h)
                    vt_load = lambda j, hh=hh: vt_ref[j, hh * HEAD_DIM:(hh + 1) * HEAD_DIM, :]
                    outs.append(_attend(q_h, k_ref, vt_load, i, slopes_ref[pair, hh], rel,
                                        _block_selection(gate, i)))
                o_t = jnp.concatenate(outs, axis=0)
                o_ref[rows, :] = o_t.T.astype(o_ref.dtype)


def _diff_kernel(slopes_ref, lam_ref, q_ref, k_ref, vt_ref, gsub_ref, o_ref, *, lam_init):
    head = pl.program_id(1)
    step = pl.program_id(2)
    nq = vt_ref.shape[0]
    rel = _rel_qk()
    slope = slopes_ref[head, 0]
    lp = lam_ref[...]
    lam = (jnp.exp(jnp.sum(lp[0:1] * lp[1:2], axis=-1, keepdims=True))
           - jnp.exp(jnp.sum(lp[2:3] * lp[3:4], axis=-1, keepdims=True)) + lam_init)
    vt_load = lambda j: vt_ref[j]
    for st in range(nq // ATT_QBLOCKS):
        @pl.when(step == st)
        def _(st=st):
            for u in range(ATT_QBLOCKS):
                i = st * ATT_QBLOCKS + u
                rows = slice(u * MOBA_BLOCK, (u + 1) * MOBA_BLOCK)
                parts = [_attend(q_h, k_ref, vt_load, i, slope, rel, None)
                         for q_h in _head_queries(q_ref[rows, :])]
                o_t = parts[0] - lam * parts[1]
                ms = jnp.mean(o_t * o_t, axis=0, keepdims=True)
                o_t = o_t * lax.rsqrt(ms + RMS_EPS) * gsub_ref[...] * (1.0 - lam_init)
                o_ref[rows, :] = o_t.T.astype(o_ref.dtype)


def _attention_specs(nq):
    nst = nq // ATT_QBLOCKS
    q_spec = pl.BlockSpec((ATT_QBLOCKS * MOBA_BLOCK, LANES), lambda b, p, i: (b * nst + i, p))
    k_spec = pl.BlockSpec((nq * MOBA_BLOCK, LANES), lambda b, p, i: (b, p))
    vt_spec = pl.BlockSpec((nq, LANES, MOBA_BLOCK), lambda b, p, i: (b, p, 0))
    return q_spec, k_spec, vt_spec


def _moba_attention(q, k, vt, kmean, slopes, batch, seq):
    nq = seq // MOBA_BLOCK
    q_spec, k_spec, vt_spec = _attention_specs(nq)
    return pl.pallas_call(
        _moba_kernel,
        grid=(batch, MOBA_HEADS // 2, nq // ATT_QBLOCKS),
        in_specs=[
            pl.BlockSpec(memory_space=pltpu.SMEM),
            q_spec, k_spec, vt_spec,
            pl.BlockSpec((SUBLANES, LANES), lambda b, p, i: (b, p)),
        ],
        out_specs=q_spec,
        out_shape=jax.ShapeDtypeStruct(q.shape, jnp.bfloat16),
        compiler_params=pltpu.CompilerParams(
            dimension_semantics=("arbitrary", "arbitrary", "arbitrary"), vmem_limit_bytes=VMEM_LIMIT),
        name="moba",
    )(slopes, q, k, vt, kmean)


def _diff_attention(q, k, vt, lam_p, gsub, slopes, lam_init, batch, seq):
    nq = seq // MOBA_BLOCK
    q_spec, k_spec, vt_spec = _attention_specs(nq)
    return pl.pallas_call(
        functools.partial(_diff_kernel, lam_init=lam_init),
        grid=(batch, DIFF_HEADS, nq // ATT_QBLOCKS),
        in_specs=[
            pl.BlockSpec(memory_space=pltpu.SMEM),
            pl.BlockSpec((4, HEAD_DIM), lambda b, p, i: (0, 0)),
            q_spec, k_spec, vt_spec,
            pl.BlockSpec((LANES, MOBA_BLOCK), lambda b, p, i: (0, 0)),
        ],
        out_specs=q_spec,
        out_shape=jax.ShapeDtypeStruct(q.shape, jnp.bfloat16),
        compiler_params=pltpu.CompilerParams(
            dimension_semantics=("arbitrary", "arbitrary", "arbitrary"), vmem_limit_bytes=VMEM_LIMIT),
        name="diff",
    )(slopes, lam_p, q, k, vt, gsub)


def _knockout_topk(sc, k, payload=None):
    n = sc.shape[0]
    row = lax.broadcasted_iota(jnp.int32, sc.shape, 0).astype(jnp.float32)
    vals, picks = [], []
    for _ in range(k):
        m = jnp.max(sc, axis=0, keepdims=True)
        idx = jnp.min(jnp.where(sc == m, row, float(n)), axis=0, keepdims=True)
        hit = row == idx
        vals.append(m)
        picks.append(idx if payload is None else jnp.sum(jnp.where(hit, payload, 0.0), axis=0, keepdims=True))
        sc = jnp.where(hit, NEG_INF, sc)
    return vals, picks


def _mix_kernel(x_ref, am_ref, ad_ref, wo_ref, g2_ref, wq_ref, keys_ref,
                x1_ref, xn_ref, exp_ref, gate_ref):
    x1 = x_ref[...] + _dot(am_ref[...], wo_ref[0:512, :]) + _dot(ad_ref[...], wo_ref[512:1024, :])
    x1_ref[...] = x1
    ms = jnp.mean(x1 * x1, axis=-1, keepdims=True)
    xn = x1 * lax.rsqrt(ms + RMS_EPS) * g2_ref[...]
    xn_ref[...] = xn
    qq = _dot(xn.astype(jnp.bfloat16), wq_ref[...]).astype(jnp.bfloat16)
    tm = qq.shape[0]
    exp_rows, gate_rows = [], []
    for h in range(PEER_HEADS):
        tops = []
        for half in range(2):
            s = h * 2 + half
            sc = _dot_nt(keys_ref[s], qq[:, s * LANES:(s + 1) * LANES])
            tops.append(_knockout_topk(sc, PEER_TOPK))
        (s1, i1), (s2, i2) = tops
        s2a = jnp.concatenate(s2, axis=0)
        i2a = jnp.concatenate(i2, axis=0)
        cand, cidx = [], []
        for a in range(PEER_TOPK):
            nb = PEER_TOPK // (a + 1)
            cand.append(s1[a] + s2a[0:nb])
            cidx.append(i1[a] * float(PEER_N_KEYS) + i2a[0:nb])
        n_cand = sum(c.shape[0] for c in cand)
        pad = -n_cand % SUBLANES
        cand.append(jnp.full((pad, tm), NEG_INF, jnp.float32))
        cidx.append(jnp.zeros((pad, tm), jnp.float32))
        best, experts = _knockout_topk(jnp.concatenate(cand, axis=0), PEER_TOPK,
                                       payload=jnp.concatenate(cidx, axis=0))
        b = jnp.concatenate(best, axis=0)
        e = jnp.exp(b - b[0:1])
        gate_rows.append(e / jnp.sum(e, axis=0, keepdims=True))
        exp_rows.append(jnp.concatenate(experts, axis=0))
    exp_ref[...] = jnp.concatenate(exp_rows, axis=0).T.astype(jnp.int32)
    gate_ref[...] = jnp.concatenate(gate_rows, axis=0).T


def _mix(x2, am, ad, wo, g2, wq, keys):
    T = x2.shape[0]
    row = lambda i: (i, 0)
    const = lambda i: (0, 0)
    return pl.pallas_call(
        _mix_kernel,
        grid=(T // MIX_TM,),
        in_specs=[
            pl.BlockSpec((MIX_TM, 1024), row),
            pl.BlockSpec((MIX_TM, 512), row),
            pl.BlockSpec((MIX_TM, 512), row),
            pl.BlockSpec((1024, 1024), const),
            pl.BlockSpec((1, 1024), const),
            pl.BlockSpec((1024, 2048), const),
            pl.BlockSpec((2 * PEER_HEADS, PEER_N_KEYS, LANES), lambda i: (0, 0, 0)),
        ],
        out_specs=[
            pl.BlockSpec((MIX_TM, 1024), row),
            pl.BlockSpec((MIX_TM, 1024), row),
            pl.BlockSpec((MIX_TM, LANES), row),
            pl.BlockSpec((MIX_TM, LANES), row),
        ],
        out_shape=[
            jax.ShapeDtypeStruct((T, 1024), jnp.float32),
            jax.ShapeDtypeStruct((T, 1024), jnp.float32),
            jax.ShapeDtypeStruct((T, LANES), jnp.int32),
            jax.ShapeDtypeStruct((T, LANES), jnp.float32),
        ],
        compiler_params=pltpu.CompilerParams(dimension_semantics=("arbitrary",), vmem_limit_bytes=VMEM_LIMIT),
        name="mix",
    )(x2, am, ad, wo, g2, wq, keys)


def _pack_kernel(t_ref, o_ref):
    t = t_ref[...]
    lo = pltpu.bitcast(t[:, 0:512].astype(jnp.bfloat16).astype(jnp.float32), jnp.int32)
    hi = pltpu.bitcast(t[:, 512:1024].astype(jnp.bfloat16).astype(jnp.float32), jnp.int32)
    o_ref[...] = lax.shift_right_logical(lo, 16) | (hi & jnp.int32(-65536))


def _pack_table(tab):
    n = tab.shape[0]
    packed = pl.pallas_call(
        _pack_kernel,
        grid=(n // PACK_ROWS,),
        in_specs=[pl.BlockSpec((PACK_ROWS, 1024), lambda i: (i, 0))],
        out_specs=pl.BlockSpec((PACK_ROWS, 512), lambda i: (i, 0)),
        out_shape=jax.ShapeDtypeStruct((n, 512), jnp.int32),
        compiler_params=pltpu.CompilerParams(dimension_semantics=("arbitrary",)),
        name="pack",
    )(tab)
    return packed.reshape(n, 4, LANES)


def _decode(w):
    lo = pltpu.bitcast(w << 16, jnp.float32)
    hi = pltpu.bitcast(w & jnp.int32(-65536), jnp.float32)
    return lo, hi


def _peer_u_kernel(exp_ref, tab_ref, x_ref, g_ref, act_ref, q0_ref, q1_ref, qs_ref):
    ntok, nsel = exp_ref.shape
    ones = jnp.ones((SUBLANES, LANES), jnp.bfloat16)

    def products(t, q_ref):
        x = x_ref[t]
        xlo = x[0:4]
        xhi = x[4:8]
        for e in range(nsel):
            lo, hi = _decode(tab_ref[exp_ref[t, e]])
            q_ref[pl.ds(4 * e, 4), :] = lo * xlo + hi * xhi

    def fold(t, q_ref):
        q = q_ref[pl.ds(0, nsel, stride=4), :]
        for s in range(1, 4):
            q = q + q_ref[pl.ds(s, nsel, stride=4), :]
        qs_ref[pl.ds(pl.multiple_of(t * nsel, nsel), nsel), :] = q

    def gather_group(g):
        for k in range(PEER_GROUP):
            q_ref = (q0_ref, q1_ref)[k % 2]
            products(g * PEER_GROUP + k, q_ref)
            fold(g * PEER_GROUP + k, q_ref)

    def finish_group(g):
        rows8 = PEER_GROUP * nsel
        q_hi, q_lo = _split_bf16(qs_ref[pl.ds(pl.multiple_of(g * rows8, rows8), rows8), :])
        row = (_dot_nt(ones, q_hi) + _dot_nt(ones, q_lo))[0:1]
        h = jnp.concatenate([row[:, k * nsel:(k + 1) * nsel] for k in range(PEER_GROUP)], axis=0)
        gelu = 0.5 * h * (1.0 + lax.erf(h * (1.0 / math.sqrt(2.0))))
        rows = pl.ds(pl.multiple_of(g * PEER_GROUP, PEER_GROUP), PEER_GROUP)
        act_ref[rows, :] = gelu * g_ref[rows, :]

    ngroups = ntok // PEER_GROUP
    gather_group(0)

    def body(g, carry):
        finish_group(g - 1)
        gather_group(g)
        return carry

    lax.fori_loop(1, ngroups, body, 0)
    finish_group(ngroups - 1)


def _peer_v_kernel(exp_ref, act_ref, tab_ref, x_ref, o_ref):
    nsel = exp_ref.shape[1]

    def tok(t, carry):
        acc_lo = jnp.zeros((4, LANES), jnp.float32)
        acc_hi = jnp.zeros((4, LANES), jnp.float32)
        for e in range(nsel):
            a = act_ref[t, e]
            lo, hi = _decode(tab_ref[exp_ref[t, e]])
            acc_lo = acc_lo + a * lo
            acc_hi = acc_hi + a * hi
        o_ref[t] = x_ref[t] + jnp.concatenate([acc_lo, acc_hi], axis=0)
        return carry

    lax.fori_loop(0, x_ref.shape[0], tok, 0)


def _sc_peer_v(vtab, idx, actrep, x1, tok0, ntok):
    d = x1.shape[1]
    nsel = idx.shape[1]
    info = plsc.get_sparse_core_info()
    nw = info.num_cores * info.num_subcores
    per_w = ntok // nw
    nchunk = nsel // SC_ROWS
    nitem = SC_TB * nchunk
    assert info.num_lanes == SC_LANES and ntok % (nw * SC_TB) == 0 and nitem % 2 == 0
    mesh = plsc.VectorSubcoreMesh(core_axis_name="c", subcore_axis_name="s")

    @functools.partial(
        pl.kernel, mesh=mesh,
        out_type=jax.ShapeDtypeStruct((ntok, d), jnp.float32),
        scratch_types=[
            pltpu.VMEM((SC_TB, nsel), jnp.int32),
            pltpu.VMEM((SC_TB, nsel * SC_LANES), jnp.float32),
            pltpu.VMEM((SC_ROWS, d), jnp.float32),
            pltpu.VMEM((SC_ROWS, d), jnp.float32),
            pltpu.VMEM((SC_TB, d), jnp.float32),
            pltpu.SemaphoreType.DMA,
            pltpu.SemaphoreType.DMA,
        ],
        name="sc_peer_v",
    )
    def k(vtab_hbm, idx_hbm, act_hbm, x1_hbm, out_hbm, idx_v, act_v, rows0, rows1, acc_v, sem0, sem1):
        wid = lax.axis_index("s") * info.num_cores + lax.axis_index("c")

        def gather(item, rows, sem):
            tl = item // nchunk
            ch = item % nchunk
            return pltpu.make_async_copy(vtab_hbm.at[idx_v.at[tl, pl.ds(ch * SC_ROWS, SC_ROWS)]], rows, sem)

        def accumulate(item, rows):
            tl = item // nchunk
            r0 = (item % nchunk) * SC_ROWS
            a = [act_v[tl, pl.ds((r0 + e) * SC_LANES, SC_LANES)] for e in range(SC_ROWS)]

            @plsc.parallel_loop(0, d // SC_LANES, 1, unroll=2)
            def _(c):
                col = pl.ds(c * SC_LANES, SC_LANES)
                terms = [a[e] * rows[e, col] for e in range(SC_ROWS)]
                while len(terms) > 1:
                    terms = [terms[j] + terms[j + 1] for j in range(0, len(terms), 2)]
                acc_v[tl, col] = acc_v[tl, col] + terms[0]

        @pl.loop(0, per_w // SC_TB)
        def _(bi):
            loc = pl.multiple_of(wid * per_w + bi * SC_TB, SC_TB)
            src = pl.ds(tok0 + loc, SC_TB)
            pltpu.sync_copy(idx_hbm.at[src], idx_v)
            pltpu.sync_copy(act_hbm.at[pl.ds(loc, SC_TB)], act_v)
            pltpu.sync_copy(x1_hbm.at[src], acc_v)
            gather(0, rows0, sem0).start()

            @pl.loop(0, nitem // 2)
            def _(p):
                gather(2 * p + 1, rows1, sem1).start()
                gather(2 * p, rows0, sem0).wait()
                accumulate(2 * p, rows0)

                @pl.when(p < nitem // 2 - 1)
                def _():
                    gather(2 * p + 2, rows0, sem0).start()

                gather(2 * p + 1, rows1, sem1).wait()
                accumulate(2 * p + 1, rows1)

            pltpu.sync_copy(acc_v, out_hbm.at[pl.ds(loc, SC_TB)])

    return k(vtab, idx, actrep, x1)


def _peer_experts(xn3, x1, experts, gates, u_packed, v_packed, v_table):
    T, nsel = experts.shape
    n_exp = u_packed.shape[0]
    d = x1.shape[1]
    tc = T // PEER_CHUNKS
    steps = tc // PEER_TT
    assert nsel == LANES and PEER_TT % PEER_GROUP == 0 and T % (PEER_CHUNKS * PEER_TT) == 0
    x13 = x1.reshape(T, SUBLANES, LANES)
    tab_spec = pl.BlockSpec((n_exp, 4, LANES), lambda i: (0, 0, 0), pipeline_mode=pl.Buffered(1))
    cp = pltpu.CompilerParams(dimension_semantics=("arbitrary",), vmem_limit_bytes=VMEM_LIMIT)
    outs = []
    for c in range(PEER_CHUNKS):
        off = c * steps
        row = lambda i, off=off: (i + off, 0)
        row3 = lambda i, off=off: (i + off, 0, 0)
        smem_blk = lambda: pl.BlockSpec((PEER_TT, nsel), row, memory_space=pltpu.SMEM)
        x_spec = pl.BlockSpec((PEER_TT, SUBLANES, LANES), row3)
        act = pl.pallas_call(
            _peer_u_kernel,
            grid=(steps,),
            in_specs=[smem_blk(), tab_spec, x_spec, pl.BlockSpec((PEER_TT, nsel), row)],
            out_specs=pl.BlockSpec((PEER_TT, nsel), lambda i: (i, 0)),
            out_shape=jax.ShapeDtypeStruct((tc, nsel), jnp.float32),
            scratch_shapes=[pltpu.VMEM((4 * nsel, LANES), jnp.float32), pltpu.VMEM((4 * nsel, LANES), jnp.float32),
                            pltpu.VMEM((PEER_TT * nsel, LANES), jnp.float32)],
            compiler_params=cp,
            name="peer_u",
        )(experts, u_packed, xn3, gates)
        if c < PEER_SC_CHUNKS:
            actrep = jnp.broadcast_to(act[:, :, None], (tc, nsel, SC_LANES)).reshape(tc, nsel * SC_LANES)
            outs.append(_sc_peer_v(v_table, experts, actrep, x1, c * tc, tc))
        else:
            out3 = pl.pallas_call(
                _peer_v_kernel,
                grid=(steps,),
                in_specs=[smem_blk(), pl.BlockSpec((PEER_TT, nsel), lambda i: (i, 0), memory_space=pltpu.SMEM),
                          tab_spec, x_spec],
                out_specs=pl.BlockSpec((PEER_TT, SUBLANES, LANES), lambda i: (i, 0, 0)),
                out_shape=jax.ShapeDtypeStruct((tc, SUBLANES, LANES), jnp.float32),
                compiler_params=cp,
                name="peer_v",
            )(experts, act, v_packed, x13)
            outs.append(out3.reshape(tc, d))
    return jnp.concatenate(outs, axis=0)


def _alibi_slopes(n):
    return 2.0 ** (-8.0 * np.arange(1, n + 1) / n)


def _lambda_init(layer):
    return 0.8 - 0.6 * math.exp(-0.3 * layer)


def kernel(x, norm1_g, w_in, moba_qk_norm_g, diff_qk_norm_g, diff_lambda, diff_subln_g, w_out, norm2_g,
           peer_w_query, peer_sub_keys, peer_u, peer_v):
    b, s, d = x.shape
    T = b * s
    depth = w_in.shape[0]
    mw = MOBA_HEADS * HEAD_DIM
    assert d == 1024 and s % PROJ_TM == 0 and mw == 512 and DIFF_HEADS * 2 * HEAD_DIM == 512
    f32, bf16 = jnp.float32, jnp.bfloat16
    scale = HEAD_DIM ** -0.5
    grp = lax.broadcasted_iota(jnp.int32, (LANES, LANES), 0) // HEAD_DIM
    gsum = (grp == grp.T).astype(bf16)
    moba_slopes = jnp.asarray(_alibi_slopes(MOBA_HEADS).reshape(MOBA_HEADS // 2, 2), f32)
    diff_slopes = jnp.asarray(np.repeat(_alibi_slopes(DIFF_HEADS)[:, None], 2, axis=1), f32)
    x2 = x.reshape(T, d)
    for l in range(depth):
        w = w_in[l]
        wqk = jnp.concatenate([w[:, 0:2 * mw], w[:, 3 * mw:3 * mw + 1024]], axis=1).astype(bf16)
        wvt = jnp.concatenate([w[:, 2 * mw:3 * mw], w[:, 3 * mw + 1024:]], axis=1).T.astype(bf16)
        mg, dg = moba_qk_norm_g[l].astype(f32), diff_qk_norm_g[l].astype(f32)
        gqk = jnp.concatenate([jnp.tile(mg[0] * scale, 8), jnp.tile(mg[1], 8),
                               jnp.tile(dg[0] * scale, 8), jnp.tile(dg[1], 8)])[None, :]
        mq, mk, dq, dk, mvt, dvt, kmean = _project(x2, norm1_g[l][None, :].astype(f32), wqk, wvt, gqk, gsum)
        nblk = PROJ_TM // MOBA_BLOCK
        kmean = kmean.reshape(T // PROJ_TM, SUBLANES, 512)[:, :nblk].reshape(T // MOBA_BLOCK, 512)
        moba_o = _moba_attention(mq, mk, mvt, kmean, moba_slopes, b, s)
        gsub = jnp.broadcast_to(diff_subln_g[l].astype(f32)[:, None], (LANES, MOBA_BLOCK))
        diff_o = _diff_attention(dq, dk, dvt, diff_lambda[l].astype(f32), gsub, diff_slopes,
                                 _lambda_init(l), b, s)
        keys = peer_sub_keys[l].reshape(2 * PEER_HEADS, PEER_N_KEYS, LANES).astype(bf16)
        x1, xn, experts, gates = _mix(x2, moba_o, diff_o, w_out[l].astype(bf16), norm2_g[l][None, :].astype(f32),
                                      peer_w_query[l].astype(bf16), keys)
        v_packed = _pack_table(peer_v[l]) if PEER_SC_CHUNKS < PEER_CHUNKS else None
        x2 = _peer_experts(xn.reshape(T, SUBLANES, LANES), x1, experts, gates, _pack_table(peer_u[l]), v_packed,
                           peer_v[l].astype(f32))
    return x2.reshape(b, s, d)
```

```python
import functools
import math

import jax
import jax.numpy as jnp
import numpy as np
from jax import lax
from jax.experimental import pallas as pl
from jax.experimental.pallas import tpu as pltpu
from jax.experimental.pallas import tpu_sc as plsc

HEAD_DIM = 64
MOBA_HEADS = 8
MOBA_BLOCK = 256
MOBA_TOPK = 3
DIFF_HEADS = 4
PEER_HEADS = 8
PEER_N_KEYS = 128
PEER_TOPK = 16
RMS_EPS = 1e-6

LANES = 128
SUBLANES = 8
VMEM_LIMIT = 48 * 1024 * 1024

PROJ_TM = 512
MIX_TM = 256
PEER_TT = 64
PACK_ROWS = 512
ATT_QBLOCKS = 2
PEER_GROUP = 8
PEER_CHUNKS = 16
PEER_SC_CHUNKS = 13
SC_LANES = 16
SC_ROWS = 32
SC_TB = 8

NEG_INF = float("-inf")
_NT = (((1,), (1,)), ((), ()))


def _dot(a, b):
    return jnp.dot(a, b, preferred_element_type=jnp.float32)


def _dot_nt(a, b):
    return lax.dot_general(a, b, _NT, preferred_element_type=jnp.float32)


def _split_bf16(x):
    hi = x.astype(jnp.bfloat16)
    lo = (x - hi.astype(jnp.float32)).astype(jnp.bfloat16)
    return hi, lo


def _proj_kernel(x_ref, g1_ref, wqk_ref, wvt_ref, gqk_ref, gsum_ref,
                 mq_ref, mk_ref, dq_ref, dk_ref, mvt_ref, dvt_ref, kmean_ref):
    x = x_ref[...]
    ms = jnp.mean(x * x, axis=-1, keepdims=True)
    h = (x * lax.rsqrt(ms + RMS_EPS) * g1_ref[...]).astype(jnp.bfloat16)
    gsum = gsum_ref[...]
    outs = (mq_ref, mk_ref, dq_ref, dk_ref)
    for sec in range(4):
        for half in range(2):
            c0 = sec * 512 + half * 256
            p = _dot(h, wqk_ref[:, c0:c0 + 256])
            for s in range(2):
                ps = p[:, s * LANES:(s + 1) * LANES]
                sq_hi, sq_lo = _split_bf16(ps * ps)
                ssq = _dot(sq_hi, gsum) + _dot(sq_lo, gsum)
                g = gqk_ref[:, c0 + s * LANES:c0 + (s + 1) * LANES]
                y = ps * lax.rsqrt(ssq * (1.0 / HEAD_DIM) + RMS_EPS) * g
                l0 = half * 256 + s * LANES
                outs[sec][:, l0:l0 + LANES] = y.astype(jnp.bfloat16)
                if sec == 1:
                    for blk in range(PROJ_TM // MOBA_BLOCK):
                        kmean_ref[blk:blk + 1, l0:l0 + LANES] = jnp.mean(
                            y[blk * MOBA_BLOCK:(blk + 1) * MOBA_BLOCK], axis=0, keepdims=True)
    kmean_ref[PROJ_TM // MOBA_BLOCK:, :] = jnp.zeros((SUBLANES - PROJ_TM // MOBA_BLOCK, 512), jnp.float32)
    vt = _dot_nt(wvt_ref[...], h).astype(jnp.bfloat16)
    for blk in range(PROJ_TM // MOBA_BLOCK):
        mvt_ref[blk] = vt[0:512, blk * MOBA_BLOCK:(blk + 1) * MOBA_BLOCK]
        dvt_ref[blk] = vt[512:1024, blk * MOBA_BLOCK:(blk + 1) * MOBA_BLOCK]


def _project(x2, g1, wqk, wvt, gqk, gsum):
    T = x2.shape[0]
    nt = T // PROJ_TM
    nblk = PROJ_TM // MOBA_BLOCK
    row = lambda i: (i, 0)
    const = lambda i: (0, 0)
    qk_shape = jax.ShapeDtypeStruct((T, 512), jnp.bfloat16)
    vt_shape = jax.ShapeDtypeStruct((T // MOBA_BLOCK, 512, MOBA_BLOCK), jnp.bfloat16)
    return pl.pallas_call(
        _proj_kernel,
        grid=(nt,),
        in_specs=[
            pl.BlockSpec((PROJ_TM, 1024), row),
            pl.BlockSpec((1, 1024), const),
            pl.BlockSpec((1024, 2048), const),
            pl.BlockSpec((1024, 1024), const),
            pl.BlockSpec((1, 2048), const),
            pl.BlockSpec((LANES, LANES), const),
        ],
        out_specs=[
            pl.BlockSpec((PROJ_TM, 512), row),
            pl.BlockSpec((PROJ_TM, 512), row),
            pl.BlockSpec((PROJ_TM, 512), row),
            pl.BlockSpec((PROJ_TM, 512), row),
            pl.BlockSpec((nblk, 512, MOBA_BLOCK), lambda i: (i, 0, 0)),
            pl.BlockSpec((nblk, 512, MOBA_BLOCK), lambda i: (i, 0, 0)),
            pl.BlockSpec((SUBLANES, 512), row),
        ],
        out_shape=[qk_shape, qk_shape, qk_shape, qk_shape, vt_shape, vt_shape,
                   jax.ShapeDtypeStruct((nt * SUBLANES, 512), jnp.float32)],
        compiler_params=pltpu.CompilerParams(dimension_semantics=("arbitrary",), vmem_limit_bytes=VMEM_LIMIT),
        name="proj",
    )(x2, g1, wqk, wvt, gqk, gsum)


def _rel_qk():
    qc = lax.broadcasted_iota(jnp.int32, (MOBA_BLOCK, MOBA_BLOCK), 1)
    kr = lax.broadcasted_iota(jnp.int32, (MOBA_BLOCK, MOBA_BLOCK), 0)
    return qc - kr


def _attend(q_h, k_ref, vt_load, i, slope, rel, selbias):
    nb = rel.astype(jnp.float32) * (-slope)
    ts, rowbs, colmax = [], [], []
    for j in range(i + 1):
        t = _dot_nt(k_ref[j * MOBA_BLOCK:(j + 1) * MOBA_BLOCK, :], q_h) + nb
        if j == i:
            t = jnp.where(rel >= 0, t, NEG_INF)
            rowb = jnp.zeros((1, MOBA_BLOCK), jnp.float32)
        else:
            rowb = jnp.full((1, MOBA_BLOCK), (i - j) * MOBA_BLOCK, jnp.float32) * (-slope)
            if selbias is not None:
                rowb = rowb + selbias[j:j + 1, :]
        ts.append(t)
        rowbs.append(rowb)
        colmax.append(jnp.max(t, axis=0, keepdims=True) + rowb)
    m = functools.reduce(jnp.maximum, colmax)
    l, acc = None, None
    for j in range(i + 1):
        p = jnp.exp(ts[j] - (m - rowbs[j]))
        lj = jnp.sum(p, axis=0, keepdims=True)
        aj = _dot(vt_load(j), p.astype(jnp.bfloat16))
        l = lj if l is None else l + lj
        acc = aj if acc is None else acc + aj
    return acc / l


def _head_queries(q2):
    lane = lax.broadcasted_iota(jnp.int32, q2.shape, 1)
    return [jnp.where((lane >= hh * HEAD_DIM) & (lane < (hh + 1) * HEAD_DIM), q2, jnp.zeros_like(q2))
            for hh in range(2)]


def _block_selection(gate, i):
    blk = lax.broadcasted_iota(jnp.int32, gate.shape, 0)
    cnt = jnp.zeros(gate.shape, jnp.int32)
    for mblk in range(i):
        gm = gate[mblk:mblk + 1, :]
        beats = (gm > gate) | ((gm == gate) & (mblk < blk))
        cnt = cnt + jnp.where(beats, 1, 0)
    return jnp.where((blk < i) & (cnt < MOBA_TOPK), 0.0, NEG_INF)


def _moba_kernel(slopes_ref, q_ref, k_ref, vt_ref, kmean_ref, o_ref):
    pair = pl.program_id(1)
    step = pl.program_id(2)
    nq = vt_ref.shape[0]
    rel = _rel_qk()
    km_hi, km_lo = _split_bf16(kmean_ref[...])
    for st in range(nq // ATT_QBLOCKS):
        @pl.when(step == st)
        def _(st=st):
            for u in range(ATT_QBLOCKS):
                i = st * ATT_QBLOCKS + u
                rows = slice(u * MOBA_BLOCK, (u + 1) * MOBA_BLOCK)
                outs = []
                for hh, q_h in enumerate(_head_queries(q_ref[rows, :])):
                    gate = _dot_nt(km_hi, q_h) + _dot_nt(km_lo, q_h)
                    vt_load = lambda j, hh=hh: vt_ref[j, hh * HEAD_DIM:(hh + 1) * HEAD_DIM, :]
                    outs.append(_attend(q_h, k_ref, vt_load, i, slopes_ref[pair, hh], rel,
                                        _block_selection(gate, i)))
                o_t = jnp.concatenate(outs, axis=0)
                o_ref[rows, :] = o_t.T.astype(o_ref.dtype)


def _diff_kernel(slopes_ref, lam_ref, q_ref, k_ref, vt_ref, gsub_ref, o_ref, *, lam_init):
    head = pl.program_id(1)
    step = pl.program_id(2)
    nq = vt_ref.shape[0]
    rel = _rel_qk()
    slope = slopes_ref[head, 0]
    lp = lam_ref[...]
    lam = (jnp.exp(jnp.sum(lp[0:1] * lp[1:2], axis=-1, keepdims=True))
           - jnp.exp(jnp.sum(lp[2:3] * lp[3:4], axis=-1, keepdims=True)) + lam_init)
    vt_load = lambda j: vt_ref[j]
    for st in range(nq // ATT_QBLOCKS):
        @pl.when(step == st)
        def _(st=st):
            for u in range(ATT_QBLOCKS):
                i = st * ATT_QBLOCKS + u
                rows = slice(u * MOBA_BLOCK, (u + 1) * MOBA_BLOCK)
                parts = [_attend(q_h, k_ref, vt_load, i, slope, rel, None)
                         for q_h in _head_queries(q_ref[rows, :])]
                o_t = parts[0] - lam * parts[1]
                ms = jnp.mean(o_t * o_t, axis=0, keepdims=True)
                o_t = o_t * lax.rsqrt(ms + RMS_EPS) * gsub_ref[...] * (1.0 - lam_init)
                o_ref[rows, :] = o_t.T.astype(o_ref.dtype)


def _attention_specs(nq):
    nst = nq // ATT_QBLOCKS
    q_spec = pl.BlockSpec((ATT_QBLOCKS * MOBA_BLOCK, LANES), lambda b, p, i: (b * nst + i, p))
    k_spec = pl.BlockSpec((nq * MOBA_BLOCK, LANES), lambda b, p, i: (b, p))
    vt_spec = pl.BlockSpec((nq, LANES, MOBA_BLOCK), lambda b, p, i: (b, p, 0))
    return q_spec, k_spec, vt_spec


def _moba_attention(q, k, vt, kmean, slopes, batch, seq):
    nq = seq // MOBA_BLOCK
    q_spec, k_spec, vt_spec = _attention_specs(nq)
    return pl.pallas_call(
        _moba_kernel,
        grid=(batch, MOBA_HEADS // 2, nq // ATT_QBLOCKS),
        in_specs=[
            pl.BlockSpec(memory_space=pltpu.SMEM),
            q_spec, k_spec, vt_spec,
            pl.BlockSpec((SUBLANES, LANES), lambda b, p, i: (b, p)),
        ],
        out_specs=q_spec,
        out_shape=jax.ShapeDtypeStruct(q.shape, jnp.bfloat16),
        compiler_params=pltpu.CompilerParams(
            dimension_semantics=("arbitrary", "arbitrary", "arbitrary"), vmem_limit_bytes=VMEM_LIMIT),
        name="moba",
    )(slopes, q, k, vt, kmean)


def _diff_attention(q, k, vt, lam_p, gsub, slopes, lam_init, batch, seq):
    nq = seq // MOBA_BLOCK
    q_spec, k_spec, vt_spec = _attention_specs(nq)
    return pl.pallas_call(
        functools.partial(_diff_kernel, lam_init=lam_init),
        grid=(batch, DIFF_HEADS, nq // ATT_QBLOCKS),
        in_specs=[
            pl.BlockSpec(memory_space=pltpu.SMEM),
            pl.BlockSpec((4, HEAD_DIM), lambda b, p, i: (0, 0)),
            q_spec, k_spec, vt_spec,
            pl.BlockSpec((LANES, MOBA_BLOCK), lambda b, p, i: (0, 0)),
        ],
        out_specs=q_spec,
        out_shape=jax.ShapeDtypeStruct(q.shape, jnp.bfloat16),
        compiler_params=pltpu.CompilerParams(
            dimension_semantics=("arbitrary", "arbitrary", "arbitrary"), vmem_limit_bytes=VMEM_LIMIT),
        name="diff",
    )(slopes, lam_p, q, k, vt, gsub)


def _knockout_topk(sc, k, payload=None):
    n = sc.shape[0]
    row = lax.broadcasted_iota(jnp.int32, sc.shape, 0).astype(jnp.float32)
    vals, picks = [], []
    for _ in range(k):
        m = jnp.max(sc, axis=0, keepdims=True)
        idx = jnp.min(jnp.where(sc == m, row, float(n)), axis=0, keepdims=True)
        hit = row == idx
        vals.append(m)
        picks.append(idx if payload is None else jnp.sum(jnp.where(hit, payload, 0.0), axis=0, keepdims=True))
        sc = jnp.where(hit, NEG_INF, sc)
    return vals, picks


def _mix_kernel(x_ref, am_ref, ad_ref, wo_ref, g2_ref, wq_ref, keys_ref,
                x1_ref, xn_ref, exp_ref, gate_ref):
    x1 = x_ref[...] + _dot(am_ref[...], wo_ref[0:512, :]) + _dot(ad_ref[...], wo_ref[512:1024, :])
    x1_ref[...] = x1
    ms = jnp.mean(x1 * x1, axis=-1, keepdims=True)
    xn = x1 * lax.rsqrt(ms + RMS_EPS) * g2_ref[...]
    xn_ref[...] = xn
    qq = _dot(xn.astype(jnp.bfloat16), wq_ref[...]).astype(jnp.bfloat16)
    tm = qq.shape[0]
    exp_rows, gate_rows = [], []
    for h in range(PEER_HEADS):
        tops = []
        for half in range(2):
            s = h * 2 + half
            sc = _dot_nt(keys_ref[s], qq[:, s * LANES:(s + 1) * LANES])
            tops.append(_knockout_topk(sc, PEER_TOPK))
        (s1, i1), (s2, i2) = tops
        s2a = jnp.concatenate(s2, axis=0)
        i2a = jnp.concatenate(i2, axis=0)
        cand, cidx = [], []
        for a in range(PEER_TOPK):
            nb = PEER_TOPK // (a + 1)
            cand.append(s1[a] + s2a[0:nb])
            cidx.append(i1[a] * float(PEER_N_KEYS) + i2a[0:nb])
        n_cand = sum(c.shape[0] for c in cand)
        pad = -n_cand % SUBLANES
        cand.append(jnp.full((pad, tm), NEG_INF, jnp.float32))
        cidx.append(jnp.zeros((pad, tm), jnp.float32))
        best, experts = _knockout_topk(jnp.concatenate(cand, axis=0), PEER_TOPK,
                                       payload=jnp.concatenate(cidx, axis=0))
        b = jnp.concatenate(best, axis=0)
        e = jnp.exp(b - b[0:1])
        gate_rows.append(e / jnp.sum(e, axis=0, keepdims=True))
        exp_rows.append(jnp.concatenate(experts, axis=0))
    exp_ref[...] = jnp.concatenate(exp_rows, axis=0).T.astype(jnp.int32)
    gate_ref[...] = jnp.concatenate(gate_rows, axis=0).T


def _mix(x2, am, ad, wo, g2, wq, keys):
    T = x2.shape[0]
    row = lambda i: (i, 0)
    const = lambda i: (0, 0)
    return pl.pallas_call(
        _mix_kernel,
        grid=(T // MIX_TM,),
        in_specs=[
            pl.BlockSpec((MIX_TM, 1024), row),
            pl.BlockSpec((MIX_TM, 512), row),
            pl.BlockSpec((MIX_TM, 512), row),
            pl.BlockSpec((1024, 1024), const),
            pl.BlockSpec((1, 1024), const),
            pl.BlockSpec((1024, 2048), const),
            pl.BlockSpec((2 * PEER_HEADS, PEER_N_KEYS, LANES), lambda i: (0, 0, 0)),
        ],
        out_specs=[
            pl.BlockSpec((MIX_TM, 1024), row),
            pl.BlockSpec((MIX_TM, 1024), row),
            pl.BlockSpec((MIX_TM, LANES), row),
            pl.BlockSpec((MIX_TM, LANES), row),
        ],
        out_shape=[
            jax.ShapeDtypeStruct((T, 1024), jnp.float32),
            jax.ShapeDtypeStruct((T, 1024), jnp.float32),
            jax.ShapeDtypeStruct((T, LANES), jnp.int32),
            jax.ShapeDtypeStruct((T, LANES), jnp.float32),
        ],
        compiler_params=pltpu.CompilerParams(dimension_semantics=("arbitrary",), vmem_limit_bytes=VMEM_LIMIT),
        name="mix",
    )(x2, am, ad, wo, g2, wq, keys)


def _pack_kernel(t_ref, o_ref):
    t = t_ref[...]
    lo = pltpu.bitcast(t[:, 0:512].astype(jnp.bfloat16).astype(jnp.float32), jnp.int32)
    hi = pltpu.bitcast(t[:, 512:1024].astype(jnp.bfloat16).astype(jnp.float32), jnp.int32)
    o_ref[...] = lax.shift_right_logical(lo, 16) | (hi & jnp.int32(-65536))


def _pack_table(tab):
    n = tab.shape[0]
    packed = pl.pallas_call(
        _pack_kernel,
        grid=(n // PACK_ROWS,),
        in_specs=[pl.BlockSpec((PACK_ROWS, 1024), lambda i: (i, 0))],
        out_specs=pl.BlockSpec((PACK_ROWS, 512), lambda i: (i, 0)),
        out_shape=jax.ShapeDtypeStruct((n, 512), jnp.int32),
        compiler_params=pltpu.CompilerParams(dimension_semantics=("arbitrary",)),
        name="pack",
    )(tab)
    return packed.reshape(n, 4, LANES)


def _decode(w):
    lo = pltpu.bitcast(w << 16, jnp.float32)
    hi = pltpu.bitcast(w & jnp.int32(-65536), jnp.float32)
    return lo, hi


def _peer_u_kernel(exp_ref, tab_ref, x_ref, g_ref, act_ref, q0_ref, q1_ref, qs_ref):
    ntok, nsel = exp_ref.shape
    ones = jnp.ones((SUBLANES, LANES), jnp.bfloat16)

    def products(t, q_ref):
        x = x_ref[t]
        xlo = x[0:4]
        xhi = x[4:8]
        for e in range(nsel):
            lo, hi = _decode(tab_ref[exp_ref[t, e]])
            q_ref[pl.ds(4 * e, 4), :] = lo * xlo + hi * xhi

    def fold(t, q_ref):
        q = q_ref[pl.ds(0, nsel, stride=4), :]
        for s in range(1, 4):
            q = q + q_ref[pl.ds(s, nsel, stride=4), :]
        qs_ref[pl.ds(pl.multiple_of(t * nsel, nsel), nsel), :] = q

    def gather_group(g):
        for k in range(PEER_GROUP):
            q_ref = (q0_ref, q1_ref)[k % 2]
            products(g * PEER_GROUP + k, q_ref)
            fold(g * PEER_GROUP + k, q_ref)

    def finish_group(g):
        rows8 = PEER_GROUP * nsel
        q_hi, q_lo = _split_bf16(qs_ref[pl.ds(pl.multiple_of(g * rows8, rows8), rows8), :])
        row = (_dot_nt(ones, q_hi) + _dot_nt(ones, q_lo))[0:1]
        h = jnp.concatenate([row[:, k * nsel:(k + 1) * nsel] for k in range(PEER_GROUP)], axis=0)
        gelu = 0.5 * h * (1.0 + lax.erf(h * (1.0 / math.sqrt(2.0))))
        rows = pl.ds(pl.multiple_of(g * PEER_GROUP, PEER_GROUP), PEER_GROUP)
        act_ref[rows, :] = gelu * g_ref[rows, :]

    ngroups = ntok // PEER_GROUP
    gather_group(0)

    def body(g, carry):
        finish_group(g - 1)
        gather_group(g)
        return carry

    lax.fori_loop(1, ngroups, body, 0)
    finish_group(ngroups - 1)


def _peer_v_kernel(exp_ref, act_ref, tab_ref, x_ref, o_ref):
    nsel = exp_ref.shape[1]

    def tok(t, carry):
        acc_lo = jnp.zeros((4, LANES), jnp.float32)
        acc_hi = jnp.zeros((4, LANES), jnp.float32)
        for e in range(nsel):
            a = act_ref[t, e]
            lo, hi = _decode(tab_ref[exp_ref[t, e]])
            acc_lo = acc_lo + a * lo
            acc_hi = acc_hi + a * hi
        o_ref[t] = x_ref[t] + jnp.concatenate([acc_lo, acc_hi], axis=0)
        return carry

    lax.fori_loop(0, x_ref.shape[0], tok, 0)


def _sc_peer_v(vtab, idx, actrep, x1, tok0, ntok):
    d = x1.shape[1]
    nsel = idx.shape[1]
    info = plsc.get_sparse_core_info()
    nw = info.num_cores * info.num_subcores
    per_w = ntok // nw
    nchunk = nsel // SC_ROWS
    nitem = SC_TB * nchunk
    assert info.num_lanes == SC_LANES and ntok % (nw * SC_TB) == 0 and nitem % 2 == 0
    mesh = plsc.VectorSubcoreMesh(core_axis_name="c", subcore_axis_name="s")

    @functools.partial(
        pl.kernel, mesh=mesh,
        out_type=jax.ShapeDtypeStruct((ntok, d), jnp.float32),
        scratch_types=[
            pltpu.VMEM((SC_TB, nsel), jnp.int32),
            pltpu.VMEM((SC_TB, nsel * SC_LANES), jnp.float32),
            pltpu.VMEM((SC_ROWS, d), jnp.float32),
            pltpu.VMEM((SC_ROWS, d), jnp.float32),
            pltpu.VMEM((SC_TB, d), jnp.float32),
            pltpu.SemaphoreType.DMA,
            pltpu.SemaphoreType.DMA,
        ],
        name="sc_peer_v",
    )
    def k(vtab_hbm, idx_hbm, act_hbm, x1_hbm, out_hbm, idx_v, act_v, rows0, rows1, acc_v, sem0, sem1):
        wid = lax.axis_index("s") * info.num_cores + lax.axis_index("c")

        def gather(item, rows, sem):
            tl = item // nchunk
            ch = item % nchunk
            return pltpu.make_async_copy(vtab_hbm.at[idx_v.at[tl, pl.ds(ch * SC_ROWS, SC_ROWS)]], rows, sem)

        def accumulate(item, rows):
            tl = item // nchunk
            r0 = (item % nchunk) * SC_ROWS
            a = [act_v[tl, pl.ds((r0 + e) * SC_LANES, SC_LANES)] for e in range(SC_ROWS)]

            @plsc.parallel_loop(0, d // SC_LANES, 1, unroll=2)
            def _(c):
                col = pl.ds(c * SC_LANES, SC_LANES)
                terms = [a[e] * rows[e, col] for e in range(SC_ROWS)]
                while len(terms) > 1:
                    terms = [terms[j] + terms[j + 1] for j in range(0, len(terms), 2)]
                acc_v[tl, col] = acc_v[tl, col] + terms[0]

        @pl.loop(0, per_w // SC_TB)
        def _(bi):
            loc = pl.multiple_of(wid * per_w + bi * SC_TB, SC_TB)
            src = pl.ds(tok0 + loc, SC_TB)
            pltpu.sync_copy(idx_hbm.at[src], idx_v)
            pltpu.sync_copy(act_hbm.at[pl.ds(loc, SC_TB)], act_v)
            pltpu.sync_copy(x1_hbm.at[src], acc_v)
            gather(0, rows0, sem0).start()

            @pl.loop(0, nitem // 2)
            def _(p):
                gather(2 * p + 1, rows1, sem1).start()
                gather(2 * p, rows0, sem0).wait()
                accumulate(2 * p, rows0)

                @pl.when(p < nitem // 2 - 1)
                def _():
                    gather(2 * p + 2, rows0, sem0).start()

                gather(2 * p + 1, rows1, sem1).wait()
                accumulate(2 * p + 1, rows1)

            pltpu.sync_copy(acc_v, out_hbm.at[pl.ds(loc, SC_TB)])

    return k(vtab, idx, actrep, x1)


def _peer_experts(xn3, x1, experts, gates, u_packed, v_packed, v_table):
    T, nsel = experts.shape
    n_exp = u_packed.shape[0]
    d = x1.shape[1]
    tc = T // PEER_CHUNKS
    steps = tc // PEER_TT
    assert nsel == LANES and PEER_TT % PEER_GROUP == 0 and T % (PEER_CHUNKS * PEER_TT) == 0
    x13 = x1.reshape(T, SUBLANES, LANES)
    tab_spec = pl.BlockSpec((n_exp, 4, LANES), lambda i: (0, 0, 0), pipeline_mode=pl.Buffered(1))
    cp = pltpu.CompilerParams(dimension_semantics=("arbitrary",), vmem_limit_bytes=VMEM_LIMIT)
    outs = []
    for c in range(PEER_CHUNKS):
        off = c * steps
        row = lambda i, off=off: (i + off, 0)
        row3 = lambda i, off=off: (i + off, 0, 0)
        smem_blk = lambda: pl.BlockSpec((PEER_TT, nsel), row, memory_space=pltpu.SMEM)
        x_spec = pl.BlockSpec((PEER_TT, SUBLANES, LANES), row3)
        act = pl.pallas_call(
            _peer_u_kernel,
            grid=(steps,),
            in_specs=[smem_blk(), tab_spec, x_spec, pl.BlockSpec((PEER_TT, nsel), row)],
            out_specs=pl.BlockSpec((PEER_TT, nsel), lambda i: (i, 0)),
            out_shape=jax.ShapeDtypeStruct((tc, nsel), jnp.float32),
            scratch_shapes=[pltpu.VMEM((4 * nsel, LANES), jnp.float32), pltpu.VMEM((4 * nsel, LANES), jnp.float32),
                            pltpu.VMEM((PEER_TT * nsel, LANES), jnp.float32)],
            compiler_params=cp,
            name="peer_u",
        )(experts, u_packed, xn3, gates)
        if c < PEER_SC_CHUNKS:
            actrep = jnp.broadcast_to(act[:, :, None], (tc, nsel, SC_LANES)).reshape(tc, nsel * SC_LANES)
            outs.append(_sc_peer_v(v_table, experts, actrep, x1, c * tc, tc))
        else:
            out3 = pl.pallas_call(
                _peer_v_kernel,
                grid=(steps,),
                in_specs=[smem_blk(), pl.BlockSpec((PEER_TT, nsel), lambda i: (i, 0), memory_space=pltpu.SMEM),
                          tab_spec, x_spec],
                out_specs=pl.BlockSpec((PEER_TT, SUBLANES, LANES), lambda i: (i, 0, 0)),
                out_shape=jax.ShapeDtypeStruct((tc, SUBLANES, LANES), jnp.float32),
                compiler_params=cp,
                name="peer_v",
            )(experts, act, v_packed, x13)
            outs.append(out3.reshape(tc, d))
    return jnp.concatenate(outs, axis=0)


def _alibi_slopes(n):
    return 2.0 ** (-8.0 * np.arange(1, n + 1) / n)


def _lambda_init(layer):
    return 0.8 - 0.6 * math.exp(-0.3 * layer)


def kernel(x, norm1_g, w_in, moba_qk_norm_g, diff_qk_norm_g, diff_lambda, diff_subln_g, w_out, norm2_g,
           peer_w_query, peer_sub_keys, peer_u, peer_v):
    b, s, d = x.shape
    T = b * s
    depth = w_in.shape[0]
    mw = MOBA_HEADS * HEAD_DIM
    assert d == 1024 and s % PROJ_TM == 0 and mw == 512 and DIFF_HEADS * 2 * HEAD_DIM == 512
    f32, bf16 = jnp.float32, jnp.bfloat16
    scale = HEAD_DIM ** -0.5
    grp = lax.broadcasted_iota(jnp.int32, (LANES, LANES), 0) // HEAD_DIM
    gsum = (grp == grp.T).astype(bf16)
    moba_slopes = jnp.asarray(_alibi_slopes(MOBA_HEADS).reshape(MOBA_HEADS // 2, 2), f32)
    diff_slopes = jnp.asarray(np.repeat(_alibi_slopes(DIFF_HEADS)[:, None], 2, axis=1), f32)
    x2 = x.reshape(T, d)
    for l in range(depth):
        w = w_in[l]
        wqk = jnp.concatenate([w[:, 0:2 * mw], w[:, 3 * mw:3 * mw + 1024]], axis=1).astype(bf16)
        wvt = jnp.concatenate([w[:, 2 * mw:3 * mw], w[:, 3 * mw + 1024:]], axis=1).T.astype(bf16)
        mg, dg = moba_qk_norm_g[l].astype(f32), diff_qk_norm_g[l].astype(f32)
        gqk = jnp.concatenate([jnp.tile(mg[0] * scale, 8), jnp.tile(mg[1], 8),
                               jnp.tile(dg[0] * scale, 8), jnp.tile(dg[1], 8)])[None, :]
        mq, mk, dq, dk, mvt, dvt, kmean = _project(x2, norm1_g[l][None, :].astype(f32), wqk, wvt, gqk, gsum)
        nblk = PROJ_TM // MOBA_BLOCK
        kmean = kmean.reshape(T // PROJ_TM, SUBLANES, 512)[:, :nblk].reshape(T // MOBA_BLOCK, 512)
        moba_o = _moba_attention(mq, mk, mvt, kmean, moba_slopes, b, s)
        gsub = jnp.broadcast_to(diff_subln_g[l].astype(f32)[:, None], (LANES, MOBA_BLOCK))
        diff_o = _diff_attention(dq, dk, dvt, diff_lambda[l].astype(f32), gsub, diff_slopes,
                                 _lambda_init(l), b, s)
        keys = peer_sub_keys[l].reshape(2 * PEER_HEADS, PEER_N_KEYS, LANES).astype(bf16)
        x1, xn, experts, gates = _mix(x2, moba_o, diff_o, w_out[l].astype(bf16), norm2_g[l][None, :].astype(f32),
                                      peer_w_query[l].astype(bf16), keys)
        v_packed = _pack_table(peer_v[l]) if PEER_SC_CHUNKS < PEER_CHUNKS else None
        x2 = _peer_experts(xn.reshape(T, SUBLANES, LANES), x1, experts, gates, _pack_table(peer_u[l]), v_packed,
                           peer_v[l].astype(f32))
    return x2.reshape(b, s, d)
```

```python
import functools
import math

import jax
import jax.numpy as jnp
import numpy as np
from jax import lax
from jax.experimental import pallas as pl
from jax.experimental.pallas import tpu as pltpu
from jax.experimental.pallas import tpu_sc as plsc

HEAD_DIM = 64
MOBA_HEADS = 8
MOBA_BLOCK = 256
MOBA_TOPK = 3
DIFF_HEADS = 4
PEER_HEADS = 8
PEER_N_KEYS = 128
PEER_TOPK = 16
RMS_EPS = 1e-6

LANES = 128
SUBLANES = 8
VMEM_LIMIT = 48 * 1024 * 1024

PROJ_TM = 512
MIX_TM = 256
PEER_TT = 64
PACK_ROWS = 512
ATT_QBLOCKS = 2
PEER_GROUP = 8
PEER_CHUNKS = 16
PEER_SC_CHUNKS = 15
SC_LANES = 16
SC_ROWS = 32
SC_TB = 8

NEG_INF = float("-inf")
_NT = (((1,), (1,)), ((), ()))


def _dot(a, b):
    return jnp.dot(a, b, preferred_element_type=jnp.float32)


def _dot_nt(a, b):
    return lax.dot_general(a, b, _NT, preferred_element_type=jnp.float32)


def _split_bf16(x):
    hi = x.astype(jnp.bfloat16)
    lo = (x - hi.astype(jnp.float32)).astype(jnp.bfloat16)
    return hi, lo


def _proj_kernel(x_ref, g1_ref, wqk_ref, wvt_ref, gqk_ref, gsum_ref,
                 mq_ref, mk_ref, dq_ref, dk_ref, mvt_ref, dvt_ref, kmean_ref):
    x = x_ref[...]
    ms = jnp.mean(x * x, axis=-1, keepdims=True)
    h = (x * lax.rsqrt(ms + RMS_EPS) * g1_ref[...]).astype(jnp.bfloat16)
    gsum = gsum_ref[...]
    outs = (mq_ref, mk_ref, dq_ref, dk_ref)
    for sec in range(4):
        for half in range(2):
            c0 = sec * 512 + half * 256
            p = _dot(h, wqk_ref[:, c0:c0 + 256])
            for s in range(2):
                ps = p[:, s * LANES:(s + 1) * LANES]
                sq_hi, sq_lo = _split_bf16(ps * ps)
                ssq = _dot(sq_hi, gsum) + _dot(sq_lo, gsum)
                g = gqk_ref[:, c0 + s * LANES:c0 + (s + 1) * LANES]
                y = ps * lax.rsqrt(ssq * (1.0 / HEAD_DIM) + RMS_EPS) * g
                l0 = half * 256 + s * LANES
                outs[sec][:, l0:l0 + LANES] = y.astype(jnp.bfloat16)
                if sec == 1:
                    for blk in range(PROJ_TM // MOBA_BLOCK):
                        kmean_ref[blk:blk + 1, l0:l0 + LANES] = jnp.mean(
                            y[blk * MOBA_BLOCK:(blk + 1) * MOBA_BLOCK], axis=0, keepdims=True)
    kmean_ref[PROJ_TM // MOBA_BLOCK:, :] = jnp.zeros((SUBLANES - PROJ_TM // MOBA_BLOCK, 512), jnp.float32)
    vt = _dot_nt(wvt_ref[...], h).astype(jnp.bfloat16)
    for blk in range(PROJ_TM // MOBA_BLOCK):
        mvt_ref[blk] = vt[0:512, blk * MOBA_BLOCK:(blk + 1) * MOBA_BLOCK]
        dvt_ref[blk] = vt[512:1024, blk * MOBA_BLOCK:(blk + 1) * MOBA_BLOCK]


def _project(x2, g1, wqk, wvt, gqk, gsum):
    T = x2.shape[0]
    nt = T // PROJ_TM
    nblk = PROJ_TM // MOBA_BLOCK
    row = lambda i: (i, 0)
    const = lambda i: (0, 0)
    qk_shape = jax.ShapeDtypeStruct((T, 512), jnp.bfloat16)
    vt_shape = jax.ShapeDtypeStruct((T // MOBA_BLOCK, 512, MOBA_BLOCK), jnp.bfloat16)
    return pl.pallas_call(
        _proj_kernel,
        grid=(nt,),
        in_specs=[
            pl.BlockSpec((PROJ_TM, 1024), row),
            pl.BlockSpec((1, 1024), const),
            pl.BlockSpec((1024, 2048), const),
            pl.BlockSpec((1024, 1024), const),
            pl.BlockSpec((1, 2048), const),
            pl.BlockSpec((LANES, LANES), const),
        ],
        out_specs=[
            pl.BlockSpec((PROJ_TM, 512), row),
            pl.BlockSpec((PROJ_TM, 512), row),
            pl.BlockSpec((PROJ_TM, 512), row),
            pl.BlockSpec((PROJ_TM, 512), row),
            pl.BlockSpec((nblk, 512, MOBA_BLOCK), lambda i: (i, 0, 0)),
            pl.BlockSpec((nblk, 512, MOBA_BLOCK), lambda i: (i, 0, 0)),
            pl.BlockSpec((SUBLANES, 512), row),
        ],
        out_shape=[qk_shape, qk_shape, qk_shape, qk_shape, vt_shape, vt_shape,
                   jax.ShapeDtypeStruct((nt * SUBLANES, 512), jnp.float32)],
        compiler_params=pltpu.CompilerParams(dimension_semantics=("arbitrary",), vmem_limit_bytes=VMEM_LIMIT),
        name="proj",
    )(x2, g1, wqk, wvt, gqk, gsum)


def _rel_qk():
    qc = lax.broadcasted_iota(jnp.int32, (MOBA_BLOCK, MOBA_BLOCK), 1)
    kr = lax.broadcasted_iota(jnp.int32, (MOBA_BLOCK, MOBA_BLOCK), 0)
    return qc - kr


def _attend(q_h, k_ref, vt_load, i, slope, rel, selbias):
    nb = rel.astype(jnp.float32) * (-slope)
    ts, rowbs, colmax = [], [], []
    for j in range(i + 1):
        t = _dot_nt(k_ref[j * MOBA_BLOCK:(j + 1) * MOBA_BLOCK, :], q_h) + nb
        if j == i:
            t = jnp.where(rel >= 0, t, NEG_INF)
            rowb = jnp.zeros((1, MOBA_BLOCK), jnp.float32)
        else:
            rowb = jnp.full((1, MOBA_BLOCK), (i - j) * MOBA_BLOCK, jnp.float32) * (-slope)
            if selbias is not None:
                rowb = rowb + selbias[j:j + 1, :]
        ts.append(t)
        rowbs.append(rowb)
        colmax.append(jnp.max(t, axis=0, keepdims=True) + rowb)
    m = functools.reduce(jnp.maximum, colmax)
    l, acc = None, None
    for j in range(i + 1):
        p = jnp.exp(ts[j] - (m - rowbs[j]))
        lj = jnp.sum(p, axis=0, keepdims=True)
        aj = _dot(vt_load(j), p.astype(jnp.bfloat16))
        l = lj if l is None else l + lj
        acc = aj if acc is None else acc + aj
    return acc / l


def _head_queries(q2):
    lane = lax.broadcasted_iota(jnp.int32, q2.shape, 1)
    return [jnp.where((lane >= hh * HEAD_DIM) & (lane < (hh + 1) * HEAD_DIM), q2, jnp.zeros_like(q2))
            for hh in range(2)]


def _block_selection(gate, i):
    blk = lax.broadcasted_iota(jnp.int32, gate.shape, 0)
    cnt = jnp.zeros(gate.shape, jnp.int32)
    for mblk in range(i):
        gm = gate[mblk:mblk + 1, :]
        beats = (gm > gate) | ((gm == gate) & (mblk < blk))
        cnt = cnt + jnp.where(beats, 1, 0)
    return jnp.where((blk < i) & (cnt < MOBA_TOPK), 0.0, NEG_INF)


def _moba_kernel(slopes_ref, q_ref, k_ref, vt_ref, kmean_ref, o_ref):
    pair = pl.program_id(1)
    step = pl.program_id(2)
    nq = vt_ref.shape[0]
    rel = _rel_qk()
    km_hi, km_lo = _split_bf16(kmean_ref[...])
    for st in range(nq // ATT_QBLOCKS):
        @pl.when(step == st)
        def _(st=st):
            for u in range(ATT_QBLOCKS):
                i = st * ATT_QBLOCKS + u
                rows = slice(u * MOBA_BLOCK, (u + 1) * MOBA_BLOCK)
                outs = []
                for hh, q_h in enumerate(_head_queries(q_ref[rows, :])):
                    gate = _dot_nt(km_hi, q_h) + _dot_nt(km_lo, q_h)
                    vt_load = lambda j, hh=hh: vt_ref[j, hh * HEAD_DIM:(hh + 1) * HEAD_DIM, :]
                    outs.append(_attend(q_h, k_ref, vt_load, i, slopes_ref[pair, hh], rel,
                                        _block_selection(gate, i)))
                o_t = jnp.concatenate(outs, axis=0)
                o_ref[rows, :] = o_t.T.astype(o_ref.dtype)


def _diff_kernel(slopes_ref, lam_ref, q_ref, k_ref, vt_ref, gsub_ref, o_ref, *, lam_init):
    head = pl.program_id(1)
    step = pl.program_id(2)
    nq = vt_ref.shape[0]
    rel = _rel_qk()
    slope = slopes_ref[head, 0]
    lp = lam_ref[...]
    lam = (jnp.exp(jnp.sum(lp[0:1] * lp[1:2], axis=-1, keepdims=True))
           - jnp.exp(jnp.sum(lp[2:3] * lp[3:4], axis=-1, keepdims=True)) + lam_init)
    vt_load = lambda j: vt_ref[j]
    for st in range(nq // ATT_QBLOCKS):
        @pl.when(step == st)
        def _(st=st):
            for u in range(ATT_QBLOCKS):
                i = st * ATT_QBLOCKS + u
                rows = slice(u * MOBA_BLOCK, (u + 1) * MOBA_BLOCK)
                parts = [_attend(q_h, k_ref, vt_load, i, slope, rel, None)
                         for q_h in _head_queries(q_ref[rows, :])]
                o_t = parts[0] - lam * parts[1]
                ms = jnp.mean(o_t * o_t, axis=0, keepdims=True)
                o_t = o_t * lax.rsqrt(ms + RMS_EPS) * gsub_ref[...] * (1.0 - lam_init)
                o_ref[rows, :] = o_t.T.astype(o_ref.dtype)


def _attention_specs(nq):
    nst = nq // ATT_QBLOCKS
    q_spec = pl.BlockSpec((ATT_QBLOCKS * MOBA_BLOCK, LANES), lambda b, p, i: (b * nst + i, p))
    k_spec = pl.BlockSpec((nq * MOBA_BLOCK, LANES), lambda b, p, i: (b, p))
    vt_spec = pl.BlockSpec((nq, LANES, MOBA_BLOCK), lambda b, p, i: (b, p, 0))
    return q_spec, k_spec, vt_spec


def _moba_attention(q, k, vt, kmean, slopes, batch, seq):
    nq = seq // MOBA_BLOCK
    q_spec, k_spec, vt_spec = _attention_specs(nq)
    return pl.pallas_call(
        _moba_kernel,
        grid=(batch, MOBA_HEADS // 2, nq // ATT_QBLOCKS),
        in_specs=[
            pl.BlockSpec(memory_space=pltpu.SMEM),
            q_spec, k_spec, vt_spec,
            pl.BlockSpec((SUBLANES, LANES), lambda b, p, i: (b, p)),
        ],
        out_specs=q_spec,
        out_shape=jax.ShapeDtypeStruct(q.shape, jnp.bfloat16),
        compiler_params=pltpu.CompilerParams(
            dimension_semantics=("arbitrary", "arbitrary", "arbitrary"), vmem_limit_bytes=VMEM_LIMIT),
        name="moba",
    )(slopes, q, k, vt, kmean)


def _diff_attention(q, k, vt, lam_p, gsub, slopes, lam_init, batch, seq):
    nq = seq // MOBA_BLOCK
    q_spec, k_spec, vt_spec = _attention_specs(nq)
    return pl.pallas_call(
        functools.partial(_diff_kernel, lam_init=lam_init),
        grid=(batch, DIFF_HEADS, nq // ATT_QBLOCKS),
        in_specs=[
            pl.BlockSpec(memory_space=pltpu.SMEM),
            pl.BlockSpec((4, HEAD_DIM), lambda b, p, i: (0, 0)),
            q_spec, k_spec, vt_spec,
            pl.BlockSpec((LANES, MOBA_BLOCK), lambda b, p, i: (0, 0)),
        ],
        out_specs=q_spec,
        out_shape=jax.ShapeDtypeStruct(q.shape, jnp.bfloat16),
        compiler_params=pltpu.CompilerParams(
            dimension_semantics=("arbitrary", "arbitrary", "arbitrary"), vmem_limit_bytes=VMEM_LIMIT),
        name="diff",
    )(slopes, lam_p, q, k, vt, gsub)


def _knockout_topk(sc, k, payload=None):
    n = sc.shape[0]
    row = lax.broadcasted_iota(jnp.int32, sc.shape, 0).astype(jnp.float32)
    vals, picks = [], []
    for _ in range(k):
        m = jnp.max(sc, axis=0, keepdims=True)
        idx = jnp.min(jnp.where(sc == m, row, float(n)), axis=0, keepdims=True)
        hit = row == idx
        vals.append(m)
        picks.append(idx if payload is None else jnp.sum(jnp.where(hit, payload, 0.0), axis=0, keepdims=True))
        sc = jnp.where(hit, NEG_INF, sc)
    return vals, picks


def _mix_kernel(x_ref, am_ref, ad_ref, wo_ref, g2_ref, wq_ref, keys_ref,
                x1_ref, xn_ref, exp_ref, gate_ref):
    x1 = x_ref[...] + _dot(am_ref[...], wo_ref[0:512, :]) + _dot(ad_ref[...], wo_ref[512:1024, :])
    x1_ref[...] = x1
    ms = jnp.mean(x1 * x1, axis=-1, keepdims=True)
    xn = x1 * lax.rsqrt(ms + RMS_EPS) * g2_ref[...]
    xn_ref[...] = xn
    qq = _dot(xn.astype(jnp.bfloat16), wq_ref[...]).astype(jnp.bfloat16)
    tm = qq.shape[0]
    exp_rows, gate_rows = [], []
    for h in range(PEER_HEADS):
        tops = []
        for half in range(2):
            s = h * 2 + half
            sc = _dot_nt(keys_ref[s], qq[:, s * LANES:(s + 1) * LANES])
            tops.append(_knockout_topk(sc, PEER_TOPK))
        (s1, i1), (s2, i2) = tops
        s2a = jnp.concatenate(s2, axis=0)
        i2a = jnp.concatenate(i2, axis=0)
        cand, cidx = [], []
        for a in range(PEER_TOPK):
            nb = PEER_TOPK // (a + 1)
            cand.append(s1[a] + s2a[0:nb])
            cidx.append(i1[a] * float(PEER_N_KEYS) + i2a[0:nb])
        n_cand = sum(c.shape[0] for c in cand)
        pad = -n_cand % SUBLANES
        cand.append(jnp.full((pad, tm), NEG_INF, jnp.float32))
        cidx.append(jnp.zeros((pad, tm), jnp.float32))
        best, experts = _knockout_topk(jnp.concatenate(cand, axis=0), PEER_TOPK,
                                       payload=jnp.concatenate(cidx, axis=0))
        b = jnp.concatenate(best, axis=0)
        e = jnp.exp(b - b[0:1])
        gate_rows.append(e / jnp.sum(e, axis=0, keepdims=True))
        exp_rows.append(jnp.concatenate(experts, axis=0))
    exp_ref[...] = jnp.concatenate(exp_rows, axis=0).T.astype(jnp.int32)
    gate_ref[...] = jnp.concatenate(gate_rows, axis=0).T


def _mix(x2, am, ad, wo, g2, wq, keys):
    T = x2.shape[0]
    row = lambda i: (i, 0)
    const = lambda i: (0, 0)
    return pl.pallas_call(
        _mix_kernel,
        grid=(T // MIX_TM,),
        in_specs=[
            pl.BlockSpec((MIX_TM, 1024), row),
            pl.BlockSpec((MIX_TM, 512), row),
            pl.BlockSpec((MIX_TM, 512), row),
            pl.BlockSpec((1024, 1024), const),
            pl.BlockSpec((1, 1024), const),
            pl.BlockSpec((1024, 2048), const),
            pl.BlockSpec((2 * PEER_HEADS, PEER_N_KEYS, LANES), lambda i: (0, 0, 0)),
        ],
        out_specs=[
            pl.BlockSpec((MIX_TM, 1024), row),
            pl.BlockSpec((MIX_TM, 1024), row),
            pl.BlockSpec((MIX_TM, LANES), row),
            pl.BlockSpec((MIX_TM, LANES), row),
        ],
        out_shape=[
            jax.ShapeDtypeStruct((T, 1024), jnp.float32),
            jax.ShapeDtypeStruct((T, 1024), jnp.float32),
            jax.ShapeDtypeStruct((T, LANES), jnp.int32),
            jax.ShapeDtypeStruct((T, LANES), jnp.float32),
        ],
        compiler_params=pltpu.CompilerParams(dimension_semantics=("arbitrary",), vmem_limit_bytes=VMEM_LIMIT),
        name="mix",
    )(x2, am, ad, wo, g2, wq, keys)


def _pack_kernel(t_ref, o_ref):
    t = t_ref[...]
    lo = pltpu.bitcast(t[:, 0:512].astype(jnp.bfloat16).astype(jnp.float32), jnp.int32)
    hi = pltpu.bitcast(t[:, 512:1024].astype(jnp.bfloat16).astype(jnp.float32), jnp.int32)
    o_ref[...] = lax.shift_right_logical(lo, 16) | (hi & jnp.int32(-65536))


def _pack_table(tab):
    n = tab.shape[0]
    return pl.pallas_call(
        _pack_kernel,
        grid=(n // PACK_ROWS,),
        in_specs=[pl.BlockSpec((PACK_ROWS, 1024), lambda i: (i, 0))],
        out_specs=pl.BlockSpec((PACK_ROWS, 512), lambda i: (i, 0)),
        out_shape=jax.ShapeDtypeStruct((n, 512), jnp.int32),
        compiler_params=pltpu.CompilerParams(dimension_semantics=("arbitrary",)),
        name="pack",
    )(tab)


def _decode(w):
    lo = pltpu.bitcast(w << 16, jnp.float32)
    hi = pltpu.bitcast(w & jnp.int32(-65536), jnp.float32)
    return lo, hi


def _peer_u_kernel(exp_ref, tab_ref, x_ref, g_ref, act_ref, q0_ref, q1_ref, qs_ref):
    ntok, nsel = exp_ref.shape
    ones = jnp.ones((SUBLANES, LANES), jnp.bfloat16)

    def products(t, q_ref):
        x = x_ref[t]
        xlo = x[0:4]
        xhi = x[4:8]
        for e in range(nsel):
            lo, hi = _decode(tab_ref[exp_ref[t, e]])
            q_ref[pl.ds(4 * e, 4), :] = lo * xlo + hi * xhi

    def fold(t, q_ref):
        q = q_ref[pl.ds(0, nsel, stride=4), :]
        for s in range(1, 4):
            q = q + q_ref[pl.ds(s, nsel, stride=4), :]
        qs_ref[pl.ds(pl.multiple_of(t * nsel, nsel), nsel), :] = q

    def gather_group(g):
        for k in range(PEER_GROUP):
            q_ref = (q0_ref, q1_ref)[k % 2]
            products(g * PEER_GROUP + k, q_ref)
            fold(g * PEER_GROUP + k, q_ref)

    def finish_group(g):
        rows8 = PEER_GROUP * nsel
        q_hi, q_lo = _split_bf16(qs_ref[pl.ds(pl.multiple_of(g * rows8, rows8), rows8), :])
        row = (_dot_nt(ones, q_hi) + _dot_nt(ones, q_lo))[0:1]
        h = jnp.concatenate([row[:, k * nsel:(k + 1) * nsel] for k in range(PEER_GROUP)], axis=0)
        gelu = 0.5 * h * (1.0 + lax.erf(h * (1.0 / math.sqrt(2.0))))
        rows = pl.ds(pl.multiple_of(g * PEER_GROUP, PEER_GROUP), PEER_GROUP)
        act_ref[rows, :] = gelu * g_ref[rows, :]

    ngroups = ntok // PEER_GROUP
    gather_group(0)

    def body(g, carry):
        finish_group(g - 1)
        gather_group(g)
        return carry

    lax.fori_loop(1, ngroups, body, 0)
    finish_group(ngroups - 1)


def _peer_v_kernel(exp_ref, act_ref, tab_ref, x_ref, o_ref):
    nsel = exp_ref.shape[1]

    def tok(t, carry):
        acc_lo = jnp.zeros((4, LANES), jnp.float32)
        acc_hi = jnp.zeros((4, LANES), jnp.float32)
        for e in range(nsel):
            a = act_ref[t, e]
            lo, hi = _decode(tab_ref[exp_ref[t, e]])
            acc_lo = acc_lo + a * lo
            acc_hi = acc_hi + a * hi
        o_ref[t] = x_ref[t] + jnp.concatenate([acc_lo, acc_hi], axis=0)
        return carry

    lax.fori_loop(0, x_ref.shape[0], tok, 0)


def _sc_peer_v(vtab, idx, actrep, x1, tok0, ntok):
    d = x1.shape[1]
    half = d // 2
    nsel = idx.shape[1]
    info = plsc.get_sparse_core_info()
    nw = info.num_cores * info.num_subcores
    per_w = ntok // nw
    nchunk = nsel // SC_ROWS
    nitem = SC_TB * nchunk
    assert info.num_lanes == SC_LANES and ntok % (nw * SC_TB) == 0 and nitem % 2 == 0
    mesh = plsc.VectorSubcoreMesh(core_axis_name="c", subcore_axis_name="s")

    @functools.partial(
        pl.kernel, mesh=mesh,
        out_type=jax.ShapeDtypeStruct((ntok, d), jnp.float32),
        scratch_types=[
            pltpu.VMEM((SC_TB, nsel), jnp.int32),
            pltpu.VMEM((SC_TB, nsel * SC_LANES), jnp.float32),
            pltpu.VMEM((SC_ROWS, half), jnp.int32),
            pltpu.VMEM((SC_ROWS, half), jnp.int32),
            pltpu.VMEM((SC_TB, d), jnp.float32),
            pltpu.SemaphoreType.DMA,
            pltpu.SemaphoreType.DMA,
        ],
        compiler_params=pltpu.CompilerParams(needs_layout_passes=False),
        name="sc_peer_v",
    )
    def k(vtab_hbm, idx_hbm, act_hbm, x1_hbm, out_hbm, idx_v, act_v, rows0, rows1, acc_v, sem0, sem1):
        wid = lax.axis_index("s") * info.num_cores + lax.axis_index("c")

        def gather(item, rows, sem):
            tl = item // nchunk
            ch = item % nchunk
            return pltpu.make_async_copy(vtab_hbm.at[idx_v.at[tl, pl.ds(ch * SC_ROWS, SC_ROWS)]], rows, sem)

        def accumulate(item, rows):
            tl = item // nchunk
            r0 = (item % nchunk) * SC_ROWS
            a = [act_v[tl, pl.ds((r0 + e) * SC_LANES, SC_LANES)] for e in range(SC_ROWS)]

            @plsc.parallel_loop(0, half // SC_LANES, 1, unroll=2)
            def _(c):
                col = pl.ds(c * SC_LANES, SC_LANES)
                col_hi = pl.ds(half + c * SC_LANES, SC_LANES)
                sum_lo, sum_hi = acc_v[tl, col], acc_v[tl, col_hi]
                for e0 in range(0, SC_ROWS, 4):
                    los, his = [], []
                    for e in range(e0, e0 + 4):
                        w = rows[e, col]
                        los.append(a[e] * plsc.bitcast(w << 16, jnp.float32))
                        his.append(a[e] * plsc.bitcast(w & jnp.int32(-65536), jnp.float32))
                    sum_lo = sum_lo + ((los[0] + los[1]) + (los[2] + los[3]))
                    sum_hi = sum_hi + ((his[0] + his[1]) + (his[2] + his[3]))
                acc_v[tl, col] = sum_lo
                acc_v[tl, col_hi] = sum_hi

        @pl.loop(0, per_w // SC_TB)
        def _(bi):
            loc = pl.multiple_of(wid * per_w + bi * SC_TB, SC_TB)
            src = pl.ds(tok0 + loc, SC_TB)
            pltpu.sync_copy(idx_hbm.at[src], idx_v)
            pltpu.sync_copy(act_hbm.at[pl.ds(loc, SC_TB)], act_v)
            pltpu.sync_copy(x1_hbm.at[src], acc_v)
            gather(0, rows0, sem0).start()

            @pl.loop(0, nitem // 2)
            def _(p):
                gather(2 * p + 1, rows1, sem1).start()
                gather(2 * p, rows0, sem0).wait()
                accumulate(2 * p, rows0)

                @pl.when(p < nitem // 2 - 1)
                def _():
                    gather(2 * p + 2, rows0, sem0).start()

                gather(2 * p + 1, rows1, sem1).wait()
                accumulate(2 * p + 1, rows1)

            pltpu.sync_copy(acc_v, out_hbm.at[pl.ds(loc, SC_TB)])

    return k(vtab, idx, actrep, x1)


def _peer_experts(xn3, x1, experts, gates, u_words, v_words):
    T, nsel = experts.shape
    n_exp = u_words.shape[0]
    d = x1.shape[1]
    u_packed = u_words.reshape(n_exp, 4, LANES)
    v_packed = v_words.reshape(n_exp, 4, LANES)
    tc = T // PEER_CHUNKS
    steps = tc // PEER_TT
    assert nsel == LANES and PEER_TT % PEER_GROUP == 0 and T % (PEER_CHUNKS * PEER_TT) == 0
    x13 = x1.reshape(T, SUBLANES, LANES)
    tab_spec = pl.BlockSpec((n_exp, 4, LANES), lambda i: (0, 0, 0), pipeline_mode=pl.Buffered(1))
    cp = pltpu.CompilerParams(dimension_semantics=("arbitrary",), vmem_limit_bytes=VMEM_LIMIT)
    outs = []
    for c in range(PEER_CHUNKS):
        off = c * steps
        row = lambda i, off=off: (i + off, 0)
        row3 = lambda i, off=off: (i + off, 0, 0)
        smem_blk = lambda: pl.BlockSpec((PEER_TT, nsel), row, memory_space=pltpu.SMEM)
        x_spec = pl.BlockSpec((PEER_TT, SUBLANES, LANES), row3)
        act = pl.pallas_call(
            _peer_u_kernel,
            grid=(steps,),
            in_specs=[smem_blk(), tab_spec, x_spec, pl.BlockSpec((PEER_TT, nsel), row)],
            out_specs=pl.BlockSpec((PEER_TT, nsel), lambda i: (i, 0)),
            out_shape=jax.ShapeDtypeStruct((tc, nsel), jnp.float32),
            scratch_shapes=[pltpu.VMEM((4 * nsel, LANES), jnp.float32), pltpu.VMEM((4 * nsel, LANES), jnp.float32),
                            pltpu.VMEM((PEER_TT * nsel, LANES), jnp.float32)],
            compiler_params=cp,
            name="peer_u",
        )(experts, u_packed, xn3, gates)
        if c < PEER_SC_CHUNKS:
            actrep = jnp.broadcast_to(act[:, :, None], (tc, nsel, SC_LANES)).reshape(tc, nsel * SC_LANES)
            outs.append(_sc_peer_v(v_words, experts, actrep, x1, c * tc, tc))
        else:
            out3 = pl.pallas_call(
                _peer_v_kernel,
                grid=(steps,),
                in_specs=[smem_blk(), pl.BlockSpec((PEER_TT, nsel), lambda i: (i, 0), memory_space=pltpu.SMEM),
                          tab_spec, x_spec],
                out_specs=pl.BlockSpec((PEER_TT, SUBLANES, LANES), lambda i: (i, 0, 0)),
                out_shape=jax.ShapeDtypeStruct((tc, SUBLANES, LANES), jnp.float32),
                compiler_params=cp,
                name="peer_v",
            )(experts, act, v_packed, x13)
            outs.append(out3.reshape(tc, d))
    return jnp.concatenate(outs, axis=0)


def _alibi_slopes(n):
    return 2.0 ** (-8.0 * np.arange(1, n + 1) / n)


def _lambda_init(layer):
    return 0.8 - 0.6 * math.exp(-0.3 * layer)


def kernel(x, norm1_g, w_in, moba_qk_norm_g, diff_qk_norm_g, diff_lambda, diff_subln_g, w_out, norm2_g,
           peer_w_query, peer_sub_keys, peer_u, peer_v):
    b, s, d = x.shape
    T = b * s
    depth = w_in.shape[0]
    mw = MOBA_HEADS * HEAD_DIM
    assert d == 1024 and s % PROJ_TM == 0 and mw == 512 and DIFF_HEADS * 2 * HEAD_DIM == 512
    f32, bf16 = jnp.float32, jnp.bfloat16
    scale = HEAD_DIM ** -0.5
    grp = lax.broadcasted_iota(jnp.int32, (LANES, LANES), 0) // HEAD_DIM
    gsum = (grp == grp.T).astype(bf16)
    moba_slopes = jnp.asarray(_alibi_slopes(MOBA_HEADS).reshape(MOBA_HEADS // 2, 2), f32)
    diff_slopes = jnp.asarray(np.repeat(_alibi_slopes(DIFF_HEADS)[:, None], 2, axis=1), f32)
    x2 = x.reshape(T, d)
    for l in range(depth):
        w = w_in[l]
        wqk = jnp.concatenate([w[:, 0:2 * mw], w[:, 3 * mw:3 * mw + 1024]], axis=1).astype(bf16)
        wvt = jnp.concatenate([w[:, 2 * mw:3 * mw], w[:, 3 * mw + 1024:]], axis=1).T.astype(bf16)
        mg, dg = moba_qk_norm_g[l].astype(f32), diff_qk_norm_g[l].astype(f32)
        gqk = jnp.concatenate([jnp.tile(mg[0] * scale, 8), jnp.tile(mg[1], 8),
                               jnp.tile(dg[0] * scale, 8), jnp.tile(dg[1], 8)])[None, :]
        mq, mk, dq, dk, mvt, dvt, kmean = _project(x2, norm1_g[l][None, :].astype(f32), wqk, wvt, gqk, gsum)
        nblk = PROJ_TM // MOBA_BLOCK
        kmean = kmean.reshape(T // PROJ_TM, SUBLANES, 512)[:, :nblk].reshape(T // MOBA_BLOCK, 512)
        moba_o = _moba_attention(mq, mk, mvt, kmean, moba_slopes, b, s)
        gsub = jnp.broadcast_to(diff_subln_g[l].astype(f32)[:, None], (LANES, MOBA_BLOCK))
        diff_o = _diff_attention(dq, dk, dvt, diff_lambda[l].astype(f32), gsub, diff_slopes,
                                 _lambda_init(l), b, s)
        keys = peer_sub_keys[l].reshape(2 * PEER_HEADS, PEER_N_KEYS, LANES).astype(bf16)
        x1, xn, experts, gates = _mix(x2, moba_o, diff_o, w_out[l].astype(bf16), norm2_g[l][None, :].astype(f32),
                                      peer_w_query[l].astype(bf16), keys)
        x2 = _peer_experts(xn.reshape(T, SUBLANES, LANES), x1, experts, gates,
                           _pack_table(peer_u[l]), _pack_table(peer_v[l]))
    return x2.reshape(b, s, d)
```

```python
import functools
import math

import jax
import jax.numpy as jnp
import numpy as np
from jax import lax
from jax.experimental import pallas as pl
from jax.experimental.pallas import tpu as pltpu
from jax.experimental.pallas import tpu_sc as plsc

HEAD_DIM = 64
MOBA_HEADS = 8
MOBA_BLOCK = 256
MOBA_TOPK = 3
DIFF_HEADS = 4
PEER_HEADS = 8
PEER_N_KEYS = 128
PEER_TOPK = 16
RMS_EPS = 1e-6

LANES = 128
SUBLANES = 8
VMEM_LIMIT = 48 * 1024 * 1024

PROJ_TM = 512
MIX_TM = 256
PEER_TT = 64
PACK_ROWS = 512
ATT_QBLOCKS = 2
PEER_GROUP = 8
PIPE_GROUPS = 4
PEER_CHUNKS = 2
SC_LANES = 16
SC_ROWS = 32
SC_TB = 8

NEG_INF = float("-inf")
_NT = (((1,), (1,)), ((), ()))


def _dot(a, b):
    return jnp.dot(a, b, preferred_element_type=jnp.float32)


def _dot_nt(a, b):
    return lax.dot_general(a, b, _NT, preferred_element_type=jnp.float32)


def _split_bf16(x):
    hi = x.astype(jnp.bfloat16)
    lo = (x - hi.astype(jnp.float32)).astype(jnp.bfloat16)
    return hi, lo


def _proj_kernel(x_ref, g1_ref, wqk_ref, wvt_ref, gqk_ref, gsum_ref,
                 mq_ref, mk_ref, dq_ref, dk_ref, mvt_ref, dvt_ref, kmean_ref):
    x = x_ref[...]
    ms = jnp.mean(x * x, axis=-1, keepdims=True)
    h = (x * lax.rsqrt(ms + RMS_EPS) * g1_ref[...]).astype(jnp.bfloat16)
    gsum = gsum_ref[...]
    outs = (mq_ref, mk_ref, dq_ref, dk_ref)
    for sec in range(4):
        for half in range(2):
            c0 = sec * 512 + half * 256
            p = _dot(h, wqk_ref[:, c0:c0 + 256])
            for s in range(2):
                ps = p[:, s * LANES:(s + 1) * LANES]
                sq_hi, sq_lo = _split_bf16(ps * ps)
                ssq = _dot(sq_hi, gsum) + _dot(sq_lo, gsum)
                g = gqk_ref[:, c0 + s * LANES:c0 + (s + 1) * LANES]
                y = ps * lax.rsqrt(ssq * (1.0 / HEAD_DIM) + RMS_EPS) * g
                l0 = half * 256 + s * LANES
                outs[sec][:, l0:l0 + LANES] = y.astype(jnp.bfloat16)
                if sec == 1:
                    for blk in range(PROJ_TM // MOBA_BLOCK):
                        kmean_ref[blk:blk + 1, l0:l0 + LANES] = jnp.mean(
                            y[blk * MOBA_BLOCK:(blk + 1) * MOBA_BLOCK], axis=0, keepdims=True)
    kmean_ref[PROJ_TM // MOBA_BLOCK:, :] = jnp.zeros((SUBLANES - PROJ_TM // MOBA_BLOCK, 512), jnp.float32)
    vt = _dot_nt(wvt_ref[...], h).astype(jnp.bfloat16)
    for blk in range(PROJ_TM // MOBA_BLOCK):
        mvt_ref[blk] = vt[0:512, blk * MOBA_BLOCK:(blk + 1) * MOBA_BLOCK]
        dvt_ref[blk] = vt[512:1024, blk * MOBA_BLOCK:(blk + 1) * MOBA_BLOCK]


def _project(x2, g1, wqk, wvt, gqk, gsum):
    T = x2.shape[0]
    nt = T // PROJ_TM
    nblk = PROJ_TM // MOBA_BLOCK
    row = lambda i: (i, 0)
    const = lambda i: (0, 0)
    qk_shape = jax.ShapeDtypeStruct((T, 512), jnp.bfloat16)
    vt_shape = jax.ShapeDtypeStruct((T // MOBA_BLOCK, 512, MOBA_BLOCK), jnp.bfloat16)
    return pl.pallas_call(
        _proj_kernel,
        grid=(nt,),
        in_specs=[
            pl.BlockSpec((PROJ_TM, 1024), row),
            pl.BlockSpec((1, 1024), const),
            pl.BlockSpec((1024, 2048), const),
            pl.BlockSpec((1024, 1024), const),
            pl.BlockSpec((1, 2048), const),
            pl.BlockSpec((LANES, LANES), const),
        ],
        out_specs=[
            pl.BlockSpec((PROJ_TM, 512), row),
            pl.BlockSpec((PROJ_TM, 512), row),
            pl.BlockSpec((PROJ_TM, 512), row),
            pl.BlockSpec((PROJ_TM, 512), row),
            pl.BlockSpec((nblk, 512, MOBA_BLOCK), lambda i: (i, 0, 0)),
            pl.BlockSpec((nblk, 512, MOBA_BLOCK), lambda i: (i, 0, 0)),
            pl.BlockSpec((SUBLANES, 512), row),
        ],
        out_shape=[qk_shape, qk_shape, qk_shape, qk_shape, vt_shape, vt_shape,
                   jax.ShapeDtypeStruct((nt * SUBLANES, 512), jnp.float32)],
        compiler_params=pltpu.CompilerParams(dimension_semantics=("arbitrary",), vmem_limit_bytes=VMEM_LIMIT),
        name="proj",
    )(x2, g1, wqk, wvt, gqk, gsum)


def _rel_qk():
    qc = lax.broadcasted_iota(jnp.int32, (MOBA_BLOCK, MOBA_BLOCK), 1)
    kr = lax.broadcasted_iota(jnp.int32, (MOBA_BLOCK, MOBA_BLOCK), 0)
    return qc - kr


def _attend(q_h, k_ref, vt_load, i, slope, rel, selbias):
    nb = rel.astype(jnp.float32) * (-slope)
    ts, rowbs, colmax = [], [], []
    for j in range(i + 1):
        t = _dot_nt(k_ref[j * MOBA_BLOCK:(j + 1) * MOBA_BLOCK, :], q_h) + nb
        if j == i:
            t = jnp.where(rel >= 0, t, NEG_INF)
            rowb = jnp.zeros((1, MOBA_BLOCK), jnp.float32)
        else:
            rowb = jnp.full((1, MOBA_BLOCK), (i - j) * MOBA_BLOCK, jnp.float32) * (-slope)
            if selbias is not None:
                rowb = rowb + selbias[j:j + 1, :]
        ts.append(t)
        rowbs.append(rowb)
        colmax.append(jnp.max(t, axis=0, keepdims=True) + rowb)
    m = functools.reduce(jnp.maximum, colmax)
    l, acc = None, None
    for j in range(i + 1):
        p = jnp.exp(ts[j] - (m - rowbs[j]))
        lj = jnp.sum(p, axis=0, keepdims=True)
        aj = _dot(vt_load(j), p.astype(jnp.bfloat16))
        l = lj if l is None else l + lj
        acc = aj if acc is None else acc + aj
    return acc / l


def _head_queries(q2):
    lane = lax.broadcasted_iota(jnp.int32, q2.shape, 1)
    return [jnp.where((lane >= hh * HEAD_DIM) & (lane < (hh + 1) * HEAD_DIM), q2, jnp.zeros_like(q2))
            for hh in range(2)]


def _block_selection(gate, i):
    blk = lax.broadcasted_iota(jnp.int32, gate.shape, 0)
    cnt = jnp.zeros(gate.shape, jnp.int32)
    for mblk in range(i):
        gm = gate[mblk:mblk + 1, :]
        beats = (gm > gate) | ((gm == gate) & (mblk < blk))
        cnt = cnt + jnp.where(beats, 1, 0)
    return jnp.where((blk < i) & (cnt < MOBA_TOPK), 0.0, NEG_INF)


def _moba_kernel(slopes_ref, q_ref, k_ref, vt_ref, kmean_ref, o_ref):
    pair = pl.program_id(1)
    step = pl.program_id(2)
    nq = vt_ref.shape[0]
    rel = _rel_qk()
    km_hi, km_lo = _split_bf16(kmean_ref[...])
    for st in range(nq // ATT_QBLOCKS):
        @pl.when(step == st)
        def _(st=st):
            for u in range(ATT_QBLOCKS):
                i = st * ATT_QBLOCKS + u
                rows = slice(u * MOBA_BLOCK, (u + 1) * MOBA_BLOCK)
                outs = []
                for hh, q_h in enumerate(_head_queries(q_ref[rows, :])):
                    gate = _dot_nt(km_hi, q_h) + _dot_nt(km_lo, q_h)
                    vt_load = lambda j, hh=hh: vt_ref[j, hh * HEAD_DIM:(hh + 1) * HEAD_DIM, :]
                    outs.append(_attend(q_h, k_ref, vt_load, i, slopes_ref[pair, hh], rel,
                                        _block_selection(gate, i)))
                o_t = jnp.concatenate(outs, axis=0)
                o_ref[rows, :] = o_t.T.astype(o_ref.dtype)


def _diff_kernel(slopes_ref, lam_ref, q_ref, k_ref, vt_ref, gsub_ref, o_ref, *, lam_init):
    head = pl.program_id(1)
    step = pl.program_id(2)
    nq = vt_ref.shape[0]
    rel = _rel_qk()
    slope = slopes_ref[head, 0]
    lp = lam_ref[...]
    lam = (jnp.exp(jnp.sum(lp[0:1] * lp[1:2], axis=-1, keepdims=True))
           - jnp.exp(jnp.sum(lp[2:3] * lp[3:4], axis=-1, keepdims=True)) + lam_init)
    vt_load = lambda j: vt_ref[j]
    for st in range(nq // ATT_QBLOCKS):
        @pl.when(step == st)
        def _(st=st):
            for u in range(ATT_QBLOCKS):
                i = st * ATT_QBLOCKS + u
                rows = slice(u * MOBA_BLOCK, (u + 1) * MOBA_BLOCK)
                parts = [_attend(q_h, k_ref, vt_load, i, slope, rel, None)
                         for q_h in _head_queries(q_ref[rows, :])]
                o_t = parts[0] - lam * parts[1]
                ms = jnp.mean(o_t * o_t, axis=0, keepdims=True)
                o_t = o_t * lax.rsqrt(ms + RMS_EPS) * gsub_ref[...] * (1.0 - lam_init)
                o_ref[rows, :] = o_t.T.astype(o_ref.dtype)


def _attention_specs(nq):
    nst = nq // ATT_QBLOCKS
    q_spec = pl.BlockSpec((ATT_QBLOCKS * MOBA_BLOCK, LANES), lambda b, p, i: (b * nst + i, p))
    k_spec = pl.BlockSpec((nq * MOBA_BLOCK, LANES), lambda b, p, i: (b, p))
    vt_spec = pl.BlockSpec((nq, LANES, MOBA_BLOCK), lambda b, p, i: (b, p, 0))
    return q_spec, k_spec, vt_spec


def _moba_attention(q, k, vt, kmean, slopes, batch, seq):
    nq = seq // MOBA_BLOCK
    q_spec, k_spec, vt_spec = _attention_specs(nq)
    return pl.pallas_call(
        _moba_kernel,
        grid=(batch, MOBA_HEADS // 2, nq // ATT_QBLOCKS),
        in_specs=[
            pl.BlockSpec(memory_space=pltpu.SMEM),
            q_spec, k_spec, vt_spec,
            pl.BlockSpec((SUBLANES, LANES), lambda b, p, i: (b, p)),
        ],
        out_specs=q_spec,
        out_shape=jax.ShapeDtypeStruct(q.shape, jnp.bfloat16),
        compiler_params=pltpu.CompilerParams(
            dimension_semantics=("arbitrary", "arbitrary", "arbitrary"), vmem_limit_bytes=VMEM_LIMIT),
        name="moba",
    )(slopes, q, k, vt, kmean)


def _diff_attention(q, k, vt, lam_p, gsub, slopes, lam_init, batch, seq):
    nq = seq // MOBA_BLOCK
    q_spec, k_spec, vt_spec = _attention_specs(nq)
    return pl.pallas_call(
        functools.partial(_diff_kernel, lam_init=lam_init),
        grid=(batch, DIFF_HEADS, nq // ATT_QBLOCKS),
        in_specs=[
            pl.BlockSpec(memory_space=pltpu.SMEM),
            pl.BlockSpec((4, HEAD_DIM), lambda b, p, i: (0, 0)),
            q_spec, k_spec, vt_spec,
            pl.BlockSpec((LANES, MOBA_BLOCK), lambda b, p, i: (0, 0)),
        ],
        out_specs=q_spec,
        out_shape=jax.ShapeDtypeStruct(q.shape, jnp.bfloat16),
        compiler_params=pltpu.CompilerParams(
            dimension_semantics=("arbitrary", "arbitrary", "arbitrary"), vmem_limit_bytes=VMEM_LIMIT),
        name="diff",
    )(slopes, lam_p, q, k, vt, gsub)


def _knockout_topk(sc, k, payload=None):
    n = sc.shape[0]
    row = lax.broadcasted_iota(jnp.int32, sc.shape, 0).astype(jnp.float32)
    vals, picks = [], []
    for _ in range(k):
        m = jnp.max(sc, axis=0, keepdims=True)
        idx = jnp.min(jnp.where(sc == m, row, float(n)), axis=0, keepdims=True)
        hit = row == idx
        vals.append(m)
        picks.append(idx if payload is None else jnp.sum(jnp.where(hit, payload, 0.0), axis=0, keepdims=True))
        sc = jnp.where(hit, NEG_INF, sc)
    return vals, picks


def _mix_kernel(x_ref, am_ref, ad_ref, wo_ref, g2_ref, wq_ref, keys_ref,
                x1_ref, xn_ref, exp_ref, gate_ref):
    x1 = x_ref[...] + _dot(am_ref[...], wo_ref[0:512, :]) + _dot(ad_ref[...], wo_ref[512:1024, :])
    x1_ref[...] = x1
    ms = jnp.mean(x1 * x1, axis=-1, keepdims=True)
    xn = x1 * lax.rsqrt(ms + RMS_EPS) * g2_ref[...]
    xn_ref[...] = xn
    qq = _dot(xn.astype(jnp.bfloat16), wq_ref[...]).astype(jnp.bfloat16)
    tm = qq.shape[0]
    exp_rows, gate_rows = [], []
    for h in range(PEER_HEADS):
        tops = []
        for half in range(2):
            s = h * 2 + half
            sc = _dot_nt(keys_ref[s], qq[:, s * LANES:(s + 1) * LANES])
            tops.append(_knockout_topk(sc, PEER_TOPK))
        (s1, i1), (s2, i2) = tops
        s2a = jnp.concatenate(s2, axis=0)
        i2a = jnp.concatenate(i2, axis=0)
        cand, cidx = [], []
        for a in range(PEER_TOPK):
            nb = PEER_TOPK // (a + 1)
            cand.append(s1[a] + s2a[0:nb])
            cidx.append(i1[a] * float(PEER_N_KEYS) + i2a[0:nb])
        n_cand = sum(c.shape[0] for c in cand)
        pad = -n_cand % SUBLANES
        cand.append(jnp.full((pad, tm), NEG_INF, jnp.float32))
        cidx.append(jnp.zeros((pad, tm), jnp.float32))
        best, experts = _knockout_topk(jnp.concatenate(cand, axis=0), PEER_TOPK,
                                       payload=jnp.concatenate(cidx, axis=0))
        b = jnp.concatenate(best, axis=0)
        e = jnp.exp(b - b[0:1])
        gate_rows.append(e / jnp.sum(e, axis=0, keepdims=True))
        exp_rows.append(jnp.concatenate(experts, axis=0))
    exp_ref[...] = jnp.concatenate(exp_rows, axis=0).T.astype(jnp.int32)
    gate_ref[...] = jnp.concatenate(gate_rows, axis=0).T


def _mix(x2, am, ad, wo, g2, wq, keys):
    T = x2.shape[0]
    row = lambda i: (i, 0)
    const = lambda i: (0, 0)
    return pl.pallas_call(
        _mix_kernel,
        grid=(T // MIX_TM,),
        in_specs=[
            pl.BlockSpec((MIX_TM, 1024), row),
            pl.BlockSpec((MIX_TM, 512), row),
            pl.BlockSpec((MIX_TM, 512), row),
            pl.BlockSpec((1024, 1024), const),
            pl.BlockSpec((1, 1024), const),
            pl.BlockSpec((1024, 2048), const),
            pl.BlockSpec((2 * PEER_HEADS, PEER_N_KEYS, LANES), lambda i: (0, 0, 0)),
        ],
        out_specs=[
            pl.BlockSpec((MIX_TM, 1024), row),
            pl.BlockSpec((MIX_TM, 1024), row),
            pl.BlockSpec((MIX_TM, LANES), row),
            pl.BlockSpec((MIX_TM, LANES), row),
        ],
        out_shape=[
            jax.ShapeDtypeStruct((T, 1024), jnp.float32),
            jax.ShapeDtypeStruct((T, 1024), jnp.float32),
            jax.ShapeDtypeStruct((T, LANES), jnp.int32),
            jax.ShapeDtypeStruct((T, LANES), jnp.float32),
        ],
        compiler_params=pltpu.CompilerParams(dimension_semantics=("arbitrary",), vmem_limit_bytes=VMEM_LIMIT),
        name="mix",
    )(x2, am, ad, wo, g2, wq, keys)


def _pack_kernel(t_ref, o_ref):
    t = t_ref[...]
    lo = pltpu.bitcast(t[:, 0:512].astype(jnp.bfloat16).astype(jnp.float32), jnp.int32)
    hi = pltpu.bitcast(t[:, 512:1024].astype(jnp.bfloat16).astype(jnp.float32), jnp.int32)
    o_ref[...] = lax.shift_right_logical(lo, 16) | (hi & jnp.int32(-65536))


def _pack_table(tab):
    n = tab.shape[0]
    return pl.pallas_call(
        _pack_kernel,
        grid=(n // PACK_ROWS,),
        in_specs=[pl.BlockSpec((PACK_ROWS, 1024), lambda i: (i, 0))],
        out_specs=pl.BlockSpec((PACK_ROWS, 512), lambda i: (i, 0)),
        out_shape=jax.ShapeDtypeStruct((n, 512), jnp.int32),
        compiler_params=pltpu.CompilerParams(dimension_semantics=("arbitrary",)),
        name="pack",
    )(tab)


def _decode(w):
    lo = pltpu.bitcast(w << 16, jnp.float32)
    hi = pltpu.bitcast(w & jnp.int32(-65536), jnp.float32)
    return lo, hi


def _peer_u_kernel(exp_ref, tab_ref, x_ref, g_ref, act_ref, q0_ref, q1_ref, qs_ref):
    ntok, nsel = exp_ref.shape
    ones = jnp.ones((SUBLANES, LANES), jnp.bfloat16)

    def products(t, q_ref):
        x = x_ref[t]
        xlo = x[0:4]
        xhi = x[4:8]
        for e in range(nsel):
            lo, hi = _decode(tab_ref[exp_ref[t, e]])
            q_ref[pl.ds(4 * e, 4), :] = lo * xlo + hi * xhi

    def fold(t, q_ref):
        q = q_ref[pl.ds(0, nsel, stride=4), :]
        for s in range(1, 4):
            q = q + q_ref[pl.ds(s, nsel, stride=4), :]
        qs_ref[pl.ds(pl.multiple_of(t * nsel, nsel), nsel), :] = q

    def gather_group(g):
        for k in range(PEER_GROUP):
            q_ref = (q0_ref, q1_ref)[k % 2]
            products(g * PEER_GROUP + k, q_ref)
            fold(g * PEER_GROUP + k, q_ref)

    def finish_group(g):
        rows8 = PEER_GROUP * nsel
        q_hi, q_lo = _split_bf16(qs_ref[pl.ds(pl.multiple_of(g * rows8, rows8), rows8), :])
        row = (_dot_nt(ones, q_hi) + _dot_nt(ones, q_lo))[0:1]
        h = jnp.concatenate([row[:, k * nsel:(k + 1) * nsel] for k in range(PEER_GROUP)], axis=0)
        gelu = 0.5 * h * (1.0 + lax.erf(h * (1.0 / math.sqrt(2.0))))
        rows = pl.ds(pl.multiple_of(g * PEER_GROUP, PEER_GROUP), PEER_GROUP)
        act_ref[rows, :] = gelu * g_ref[rows, :]

    ngroups = ntok // PEER_GROUP
    gather_group(0)

    def body(g, carry):
        finish_group(g - 1)
        gather_group(g)
        return carry

    lax.fori_loop(1, ngroups, body, 0)
    finish_group(ngroups - 1)


def _sc_peer_v(vtab, idx, actrep, x1, tok0, ntok):
    d = x1.shape[1]
    nsel = idx.shape[1]
    info = plsc.get_sparse_core_info()
    nw = info.num_cores * info.num_subcores
    per_w = ntok // nw
    nchunk = nsel // SC_ROWS
    nitem = SC_TB * nchunk
    assert info.num_lanes == SC_LANES and ntok % (nw * SC_TB) == 0 and nitem % 2 == 0
    mesh = plsc.VectorSubcoreMesh(core_axis_name="c", subcore_axis_name="s")

    @functools.partial(
        pl.kernel, mesh=mesh,
        out_type=jax.ShapeDtypeStruct((ntok, d), jnp.float32),
        scratch_types=[
            pltpu.VMEM((SC_TB, nsel), jnp.int32),
            pltpu.VMEM((SC_TB, nsel * SC_LANES), jnp.float32),
            pltpu.VMEM((SC_ROWS, d), jnp.float32),
            pltpu.VMEM((SC_ROWS, d), jnp.float32),
            pltpu.VMEM((SC_TB, d), jnp.float32),
            pltpu.SemaphoreType.DMA,
            pltpu.SemaphoreType.DMA,
        ],
        name="sc_peer_v",
    )
    def k(vtab_hbm, idx_hbm, act_hbm, x1_hbm, out_hbm, idx_v, act_v, rows0, rows1, acc_v, sem0, sem1):
        wid = lax.axis_index("s") * info.num_cores + lax.axis_index("c")

        def gather(item, rows, sem):
            tl = item // nchunk
            ch = item % nchunk
            return pltpu.make_async_copy(vtab_hbm.at[idx_v.at[tl, pl.ds(ch * SC_ROWS, SC_ROWS)]], rows, sem)

        def accumulate(item, rows):
            tl = item // nchunk
            r0 = (item % nchunk) * SC_ROWS
            a = [act_v[tl, pl.ds((r0 + e) * SC_LANES, SC_LANES)] for e in range(SC_ROWS)]

            @plsc.parallel_loop(0, d // SC_LANES, 1, unroll=2)
            def _(c):
                col = pl.ds(c * SC_LANES, SC_LANES)
                terms = [a[e] * rows[e, col] for e in range(SC_ROWS)]
                while len(terms) > 1:
                    terms = [terms[j] + terms[j + 1] for j in range(0, len(terms), 2)]
                acc_v[tl, col] = acc_v[tl, col] + terms[0]

        @pl.loop(0, per_w // SC_TB)
        def _(bi):
            loc = pl.multiple_of(wid * per_w + bi * SC_TB, SC_TB)
            src = pl.ds(tok0 + loc, SC_TB)
            pltpu.sync_copy(idx_hbm.at[src], idx_v)
            pltpu.sync_copy(act_hbm.at[pl.ds(loc, SC_TB)], act_v)
            pltpu.sync_copy(x1_hbm.at[src], acc_v)
            gather(0, rows0, sem0).start()

            @pl.loop(0, nitem // 2)
            def _(p):
                gather(2 * p + 1, rows1, sem1).start()
                gather(2 * p, rows0, sem0).wait()
                accumulate(2 * p, rows0)

                @pl.when(p < nitem // 2 - 1)
                def _():
                    gather(2 * p + 2, rows0, sem0).start()

                gather(2 * p + 1, rows1, sem1).wait()
                accumulate(2 * p + 1, rows1)

            pltpu.sync_copy(acc_v, out_hbm.at[pl.ds(loc, SC_TB)])

    return k(vtab, idx, actrep, x1)


def _peer_experts(xn3, x1, experts, gates, u_packed, v_table):
    T, nsel = experts.shape
    n_exp = u_packed.shape[0]
    tc = T // PEER_CHUNKS
    steps = tc // PEER_TT
    assert nsel == LANES and PEER_TT % PEER_GROUP == 0 and T % (PEER_CHUNKS * PEER_TT) == 0
    tab_spec = pl.BlockSpec((n_exp, 4, LANES), lambda i: (0, 0, 0), pipeline_mode=pl.Buffered(1))
    cp = pltpu.CompilerParams(dimension_semantics=("arbitrary",), vmem_limit_bytes=VMEM_LIMIT)
    outs = []
    for c in range(PEER_CHUNKS):
        off = c * steps
        row = lambda i, off=off: (i + off, 0)
        act = pl.pallas_call(
            _peer_u_kernel,
            grid=(steps,),
            in_specs=[pl.BlockSpec((PEER_TT, nsel), row, memory_space=pltpu.SMEM), tab_spec,
                      pl.BlockSpec((PEER_TT, SUBLANES, LANES), lambda i, off=off: (i + off, 0, 0)),
                      pl.BlockSpec((PEER_TT, nsel), row)],
            out_specs=pl.BlockSpec((PEER_TT, nsel), lambda i: (i, 0)),
            out_shape=jax.ShapeDtypeStruct((tc, nsel), jnp.float32),
            scratch_shapes=[pltpu.VMEM((4 * nsel, LANES), jnp.float32), pltpu.VMEM((4 * nsel, LANES), jnp.float32),
                            pltpu.VMEM((PEER_TT * nsel, LANES), jnp.float32)],
            compiler_params=cp,
            name="peer_u",
        )(experts, u_packed, xn3, gates)
        actrep = jnp.broadcast_to(act[:, :, None], (tc, nsel, SC_LANES)).reshape(tc, nsel * SC_LANES)
        outs.append(_sc_peer_v(v_table, experts, actrep, x1, c * tc, tc))
    return jnp.concatenate(outs, axis=0)


def _alibi_slopes(n):
    return 2.0 ** (-8.0 * np.arange(1, n + 1) / n)


def _lambda_init(layer):
    return 0.8 - 0.6 * math.exp(-0.3 * layer)


def kernel(x, norm1_g, w_in, moba_qk_norm_g, diff_qk_norm_g, diff_lambda, diff_subln_g, w_out, norm2_g,
           peer_w_query, peer_sub_keys, peer_u, peer_v):
    b, s, d = x.shape
    T = b * s
    depth = w_in.shape[0]
    mw = MOBA_HEADS * HEAD_DIM
    assert d == 1024 and s % PROJ_TM == 0 and mw == 512 and DIFF_HEADS * 2 * HEAD_DIM == 512
    f32, bf16 = jnp.float32, jnp.bfloat16
    scale = HEAD_DIM ** -0.5
    grp = lax.broadcasted_iota(jnp.int32, (LANES, LANES), 0) // HEAD_DIM
    gsum = (grp == grp.T).astype(bf16)
    moba_slopes = jnp.asarray(_alibi_slopes(MOBA_HEADS).reshape(MOBA_HEADS // 2, 2), f32)
    diff_slopes = jnp.asarray(np.repeat(_alibi_slopes(DIFF_HEADS)[:, None], 2, axis=1), f32)
    assert b % PIPE_GROUPS == 0
    bg = b // PIPE_GROUPS
    tg = bg * s
    nblk = PROJ_TM // MOBA_BLOCK
    x2 = x.reshape(T, d)
    for l in range(depth):
        w = w_in[l]
        wqk = jnp.concatenate([w[:, 0:2 * mw], w[:, 3 * mw:3 * mw + 1024]], axis=1).astype(bf16)
        wvt = jnp.concatenate([w[:, 2 * mw:3 * mw], w[:, 3 * mw + 1024:]], axis=1).T.astype(bf16)
        mg, dg = moba_qk_norm_g[l].astype(f32), diff_qk_norm_g[l].astype(f32)
        gqk = jnp.concatenate([jnp.tile(mg[0] * scale, 8), jnp.tile(mg[1], 8),
                               jnp.tile(dg[0] * scale, 8), jnp.tile(dg[1], 8)])[None, :]
        g1 = norm1_g[l][None, :].astype(f32)
        g2 = norm2_g[l][None, :].astype(f32)
        gsub = jnp.broadcast_to(diff_subln_g[l].astype(f32)[:, None], (LANES, MOBA_BLOCK))
        lam_p = diff_lambda[l].astype(f32)
        wo, wq = w_out[l].astype(bf16), peer_w_query[l].astype(bf16)
        keys = peer_sub_keys[l].reshape(2 * PEER_HEADS, PEER_N_KEYS, LANES).astype(bf16)
        u_packed = _pack_table(peer_u[l]).reshape(peer_u.shape[1], 4, LANES)
        v_table = peer_v[l].astype(f32)
        outs = []
        for g in range(PIPE_GROUPS):
            xg = x2[g * tg:(g + 1) * tg]
            mq, mk, dq, dk, mvt, dvt, kmean = _project(xg, g1, wqk, wvt, gqk, gsum)
            kmean = kmean.reshape(tg // PROJ_TM, SUBLANES, 512)[:, :nblk].reshape(tg // MOBA_BLOCK, 512)
            moba_o = _moba_attention(mq, mk, mvt, kmean, moba_slopes, bg, s)
            diff_o = _diff_attention(dq, dk, dvt, lam_p, gsub, diff_slopes, _lambda_init(l), bg, s)
            x1, xn, experts, gates = _mix(xg, moba_o, diff_o, wo, g2, wq, keys)
            outs.append(_peer_experts(xn.reshape(tg, SUBLANES, LANES), x1, experts, gates, u_packed, v_table))
        x2 = jnp.concatenate(outs, axis=0)
    return x2.reshape(b, s, d)
```

```python
import functools
import math

import jax
import jax.numpy as jnp
import numpy as np
from jax import lax
from jax.experimental import pallas as pl
from jax.experimental.pallas import tpu as pltpu
from jax.experimental.pallas import tpu_sc as plsc

HEAD_DIM = 64
MOBA_HEADS = 8
MOBA_BLOCK = 256
MOBA_TOPK = 3
DIFF_HEADS = 4
PEER_HEADS = 8
PEER_N_KEYS = 128
PEER_TOPK = 16
RMS_EPS = 1e-6

LANES = 128
SUBLANES = 8
VMEM_LIMIT = 48 * 1024 * 1024

PROJ_TM = 512
MIX_TM = 256
PEER_TT = 64
PACK_ROWS = 512
ATT_QBLOCKS = 2
PEER_GROUP = 8
PIPE_GROUPS = 4
PEER_CHUNKS = 2
LAST_CHUNKS = 8
LAST_SC_CHUNKS = 6
SC_LANES = 16
SC_ROWS = 32
SC_TB = 8

NEG_INF = float("-inf")
_NT = (((1,), (1,)), ((), ()))


def _dot(a, b):
    return jnp.dot(a, b, preferred_element_type=jnp.float32)


def _dot_nt(a, b):
    return lax.dot_general(a, b, _NT, preferred_element_type=jnp.float32)


def _split_bf16(x):
    hi = x.astype(jnp.bfloat16)
    lo = (x - hi.astype(jnp.float32)).astype(jnp.bfloat16)
    return hi, lo


def _proj_kernel(x_ref, g1_ref, wqk_ref, wvt_ref, gqk_ref, gsum_ref,
                 mq_ref, mk_ref, dq_ref, dk_ref, mvt_ref, dvt_ref, kmean_ref):
    x = x_ref[...]
    ms = jnp.mean(x * x, axis=-1, keepdims=True)
    h = (x * lax.rsqrt(ms + RMS_EPS) * g1_ref[...]).astype(jnp.bfloat16)
    gsum = gsum_ref[...]
    outs = (mq_ref, mk_ref, dq_ref, dk_ref)
    for sec in range(4):
        for half in range(2):
            c0 = sec * 512 + half * 256
            p = _dot(h, wqk_ref[:, c0:c0 + 256])
            for s in range(2):
                ps = p[:, s * LANES:(s + 1) * LANES]
                sq_hi, sq_lo = _split_bf16(ps * ps)
                ssq = _dot(sq_hi, gsum) + _dot(sq_lo, gsum)
                g = gqk_ref[:, c0 + s * LANES:c0 + (s + 1) * LANES]
                y = ps * lax.rsqrt(ssq * (1.0 / HEAD_DIM) + RMS_EPS) * g
                l0 = half * 256 + s * LANES
                outs[sec][:, l0:l0 + LANES] = y.astype(jnp.bfloat16)
                if sec == 1:
                    for blk in range(PROJ_TM // MOBA_BLOCK):
                        kmean_ref[blk:blk + 1, l0:l0 + LANES] = jnp.mean(
                            y[blk * MOBA_BLOCK:(blk + 1) * MOBA_BLOCK], axis=0, keepdims=True)
    kmean_ref[PROJ_TM // MOBA_BLOCK:, :] = jnp.zeros((SUBLANES - PROJ_TM // MOBA_BLOCK, 512), jnp.float32)
    vt = _dot_nt(wvt_ref[...], h).astype(jnp.bfloat16)
    for blk in range(PROJ_TM // MOBA_BLOCK):
        mvt_ref[blk] = vt[0:512, blk * MOBA_BLOCK:(blk + 1) * MOBA_BLOCK]
        dvt_ref[blk] = vt[512:1024, blk * MOBA_BLOCK:(blk + 1) * MOBA_BLOCK]


def _project(x2, g1, wqk, wvt, gqk, gsum):
    T = x2.shape[0]
    nt = T // PROJ_TM
    nblk = PROJ_TM // MOBA_BLOCK
    row = lambda i: (i, 0)
    const = lambda i: (0, 0)
    qk_shape = jax.ShapeDtypeStruct((T, 512), jnp.bfloat16)
    vt_shape = jax.ShapeDtypeStruct((T // MOBA_BLOCK, 512, MOBA_BLOCK), jnp.bfloat16)
    return pl.pallas_call(
        _proj_kernel,
        grid=(nt,),
        in_specs=[
            pl.BlockSpec((PROJ_TM, 1024), row),
            pl.BlockSpec((1, 1024), const),
            pl.BlockSpec((1024, 2048), const),
            pl.BlockSpec((1024, 1024), const),
            pl.BlockSpec((1, 2048), const),
            pl.BlockSpec((LANES, LANES), const),
        ],
        out_specs=[
            pl.BlockSpec((PROJ_TM, 512), row),
            pl.BlockSpec((PROJ_TM, 512), row),
            pl.BlockSpec((PROJ_TM, 512), row),
            pl.BlockSpec((PROJ_TM, 512), row),
            pl.BlockSpec((nblk, 512, MOBA_BLOCK), lambda i: (i, 0, 0)),
            pl.BlockSpec((nblk, 512, MOBA_BLOCK), lambda i: (i, 0, 0)),
            pl.BlockSpec((SUBLANES, 512), row),
        ],
        out_shape=[qk_shape, qk_shape, qk_shape, qk_shape, vt_shape, vt_shape,
                   jax.ShapeDtypeStruct((nt * SUBLANES, 512), jnp.float32)],
        compiler_params=pltpu.CompilerParams(dimension_semantics=("arbitrary",), vmem_limit_bytes=VMEM_LIMIT),
        name="proj",
    )(x2, g1, wqk, wvt, gqk, gsum)


def _rel_qk():
    qc = lax.broadcasted_iota(jnp.int32, (MOBA_BLOCK, MOBA_BLOCK), 1)
    kr = lax.broadcasted_iota(jnp.int32, (MOBA_BLOCK, MOBA_BLOCK), 0)
    return qc - kr


def _attend(q_h, k_ref, vt_load, i, slope, rel, selbias):
    nb = rel.astype(jnp.float32) * (-slope)
    ts, rowbs, colmax = [], [], []
    for j in range(i + 1):
        t = _dot_nt(k_ref[j * MOBA_BLOCK:(j + 1) * MOBA_BLOCK, :], q_h) + nb
        if j == i:
            t = jnp.where(rel >= 0, t, NEG_INF)
            rowb = jnp.zeros((1, MOBA_BLOCK), jnp.float32)
        else:
            rowb = jnp.full((1, MOBA_BLOCK), (i - j) * MOBA_BLOCK, jnp.float32) * (-slope)
            if selbias is not None:
                rowb = rowb + selbias[j:j + 1, :]
        ts.append(t)
        rowbs.append(rowb)
        colmax.append(jnp.max(t, axis=0, keepdims=True) + rowb)
    m = functools.reduce(jnp.maximum, colmax)
    l, acc = None, None
    for j in range(i + 1):
        p = jnp.exp(ts[j] - (m - rowbs[j]))
        lj = jnp.sum(p, axis=0, keepdims=True)
        aj = _dot(vt_load(j), p.astype(jnp.bfloat16))
        l = lj if l is None else l + lj
        acc = aj if acc is None else acc + aj
    return acc / l


def _head_queries(q2):
    lane = lax.broadcasted_iota(jnp.int32, q2.shape, 1)
    return [jnp.where((lane >= hh * HEAD_DIM) & (lane < (hh + 1) * HEAD_DIM), q2, jnp.zeros_like(q2))
            for hh in range(2)]


def _block_selection(gate, i):
    blk = lax.broadcasted_iota(jnp.int32, gate.shape, 0)
    cnt = jnp.zeros(gate.shape, jnp.int32)
    for mblk in range(i):
        gm = gate[mblk:mblk + 1, :]
        beats = (gm > gate) | ((gm == gate) & (mblk < blk))
        cnt = cnt + jnp.where(beats, 1, 0)
    return jnp.where((blk < i) & (cnt < MOBA_TOPK), 0.0, NEG_INF)


def _moba_kernel(slopes_ref, q_ref, k_ref, vt_ref, kmean_ref, o_ref):
    pair = pl.program_id(1)
    step = pl.program_id(2)
    nq = vt_ref.shape[0]
    rel = _rel_qk()
    km_hi, km_lo = _split_bf16(kmean_ref[...])
    for st in range(nq // ATT_QBLOCKS):
        @pl.when(step == st)
        def _(st=st):
            for u in range(ATT_QBLOCKS):
                i = st * ATT_QBLOCKS + u
                rows = slice(u * MOBA_BLOCK, (u + 1) * MOBA_BLOCK)
                outs = []
                for hh, q_h in enumerate(_head_queries(q_ref[rows, :])):
                    gate = _dot_nt(km_hi, q_h) + _dot_nt(km_lo, q_h)
                    vt_load = lambda j, hh=hh: vt_ref[j, hh * HEAD_DIM:(hh + 1) * HEAD_DIM, :]
                    outs.append(_attend(q_h, k_ref, vt_load, i, slopes_ref[pair, hh], rel,
                                        _block_selection(gate, i)))
                o_t = jnp.concatenate(outs, axis=0)
                o_ref[rows, :] = o_t.T.astype(o_ref.dtype)


def _diff_kernel(slopes_ref, lam_ref, q_ref, k_ref, vt_ref, gsub_ref, o_ref, *, lam_init):
    head = pl.program_id(1)
    step = pl.program_id(2)
    nq = vt_ref.shape[0]
    rel = _rel_qk()
    slope = slopes_ref[head, 0]
    lp = lam_ref[...]
    lam = (jnp.exp(jnp.sum(lp[0:1] * lp[1:2], axis=-1, keepdims=True))
           - jnp.exp(jnp.sum(lp[2:3] * lp[3:4], axis=-1, keepdims=True)) + lam_init)
    vt_load = lambda j: vt_ref[j]
    for st in range(nq // ATT_QBLOCKS):
        @pl.when(step == st)
        def _(st=st):
            for u in range(ATT_QBLOCKS):
                i = st * ATT_QBLOCKS + u
                rows = slice(u * MOBA_BLOCK, (u + 1) * MOBA_BLOCK)
                parts = [_attend(q_h, k_ref, vt_load, i, slope, rel, None)
                         for q_h in _head_queries(q_ref[rows, :])]
                o_t = parts[0] - lam * parts[1]
                ms = jnp.mean(o_t * o_t, axis=0, keepdims=True)
                o_t = o_t * lax.rsqrt(ms + RMS_EPS) * gsub_ref[...] * (1.0 - lam_init)
                o_ref[rows, :] = o_t.T.astype(o_ref.dtype)


def _attention_specs(nq):
    nst = nq // ATT_QBLOCKS
    q_spec = pl.BlockSpec((ATT_QBLOCKS * MOBA_BLOCK, LANES), lambda b, p, i: (b * nst + i, p))
    k_spec = pl.BlockSpec((nq * MOBA_BLOCK, LANES), lambda b, p, i: (b, p))
    vt_spec = pl.BlockSpec((nq, LANES, MOBA_BLOCK), lambda b, p, i: (b, p, 0))
    return q_spec, k_spec, vt_spec


def _moba_attention(q, k, vt, kmean, slopes, batch, seq):
    nq = seq // MOBA_BLOCK
    q_spec, k_spec, vt_spec = _attention_specs(nq)
    return pl.pallas_call(
        _moba_kernel,
        grid=(batch, MOBA_HEADS // 2, nq // ATT_QBLOCKS),
        in_specs=[
            pl.BlockSpec(memory_space=pltpu.SMEM),
            q_spec, k_spec, vt_spec,
            pl.BlockSpec((SUBLANES, LANES), lambda b, p, i: (b, p)),
        ],
        out_specs=q_spec,
        out_shape=jax.ShapeDtypeStruct(q.shape, jnp.bfloat16),
        compiler_params=pltpu.CompilerParams(
            dimension_semantics=("arbitrary", "arbitrary", "arbitrary"), vmem_limit_bytes=VMEM_LIMIT),
        name="moba",
    )(slopes, q, k, vt, kmean)


def _diff_attention(q, k, vt, lam_p, gsub, slopes, lam_init, batch, seq):
    nq = seq // MOBA_BLOCK
    q_spec, k_spec, vt_spec = _attention_specs(nq)
    return pl.pallas_call(
        functools.partial(_diff_kernel, lam_init=lam_init),
        grid=(batch, DIFF_HEADS, nq // ATT_QBLOCKS),
        in_specs=[
            pl.BlockSpec(memory_space=pltpu.SMEM),
            pl.BlockSpec((4, HEAD_DIM), lambda b, p, i: (0, 0)),
            q_spec, k_spec, vt_spec,
            pl.BlockSpec((LANES, MOBA_BLOCK), lambda b, p, i: (0, 0)),
        ],
        out_specs=q_spec,
        out_shape=jax.ShapeDtypeStruct(q.shape, jnp.bfloat16),
        compiler_params=pltpu.CompilerParams(
            dimension_semantics=("arbitrary", "arbitrary", "arbitrary"), vmem_limit_bytes=VMEM_LIMIT),
        name="diff",
    )(slopes, lam_p, q, k, vt, gsub)


def _knockout_topk(sc, k, payload=None):
    n = sc.shape[0]
    row = lax.broadcasted_iota(jnp.int32, sc.shape, 0).astype(jnp.float32)
    vals, picks = [], []
    for _ in range(k):
        m = jnp.max(sc, axis=0, keepdims=True)
        idx = jnp.min(jnp.where(sc == m, row, float(n)), axis=0, keepdims=True)
        hit = row == idx
        vals.append(m)
        picks.append(idx if payload is None else jnp.sum(jnp.where(hit, payload, 0.0), axis=0, keepdims=True))
        sc = jnp.where(hit, NEG_INF, sc)
    return vals, picks


def _mix_kernel(x_ref, am_ref, ad_ref, wo_ref, g2_ref, wq_ref, keys_ref,
                x1_ref, xn_ref, exp_ref, gate_ref):
    x1 = x_ref[...] + _dot(am_ref[...], wo_ref[0:512, :]) + _dot(ad_ref[...], wo_ref[512:1024, :])
    x1_ref[...] = x1
    ms = jnp.mean(x1 * x1, axis=-1, keepdims=True)
    xn = x1 * lax.rsqrt(ms + RMS_EPS) * g2_ref[...]
    xn_ref[...] = xn
    qq = _dot(xn.astype(jnp.bfloat16), wq_ref[...]).astype(jnp.bfloat16)
    tm = qq.shape[0]
    exp_rows, gate_rows = [], []
    for h in range(PEER_HEADS):
        tops = []
        for half in range(2):
            s = h * 2 + half
            sc = _dot_nt(keys_ref[s], qq[:, s * LANES:(s + 1) * LANES])
            tops.append(_knockout_topk(sc, PEER_TOPK))
        (s1, i1), (s2, i2) = tops
        s2a = jnp.concatenate(s2, axis=0)
        i2a = jnp.concatenate(i2, axis=0)
        cand, cidx = [], []
        for a in range(PEER_TOPK):
            nb = PEER_TOPK // (a + 1)
            cand.append(s1[a] + s2a[0:nb])
            cidx.append(i1[a] * float(PEER_N_KEYS) + i2a[0:nb])
        n_cand = sum(c.shape[0] for c in cand)
        pad = -n_cand % SUBLANES
        cand.append(jnp.full((pad, tm), NEG_INF, jnp.float32))
        cidx.append(jnp.zeros((pad, tm), jnp.float32))
        best, experts = _knockout_topk(jnp.concatenate(cand, axis=0), PEER_TOPK,
                                       payload=jnp.concatenate(cidx, axis=0))
        b = jnp.concatenate(best, axis=0)
        e = jnp.exp(b - b[0:1])
        gate_rows.append(e / jnp.sum(e, axis=0, keepdims=True))
        exp_rows.append(jnp.concatenate(experts, axis=0))
    exp_ref[...] = jnp.concatenate(exp_rows, axis=0).T.astype(jnp.int32)
    gate_ref[...] = jnp.concatenate(gate_rows, axis=0).T


def _mix(x2, am, ad, wo, g2, wq, keys):
    T = x2.shape[0]
    row = lambda i: (i, 0)
    const = lambda i: (0, 0)
    return pl.pallas_call(
        _mix_kernel,
        grid=(T // MIX_TM,),
        in_specs=[
            pl.BlockSpec((MIX_TM, 1024), row),
            pl.BlockSpec((MIX_TM, 512), row),
            pl.BlockSpec((MIX_TM, 512), row),
            pl.BlockSpec((1024, 1024), const),
            pl.BlockSpec((1, 1024), const),
            pl.BlockSpec((1024, 2048), const),
            pl.BlockSpec((2 * PEER_HEADS, PEER_N_KEYS, LANES), lambda i: (0, 0, 0)),
        ],
        out_specs=[
            pl.BlockSpec((MIX_TM, 1024), row),
            pl.BlockSpec((MIX_TM, 1024), row),
            pl.BlockSpec((MIX_TM, LANES), row),
            pl.BlockSpec((MIX_TM, LANES), row),
        ],
        out_shape=[
            jax.ShapeDtypeStruct((T, 1024), jnp.float32),
            jax.ShapeDtypeStruct((T, 1024), jnp.float32),
            jax.ShapeDtypeStruct((T, LANES), jnp.int32),
            jax.ShapeDtypeStruct((T, LANES), jnp.float32),
        ],
        compiler_params=pltpu.CompilerParams(dimension_semantics=("arbitrary",), vmem_limit_bytes=VMEM_LIMIT),
        name="mix",
    )(x2, am, ad, wo, g2, wq, keys)


def _pack_kernel(t_ref, o_ref):
    t = t_ref[...]
    lo = pltpu.bitcast(t[:, 0:512].astype(jnp.bfloat16).astype(jnp.float32), jnp.int32)
    hi = pltpu.bitcast(t[:, 512:1024].astype(jnp.bfloat16).astype(jnp.float32), jnp.int32)
    o_ref[...] = lax.shift_right_logical(lo, 16) | (hi & jnp.int32(-65536))


def _pack_table(tab):
    n = tab.shape[0]
    return pl.pallas_call(
        _pack_kernel,
        grid=(n // PACK_ROWS,),
        in_specs=[pl.BlockSpec((PACK_ROWS, 1024), lambda i: (i, 0))],
        out_specs=pl.BlockSpec((PACK_ROWS, 512), lambda i: (i, 0)),
        out_shape=jax.ShapeDtypeStruct((n, 512), jnp.int32),
        compiler_params=pltpu.CompilerParams(dimension_semantics=("arbitrary",)),
        name="pack",
    )(tab)


def _decode(w):
    lo = pltpu.bitcast(w << 16, jnp.float32)
    hi = pltpu.bitcast(w & jnp.int32(-65536), jnp.float32)
    return lo, hi


def _peer_u_kernel(exp_ref, tab_ref, x_ref, g_ref, act_ref, q0_ref, q1_ref, qs_ref):
    ntok, nsel = exp_ref.shape
    ones = jnp.ones((SUBLANES, LANES), jnp.bfloat16)

    def products(t, q_ref):
        x = x_ref[t]
        xlo = x[0:4]
        xhi = x[4:8]
        for e in range(nsel):
            lo, hi = _decode(tab_ref[exp_ref[t, e]])
            q_ref[pl.ds(4 * e, 4), :] = lo * xlo + hi * xhi

    def fold(t, q_ref):
        q = q_ref[pl.ds(0, nsel, stride=4), :]
        for s in range(1, 4):
            q = q + q_ref[pl.ds(s, nsel, stride=4), :]
        qs_ref[pl.ds(pl.multiple_of(t * nsel, nsel), nsel), :] = q

    def gather_group(g):
        for k in range(PEER_GROUP):
            q_ref = (q0_ref, q1_ref)[k % 2]
            products(g * PEER_GROUP + k, q_ref)
            fold(g * PEER_GROUP + k, q_ref)

    def finish_group(g):
        rows8 = PEER_GROUP * nsel
        q_hi, q_lo = _split_bf16(qs_ref[pl.ds(pl.multiple_of(g * rows8, rows8), rows8), :])
        row = (_dot_nt(ones, q_hi) + _dot_nt(ones, q_lo))[0:1]
        h = jnp.concatenate([row[:, k * nsel:(k + 1) * nsel] for k in range(PEER_GROUP)], axis=0)
        gelu = 0.5 * h * (1.0 + lax.erf(h * (1.0 / math.sqrt(2.0))))
        rows = pl.ds(pl.multiple_of(g * PEER_GROUP, PEER_GROUP), PEER_GROUP)
        act_ref[rows, :] = gelu * g_ref[rows, :]

    ngroups = ntok // PEER_GROUP
    gather_group(0)

    def body(g, carry):
        finish_group(g - 1)
        gather_group(g)
        return carry

    lax.fori_loop(1, ngroups, body, 0)
    finish_group(ngroups - 1)


def _sc_peer_v(vtab, idx, actrep, x1, tok0, ntok):
    d = x1.shape[1]
    nsel = idx.shape[1]
    info = plsc.get_sparse_core_info()
    nw = info.num_cores * info.num_subcores
    per_w = ntok // nw
    nchunk = nsel // SC_ROWS
    nitem = SC_TB * nchunk
    assert info.num_lanes == SC_LANES and ntok % (nw * SC_TB) == 0 and nitem % 2 == 0
    mesh = plsc.VectorSubcoreMesh(core_axis_name="c", subcore_axis_name="s")

    @functools.partial(
        pl.kernel, mesh=mesh,
        out_type=jax.ShapeDtypeStruct((ntok, d), jnp.float32),
        scratch_types=[
            pltpu.VMEM((SC_TB, nsel), jnp.int32),
            pltpu.VMEM((SC_TB, nsel * SC_LANES), jnp.float32),
            pltpu.VMEM((SC_ROWS, d), jnp.float32),
            pltpu.VMEM((SC_ROWS, d), jnp.float32),
            pltpu.VMEM((SC_TB, d), jnp.float32),
            pltpu.SemaphoreType.DMA,
            pltpu.SemaphoreType.DMA,
        ],
        name="sc_peer_v",
    )
    def k(vtab_hbm, idx_hbm, act_hbm, x1_hbm, out_hbm, idx_v, act_v, rows0, rows1, acc_v, sem0, sem1):
        wid = lax.axis_index("s") * info.num_cores + lax.axis_index("c")

        def gather(item, rows, sem):
            tl = item // nchunk
            ch = item % nchunk
            return pltpu.make_async_copy(vtab_hbm.at[idx_v.at[tl, pl.ds(ch * SC_ROWS, SC_ROWS)]], rows, sem)

        def accumulate(item, rows):
            tl = item // nchunk
            r0 = (item % nchunk) * SC_ROWS
            a = [act_v[tl, pl.ds((r0 + e) * SC_LANES, SC_LANES)] for e in range(SC_ROWS)]

            @plsc.parallel_loop(0, d // SC_LANES, 1, unroll=2)
            def _(c):
                col = pl.ds(c * SC_LANES, SC_LANES)
                terms = [a[e] * rows[e, col] for e in range(SC_ROWS)]
                while len(terms) > 1:
                    terms = [terms[j] + terms[j + 1] for j in range(0, len(terms), 2)]
                acc_v[tl, col] = acc_v[tl, col] + terms[0]

        @pl.loop(0, per_w // SC_TB)
        def _(bi):
            loc = pl.multiple_of(wid * per_w + bi * SC_TB, SC_TB)
            src = pl.ds(tok0 + loc, SC_TB)
            pltpu.sync_copy(idx_hbm.at[src], idx_v)
            pltpu.sync_copy(act_hbm.at[pl.ds(loc, SC_TB)], act_v)
            pltpu.sync_copy(x1_hbm.at[src], acc_v)
            gather(0, rows0, sem0).start()

            @pl.loop(0, nitem // 2)
            def _(p):
                gather(2 * p + 1, rows1, sem1).start()
                gather(2 * p, rows0, sem0).wait()
                accumulate(2 * p, rows0)

                @pl.when(p < nitem // 2 - 1)
                def _():
                    gather(2 * p + 2, rows0, sem0).start()

                gather(2 * p + 1, rows1, sem1).wait()
                accumulate(2 * p + 1, rows1)

            pltpu.sync_copy(acc_v, out_hbm.at[pl.ds(loc, SC_TB)])

    return k(vtab, idx, actrep, x1)


def _peer_v_kernel(exp_ref, act_ref, tab_ref, x_ref, o_ref):
    nsel = exp_ref.shape[1]

    def tok(t, carry):
        acc_lo = jnp.zeros((4, LANES), jnp.float32)
        acc_hi = jnp.zeros((4, LANES), jnp.float32)
        for e in range(nsel):
            a = act_ref[t, e]
            lo, hi = _decode(tab_ref[exp_ref[t, e]])
            acc_lo = acc_lo + a * lo
            acc_hi = acc_hi + a * hi
        o_ref[t] = x_ref[t] + jnp.concatenate([acc_lo, acc_hi], axis=0)
        return carry

    lax.fori_loop(0, x_ref.shape[0], tok, 0)


def _peer_experts(xn3, x1, experts, gates, u_packed, v_table, v_packed, n_chunks, n_sc):
    T, nsel = experts.shape
    n_exp = u_packed.shape[0]
    d = x1.shape[1]
    tc = T // n_chunks
    steps = tc // PEER_TT
    assert nsel == LANES and PEER_TT % PEER_GROUP == 0 and T % (n_chunks * PEER_TT) == 0
    tab_spec = pl.BlockSpec((n_exp, 4, LANES), lambda i: (0, 0, 0), pipeline_mode=pl.Buffered(1))
    cp = pltpu.CompilerParams(dimension_semantics=("arbitrary",), vmem_limit_bytes=VMEM_LIMIT)
    x13 = x1.reshape(T, SUBLANES, LANES) if n_sc < n_chunks else None
    outs = []
    for c in range(n_chunks):
        off = c * steps
        row = lambda i, off=off: (i + off, 0)
        row3 = lambda i, off=off: (i + off, 0, 0)
        act = pl.pallas_call(
            _peer_u_kernel,
            grid=(steps,),
            in_specs=[pl.BlockSpec((PEER_TT, nsel), row, memory_space=pltpu.SMEM), tab_spec,
                      pl.BlockSpec((PEER_TT, SUBLANES, LANES), row3), pl.BlockSpec((PEER_TT, nsel), row)],
            out_specs=pl.BlockSpec((PEER_TT, nsel), lambda i: (i, 0)),
            out_shape=jax.ShapeDtypeStruct((tc, nsel), jnp.float32),
            scratch_shapes=[pltpu.VMEM((4 * nsel, LANES), jnp.float32), pltpu.VMEM((4 * nsel, LANES), jnp.float32),
                            pltpu.VMEM((PEER_TT * nsel, LANES), jnp.float32)],
            compiler_params=cp,
            name="peer_u",
        )(experts, u_packed, xn3, gates)
        if c < n_sc:
            actrep = jnp.broadcast_to(act[:, :, None], (tc, nsel, SC_LANES)).reshape(tc, nsel * SC_LANES)
            outs.append(_sc_peer_v(v_table, experts, actrep, x1, c * tc, tc))
        else:
            out3 = pl.pallas_call(
                _peer_v_kernel,
                grid=(steps,),
                in_specs=[pl.BlockSpec((PEER_TT, nsel), row, memory_space=pltpu.SMEM),
                          pl.BlockSpec((PEER_TT, nsel), lambda i: (i, 0), memory_space=pltpu.SMEM),
                          tab_spec, pl.BlockSpec((PEER_TT, SUBLANES, LANES), row3)],
                out_specs=pl.BlockSpec((PEER_TT, SUBLANES, LANES), lambda i: (i, 0, 0)),
                out_shape=jax.ShapeDtypeStruct((tc, SUBLANES, LANES), jnp.float32),
                compiler_params=cp,
                name="peer_v",
            )(experts, act, v_packed, x13)
            outs.append(out3.reshape(tc, d))
    return jnp.concatenate(outs, axis=0)


def _alibi_slopes(n):
    return 2.0 ** (-8.0 * np.arange(1, n + 1) / n)


def _lambda_init(layer):
    return 0.8 - 0.6 * math.exp(-0.3 * layer)


def kernel(x, norm1_g, w_in, moba_qk_norm_g, diff_qk_norm_g, diff_lambda, diff_subln_g, w_out, norm2_g,
           peer_w_query, peer_sub_keys, peer_u, peer_v):
    b, s, d = x.shape
    T = b * s
    depth = w_in.shape[0]
    mw = MOBA_HEADS * HEAD_DIM
    assert d == 1024 and s % PROJ_TM == 0 and mw == 512 and DIFF_HEADS * 2 * HEAD_DIM == 512
    f32, bf16 = jnp.float32, jnp.bfloat16
    scale = HEAD_DIM ** -0.5
    grp = lax.broadcasted_iota(jnp.int32, (LANES, LANES), 0) // HEAD_DIM
    gsum = (grp == grp.T).astype(bf16)
    moba_slopes = jnp.asarray(_alibi_slopes(MOBA_HEADS).reshape(MOBA_HEADS // 2, 2), f32)
    diff_slopes = jnp.asarray(np.repeat(_alibi_slopes(DIFF_HEADS)[:, None], 2, axis=1), f32)
    assert b % PIPE_GROUPS == 0
    bg = b // PIPE_GROUPS
    tg = bg * s
    nblk = PROJ_TM // MOBA_BLOCK
    x2 = x.reshape(T, d)
    for l in range(depth):
        w = w_in[l]
        wqk = jnp.concatenate([w[:, 0:2 * mw], w[:, 3 * mw:3 * mw + 1024]], axis=1).astype(bf16)
        wvt = jnp.concatenate([w[:, 2 * mw:3 * mw], w[:, 3 * mw + 1024:]], axis=1).T.astype(bf16)
        mg, dg = moba_qk_norm_g[l].astype(f32), diff_qk_norm_g[l].astype(f32)
        gqk = jnp.concatenate([jnp.tile(mg[0] * scale, 8), jnp.tile(mg[1], 8),
                               jnp.tile(dg[0] * scale, 8), jnp.tile(dg[1], 8)])[None, :]
        g1 = norm1_g[l][None, :].astype(f32)
        g2 = norm2_g[l][None, :].astype(f32)
        gsub = jnp.broadcast_to(diff_subln_g[l].astype(f32)[:, None], (LANES, MOBA_BLOCK))
        lam_p = diff_lambda[l].astype(f32)
        wo, wq = w_out[l].astype(bf16), peer_w_query[l].astype(bf16)
        keys = peer_sub_keys[l].reshape(2 * PEER_HEADS, PEER_N_KEYS, LANES).astype(bf16)
        u_packed = _pack_table(peer_u[l]).reshape(peer_u.shape[1], 4, LANES)
        v_packed = _pack_table(peer_v[l]).reshape(peer_v.shape[1], 4, LANES)
        v_table = peer_v[l].astype(f32)
        outs = []
        for g in range(PIPE_GROUPS):
            xg = x2[g * tg:(g + 1) * tg]
            mq, mk, dq, dk, mvt, dvt, kmean = _project(xg, g1, wqk, wvt, gqk, gsum)
            kmean = kmean.reshape(tg // PROJ_TM, SUBLANES, 512)[:, :nblk].reshape(tg // MOBA_BLOCK, 512)
            moba_o = _moba_attention(mq, mk, mvt, kmean, moba_slopes, bg, s)
            diff_o = _diff_attention(dq, dk, dvt, lam_p, gsub, diff_slopes, _lambda_init(l), bg, s)
            x1, xn, experts, gates = _mix(xg, moba_o, diff_o, wo, g2, wq, keys)
            last = g == PIPE_GROUPS - 1
            n_chunks, n_sc = (LAST_CHUNKS, LAST_SC_CHUNKS) if last else (PEER_CHUNKS, PEER_CHUNKS)
            outs.append(_peer_experts(xn.reshape(tg, SUBLANES, LANES), x1, experts, gates, u_packed, v_table,
                                      v_packed, n_chunks, n_sc))
        x2 = jnp.concatenate(outs, axis=0)
    return x2.reshape(b, s, d)
```

```python
import functools
import math

import jax
import jax.numpy as jnp
import numpy as np
from jax import lax
from jax.experimental import pallas as pl
from jax.experimental.pallas import tpu as pltpu
from jax.experimental.pallas import tpu_sc as plsc

HEAD_DIM = 64
MOBA_HEADS = 8
MOBA_BLOCK = 256
MOBA_TOPK = 3
DIFF_HEADS = 4
PEER_HEADS = 8
PEER_N_KEYS = 128
PEER_TOPK = 16
RMS_EPS = 1e-6

LANES = 128
SUBLANES = 8
VMEM_LIMIT = 48 * 1024 * 1024

PROJ_TM = 512
MIX_TM = 256
PEER_TT = 64
PACK_ROWS = 512
ATT_QBLOCKS = 2
PEER_GROUP = 8
PIPE_GROUPS = 4
PEER_CHUNKS = 2
LAST_CHUNKS = 4
LAST_SC_CHUNKS = 3
SC_LANES = 16
SC_ROWS = 32
SC_TB = 8

NEG_INF = float("-inf")
_NT = (((1,), (1,)), ((), ()))


def _dot(a, b):
    return jnp.dot(a, b, preferred_element_type=jnp.float32)


def _dot_nt(a, b):
    return lax.dot_general(a, b, _NT, preferred_element_type=jnp.float32)


def _split_bf16(x):
    hi = x.astype(jnp.bfloat16)
    lo = (x - hi.astype(jnp.float32)).astype(jnp.bfloat16)
    return hi, lo


def _proj_kernel(x_ref, g1_ref, wqk_ref, wvt_ref, gqk_ref, gsum_ref,
                 mq_ref, mk_ref, dq_ref, dk_ref, mvt_ref, dvt_ref, kmean_ref):
    x = x_ref[...]
    ms = jnp.mean(x * x, axis=-1, keepdims=True)
    h = (x * lax.rsqrt(ms + RMS_EPS) * g1_ref[...]).astype(jnp.bfloat16)
    gsum = gsum_ref[...]
    outs = (mq_ref, mk_ref, dq_ref, dk_ref)
    for sec in range(4):
        for half in range(2):
            c0 = sec * 512 + half * 256
            p = _dot(h, wqk_ref[:, c0:c0 + 256])
            for s in range(2):
                ps = p[:, s * LANES:(s + 1) * LANES]
                sq_hi, sq_lo = _split_bf16(ps * ps)
                ssq = _dot(sq_hi, gsum) + _dot(sq_lo, gsum)
                g = gqk_ref[:, c0 + s * LANES:c0 + (s + 1) * LANES]
                y = ps * lax.rsqrt(ssq * (1.0 / HEAD_DIM) + RMS_EPS) * g
                l0 = half * 256 + s * LANES
                outs[sec][:, l0:l0 + LANES] = y.astype(jnp.bfloat16)
                if sec == 1:
                    for blk in range(PROJ_TM // MOBA_BLOCK):
                        kmean_ref[blk:blk + 1, l0:l0 + LANES] = jnp.mean(
                            y[blk * MOBA_BLOCK:(blk + 1) * MOBA_BLOCK], axis=0, keepdims=True)
    kmean_ref[PROJ_TM // MOBA_BLOCK:, :] = jnp.zeros((SUBLANES - PROJ_TM // MOBA_BLOCK, 512), jnp.float32)
    vt = _dot_nt(wvt_ref[...], h).astype(jnp.bfloat16)
    for blk in range(PROJ_TM // MOBA_BLOCK):
        mvt_ref[blk] = vt[0:512, blk * MOBA_BLOCK:(blk + 1) * MOBA_BLOCK]
        dvt_ref[blk] = vt[512:1024, blk * MOBA_BLOCK:(blk + 1) * MOBA_BLOCK]


def _project(x2, g1, wqk, wvt, gqk, gsum):
    T = x2.shape[0]
    nt = T // PROJ_TM
    nblk = PROJ_TM // MOBA_BLOCK
    row = lambda i: (i, 0)
    const = lambda i: (0, 0)
    qk_shape = jax.ShapeDtypeStruct((T, 512), jnp.bfloat16)
    vt_shape = jax.ShapeDtypeStruct((T // MOBA_BLOCK, 512, MOBA_BLOCK), jnp.bfloat16)
    return pl.pallas_call(
        _proj_kernel,
        grid=(nt,),
        in_specs=[
            pl.BlockSpec((PROJ_TM, 1024), row),
            pl.BlockSpec((1, 1024), const),
            pl.BlockSpec((1024, 2048), const),
            pl.BlockSpec((1024, 1024), const),
            pl.BlockSpec((1, 2048), const),
            pl.BlockSpec((LANES, LANES), const),
        ],
        out_specs=[
            pl.BlockSpec((PROJ_TM, 512), row),
            pl.BlockSpec((PROJ_TM, 512), row),
            pl.BlockSpec((PROJ_TM, 512), row),
            pl.BlockSpec((PROJ_TM, 512), row),
            pl.BlockSpec((nblk, 512, MOBA_BLOCK), lambda i: (i, 0, 0)),
            pl.BlockSpec((nblk, 512, MOBA_BLOCK), lambda i: (i, 0, 0)),
            pl.BlockSpec((SUBLANES, 512), row),
        ],
        out_shape=[qk_shape, qk_shape, qk_shape, qk_shape, vt_shape, vt_shape,
                   jax.ShapeDtypeStruct((nt * SUBLANES, 512), jnp.float32)],
        compiler_params=pltpu.CompilerParams(dimension_semantics=("arbitrary",), vmem_limit_bytes=VMEM_LIMIT),
        name="proj",
    )(x2, g1, wqk, wvt, gqk, gsum)


def _rel_qk():
    qc = lax.broadcasted_iota(jnp.int32, (MOBA_BLOCK, MOBA_BLOCK), 1)
    kr = lax.broadcasted_iota(jnp.int32, (MOBA_BLOCK, MOBA_BLOCK), 0)
    return qc - kr


def _attend(q_h, k_ref, vt_load, i, slope, rel, selbias):
    nb = rel.astype(jnp.float32) * (-slope)
    ts, rowbs, colmax = [], [], []
    for j in range(i + 1):
        t = _dot_nt(k_ref[j * MOBA_BLOCK:(j + 1) * MOBA_BLOCK, :], q_h) + nb
        if j == i:
            t = jnp.where(rel >= 0, t, NEG_INF)
            rowb = jnp.zeros((1, MOBA_BLOCK), jnp.float32)
        else:
            rowb = jnp.full((1, MOBA_BLOCK), (i - j) * MOBA_BLOCK, jnp.float32) * (-slope)
            if selbias is not None:
                rowb = rowb + selbias[j:j + 1, :]
        ts.append(t)
        rowbs.append(rowb)
        colmax.append(jnp.max(t, axis=0, keepdims=True) + rowb)
    m = functools.reduce(jnp.maximum, colmax)
    l, acc = None, None
    for j in range(i + 1):
        p = jnp.exp(ts[j] - (m - rowbs[j]))
        lj = jnp.sum(p, axis=0, keepdims=True)
        aj = _dot(vt_load(j), p.astype(jnp.bfloat16))
        l = lj if l is None else l + lj
        acc = aj if acc is None else acc + aj
    return acc / l


def _head_queries(q2):
    lane = lax.broadcasted_iota(jnp.int32, q2.shape, 1)
    return [jnp.where((lane >= hh * HEAD_DIM) & (lane < (hh + 1) * HEAD_DIM), q2, jnp.zeros_like(q2))
            for hh in range(2)]


def _block_selection(gate, i):
    blk = lax.broadcasted_iota(jnp.int32, gate.shape, 0)
    cnt = jnp.zeros(gate.shape, jnp.int32)
    for mblk in range(i):
        gm = gate[mblk:mblk + 1, :]
        beats = (gm > gate) | ((gm == gate) & (mblk < blk))
        cnt = cnt + jnp.where(beats, 1, 0)
    return jnp.where((blk < i) & (cnt < MOBA_TOPK), 0.0, NEG_INF)


def _moba_kernel(slopes_ref, q_ref, k_ref, vt_ref, kmean_ref, o_ref):
    pair = pl.program_id(1)
    step = pl.program_id(2)
    nq = vt_ref.shape[0]
    rel = _rel_qk()
    km_hi, km_lo = _split_bf16(kmean_ref[...])
    for st in range(nq // ATT_QBLOCKS):
        @pl.when(step == st)
        def _(st=st):
            for u in range(ATT_QBLOCKS):
                i = st * ATT_QBLOCKS + u
                rows = slice(u * MOBA_BLOCK, (u + 1) * MOBA_BLOCK)
                outs = []
                for hh, q_h in enumerate(_head_queries(q_ref[rows, :])):
                    gate = _dot_nt(km_hi, q_h) + _dot_nt(km_lo, q_h)
                    vt_load = lambda j, hh=hh: vt_ref[j, hh * HEAD_DIM:(hh + 1) * HEAD_DIM, :]
                    outs.append(_attend(q_h, k_ref, vt_load, i, slopes_ref[pair, hh], rel,
                                        _block_selection(gate, i)))
                o_t = jnp.concatenate(outs, axis=0)
                o_ref[rows, :] = o_t.T.astype(o_ref.dtype)


def _diff_kernel(slopes_ref, lam_ref, q_ref, k_ref, vt_ref, gsub_ref, o_ref, *, lam_init):
    head = pl.program_id(1)
    step = pl.program_id(2)
    nq = vt_ref.shape[0]
    rel = _rel_qk()
    slope = slopes_ref[head, 0]
    lp = lam_ref[...]
    lam = (jnp.exp(jnp.sum(lp[0:1] * lp[1:2], axis=-1, keepdims=True))
           - jnp.exp(jnp.sum(lp[2:3] * lp[3:4], axis=-1, keepdims=True)) + lam_init)
    vt_load = lambda j: vt_ref[j]
    for st in range(nq // ATT_QBLOCKS):
        @pl.when(step == st)
        def _(st=st):
            for u in range(ATT_QBLOCKS):
                i = st * ATT_QBLOCKS + u
                rows = slice(u * MOBA_BLOCK, (u + 1) * MOBA_BLOCK)
                parts = [_attend(q_h, k_ref, vt_load, i, slope, rel, None)
                         for q_h in _head_queries(q_ref[rows, :])]
                o_t = parts[0] - lam * parts[1]
                ms = jnp.mean(o_t * o_t, axis=0, keepdims=True)
                o_t = o_t * lax.rsqrt(ms + RMS_EPS) * gsub_ref[...] * (1.0 - lam_init)
                o_ref[rows, :] = o_t.T.astype(o_ref.dtype)


def _attention_specs(nq):
    nst = nq // ATT_QBLOCKS
    q_spec = pl.BlockSpec((ATT_QBLOCKS * MOBA_BLOCK, LANES), lambda b, p, i: (b * nst + i, p))
    k_spec = pl.BlockSpec((nq * MOBA_BLOCK, LANES), lambda b, p, i: (b, p))
    vt_spec = pl.BlockSpec((nq, LANES, MOBA_BLOCK), lambda b, p, i: (b, p, 0))
    return q_spec, k_spec, vt_spec


def _moba_attention(q, k, vt, kmean, slopes, batch, seq):
    nq = seq // MOBA_BLOCK
    q_spec, k_spec, vt_spec = _attention_specs(nq)
    return pl.pallas_call(
        _moba_kernel,
        grid=(batch, MOBA_HEADS // 2, nq // ATT_QBLOCKS),
        in_specs=[
            pl.BlockSpec(memory_space=pltpu.SMEM),
            q_spec, k_spec, vt_spec,
            pl.BlockSpec((SUBLANES, LANES), lambda b, p, i: (b, p)),
        ],
        out_specs=q_spec,
        out_shape=jax.ShapeDtypeStruct(q.shape, jnp.bfloat16),
        compiler_params=pltpu.CompilerParams(
            dimension_semantics=("arbitrary", "arbitrary", "arbitrary"), vmem_limit_bytes=VMEM_LIMIT),
        name="moba",
    )(slopes, q, k, vt, kmean)


def _diff_attention(q, k, vt, lam_p, gsub, slopes, lam_init, batch, seq):
    nq = seq // MOBA_BLOCK
    q_spec, k_spec, vt_spec = _attention_specs(nq)
    return pl.pallas_call(
        functools.partial(_diff_kernel, lam_init=lam_init),
        grid=(batch, DIFF_HEADS, nq // ATT_QBLOCKS),
        in_specs=[
            pl.BlockSpec(memory_space=pltpu.SMEM),
            pl.BlockSpec((4, HEAD_DIM), lambda b, p, i: (0, 0)),
            q_spec, k_spec, vt_spec,
            pl.BlockSpec((LANES, MOBA_BLOCK), lambda b, p, i: (0, 0)),
        ],
        out_specs=q_spec,
        out_shape=jax.ShapeDtypeStruct(q.shape, jnp.bfloat16),
        compiler_params=pltpu.CompilerParams(
            dimension_semantics=("arbitrary", "arbitrary", "arbitrary"), vmem_limit_bytes=VMEM_LIMIT),
        name="diff",
    )(slopes, lam_p, q, k, vt, gsub)


def _knockout_topk(sc, k, payload=None):
    n = sc.shape[0]
    row = lax.broadcasted_iota(jnp.int32, sc.shape, 0).astype(jnp.float32)
    vals, picks = [], []
    for _ in range(k):
        m = jnp.max(sc, axis=0, keepdims=True)
        idx = jnp.min(jnp.where(sc == m, row, float(n)), axis=0, keepdims=True)
        hit = row == idx
        vals.append(m)
        picks.append(idx if payload is None else jnp.sum(jnp.where(hit, payload, 0.0), axis=0, keepdims=True))
        sc = jnp.where(hit, NEG_INF, sc)
    return vals, picks


def _mix_kernel(x_ref, am_ref, ad_ref, wo_ref, g2_ref, wq_ref, keys_ref,
                x1_ref, xn_ref, exp_ref, gate_ref):
    x1 = x_ref[...] + _dot(am_ref[...], wo_ref[0:512, :]) + _dot(ad_ref[...], wo_ref[512:1024, :])
    x1_ref[...] = x1
    ms = jnp.mean(x1 * x1, axis=-1, keepdims=True)
    xn = x1 * lax.rsqrt(ms + RMS_EPS) * g2_ref[...]
    xn_ref[...] = xn
    qq = _dot(xn.astype(jnp.bfloat16), wq_ref[...]).astype(jnp.bfloat16)
    tm = qq.shape[0]
    exp_rows, gate_rows = [], []
    for h in range(PEER_HEADS):
        tops = []
        for half in range(2):
            s = h * 2 + half
            sc = _dot_nt(keys_ref[s], qq[:, s * LANES:(s + 1) * LANES])
            tops.append(_knockout_topk(sc, PEER_TOPK))
        (s1, i1), (s2, i2) = tops
        s2a = jnp.concatenate(s2, axis=0)
        i2a = jnp.concatenate(i2, axis=0)
        cand, cidx = [], []
        for a in range(PEER_TOPK):
            nb = PEER_TOPK // (a + 1)
            cand.append(s1[a] + s2a[0:nb])
            cidx.append(i1[a] * float(PEER_N_KEYS) + i2a[0:nb])
        n_cand = sum(c.shape[0] for c in cand)
        pad = -n_cand % SUBLANES
        cand.append(jnp.full((pad, tm), NEG_INF, jnp.float32))
        cidx.append(jnp.zeros((pad, tm), jnp.float32))
        best, experts = _knockout_topk(jnp.concatenate(cand, axis=0), PEER_TOPK,
                                       payload=jnp.concatenate(cidx, axis=0))
        b = jnp.concatenate(best, axis=0)
        e = jnp.exp(b - b[0:1])
        gate_rows.append(e / jnp.sum(e, axis=0, keepdims=True))
        exp_rows.append(jnp.concatenate(experts, axis=0))
    exp_ref[...] = jnp.concatenate(exp_rows, axis=0).T.astype(jnp.int32)
    gate_ref[...] = jnp.concatenate(gate_rows, axis=0).T


def _mix(x2, am, ad, wo, g2, wq, keys):
    T = x2.shape[0]
    row = lambda i: (i, 0)
    const = lambda i: (0, 0)
    return pl.pallas_call(
        _mix_kernel,
        grid=(T // MIX_TM,),
        in_specs=[
            pl.BlockSpec((MIX_TM, 1024), row),
            pl.BlockSpec((MIX_TM, 512), row),
            pl.BlockSpec((MIX_TM, 512), row),
            pl.BlockSpec((1024, 1024), const),
            pl.BlockSpec((1, 1024), const),
            pl.BlockSpec((1024, 2048), const),
            pl.BlockSpec((2 * PEER_HEADS, PEER_N_KEYS, LANES), lambda i: (0, 0, 0)),
        ],
        out_specs=[
            pl.BlockSpec((MIX_TM, 1024), row),
            pl.BlockSpec((MIX_TM, 1024), row),
            pl.BlockSpec((MIX_TM, LANES), row),
            pl.BlockSpec((MIX_TM, LANES), row),
        ],
        out_shape=[
            jax.ShapeDtypeStruct((T, 1024), jnp.float32),
            jax.ShapeDtypeStruct((T, 1024), jnp.float32),
            jax.ShapeDtypeStruct((T, LANES), jnp.int32),
            jax.ShapeDtypeStruct((T, LANES), jnp.float32),
        ],
        compiler_params=pltpu.CompilerParams(dimension_semantics=("arbitrary",), vmem_limit_bytes=VMEM_LIMIT),
        name="mix",
    )(x2, am, ad, wo, g2, wq, keys)


def _pack_kernel(t_ref, o_ref):
    t = t_ref[...]
    lo = pltpu.bitcast(t[:, 0:512].astype(jnp.bfloat16).astype(jnp.float32), jnp.int32)
    hi = pltpu.bitcast(t[:, 512:1024].astype(jnp.bfloat16).astype(jnp.float32), jnp.int32)
    o_ref[...] = lax.shift_right_logical(lo, 16) | (hi & jnp.int32(-65536))


def _pack_table(tab):
    n = tab.shape[0]
    return pl.pallas_call(
        _pack_kernel,
        grid=(n // PACK_ROWS,),
        in_specs=[pl.BlockSpec((PACK_ROWS, 1024), lambda i: (i, 0))],
        out_specs=pl.BlockSpec((PACK_ROWS, 512), lambda i: (i, 0)),
        out_shape=jax.ShapeDtypeStruct((n, 512), jnp.int32),
        compiler_params=pltpu.CompilerParams(dimension_semantics=("arbitrary",)),
        name="pack",
    )(tab)


def _decode(w):
    lo = pltpu.bitcast(w << 16, jnp.float32)
    hi = pltpu.bitcast(w & jnp.int32(-65536), jnp.float32)
    return lo, hi


def _peer_u_kernel(exp_ref, tab_ref, x_ref, g_ref, act_ref, q0_ref, q1_ref, qs_ref):
    ntok, nsel = exp_ref.shape
    ones = jnp.ones((SUBLANES, LANES), jnp.bfloat16)

    def products(t, q_ref):
        x = x_ref[t]
        xlo = x[0:4]
        xhi = x[4:8]
        for e in range(nsel):
            lo, hi = _decode(tab_ref[exp_ref[t, e]])
            q_ref[pl.ds(4 * e, 4), :] = lo * xlo + hi * xhi

    def fold(t, q_ref):
        q = q_ref[pl.ds(0, nsel, stride=4), :]
        for s in range(1, 4):
            q = q + q_ref[pl.ds(s, nsel, stride=4), :]
        qs_ref[pl.ds(pl.multiple_of(t * nsel, nsel), nsel), :] = q

    def gather_group(g):
        for k in range(PEER_GROUP):
            q_ref = (q0_ref, q1_ref)[k % 2]
            products(g * PEER_GROUP + k, q_ref)
            fold(g * PEER_GROUP + k, q_ref)

    def finish_group(g):
        rows8 = PEER_GROUP * nsel
        q_hi, q_lo = _split_bf16(qs_ref[pl.ds(pl.multiple_of(g * rows8, rows8), rows8), :])
        row = (_dot_nt(ones, q_hi) + _dot_nt(ones, q_lo))[0:1]
        h = jnp.concatenate([row[:, k * nsel:(k + 1) * nsel] for k in range(PEER_GROUP)], axis=0)
        gelu = 0.5 * h * (1.0 + lax.erf(h * (1.0 / math.sqrt(2.0))))
        rows = pl.ds(pl.multiple_of(g * PEER_GROUP, PEER_GROUP), PEER_GROUP)
        act_ref[rows, :] = gelu * g_ref[rows, :]

    ngroups = ntok // PEER_GROUP
    gather_group(0)

    def body(g, carry):
        finish_group(g - 1)
        gather_group(g)
        return carry

    lax.fori_loop(1, ngroups, body, 0)
    finish_group(ngroups - 1)


def _sc_peer_v(vtab, idx, actrep, x1, tok0, ntok):
    d = x1.shape[1]
    nsel = idx.shape[1]
    info = plsc.get_sparse_core_info()
    nw = info.num_cores * info.num_subcores
    per_w = ntok // nw
    nchunk = nsel // SC_ROWS
    nitem = SC_TB * nchunk
    assert info.num_lanes == SC_LANES and ntok % (nw * SC_TB) == 0 and nitem % 2 == 0
    mesh = plsc.VectorSubcoreMesh(core_axis_name="c", subcore_axis_name="s")

    @functools.partial(
        pl.kernel, mesh=mesh,
        out_type=jax.ShapeDtypeStruct((ntok, d), jnp.float32),
        scratch_types=[
            pltpu.VMEM((SC_TB, nsel), jnp.int32),
            pltpu.VMEM((SC_TB, nsel * SC_LANES), jnp.float32),
            pltpu.VMEM((SC_ROWS, d), jnp.float32),
            pltpu.VMEM((SC_ROWS, d), jnp.float32),
            pltpu.VMEM((SC_TB, d), jnp.float32),
            pltpu.SemaphoreType.DMA,
            pltpu.SemaphoreType.DMA,
        ],
        name="sc_peer_v",
    )
    def k(vtab_hbm, idx_hbm, act_hbm, x1_hbm, out_hbm, idx_v, act_v, rows0, rows1, acc_v, sem0, sem1):
        wid = lax.axis_index("s") * info.num_cores + lax.axis_index("c")

        def gather(item, rows, sem):
            tl = item // nchunk
            ch = item % nchunk
            return pltpu.make_async_copy(vtab_hbm.at[idx_v.at[tl, pl.ds(ch * SC_ROWS, SC_ROWS)]], rows, sem)

        def accumulate(item, rows):
            tl = item // nchunk
            r0 = (item % nchunk) * SC_ROWS
            a = [act_v[tl, pl.ds((r0 + e) * SC_LANES, SC_LANES)] for e in range(SC_ROWS)]

            @plsc.parallel_loop(0, d // SC_LANES, 1, unroll=2)
            def _(c):
                col = pl.ds(c * SC_LANES, SC_LANES)
                terms = [a[e] * rows[e, col] for e in range(SC_ROWS)]
                while len(terms) > 1:
                    terms = [terms[j] + terms[j + 1] for j in range(0, len(terms), 2)]
                acc_v[tl, col] = acc_v[tl, col] + terms[0]

        @pl.loop(0, per_w // SC_TB)
        def _(bi):
            loc = pl.multiple_of(wid * per_w + bi * SC_TB, SC_TB)
            src = pl.ds(tok0 + loc, SC_TB)
            pltpu.sync_copy(idx_hbm.at[src], idx_v)
            pltpu.sync_copy(act_hbm.at[pl.ds(loc, SC_TB)], act_v)
            pltpu.sync_copy(x1_hbm.at[src], acc_v)
            gather(0, rows0, sem0).start()

            @pl.loop(0, nitem // 2)
            def _(p):
                gather(2 * p + 1, rows1, sem1).start()
                gather(2 * p, rows0, sem0).wait()
                accumulate(2 * p, rows0)

                @pl.when(p < nitem // 2 - 1)
                def _():
                    gather(2 * p + 2, rows0, sem0).start()

                gather(2 * p + 1, rows1, sem1).wait()
                accumulate(2 * p + 1, rows1)

            pltpu.sync_copy(acc_v, out_hbm.at[pl.ds(loc, SC_TB)])

    return k(vtab, idx, actrep, x1)


def _peer_v_kernel(exp_ref, act_ref, tab_ref, x_ref, o_ref):
    nsel = exp_ref.shape[1]

    def tok(t, carry):
        acc_lo = jnp.zeros((4, LANES), jnp.float32)
        acc_hi = jnp.zeros((4, LANES), jnp.float32)
        for e in range(nsel):
            a = act_ref[t, e]
            lo, hi = _decode(tab_ref[exp_ref[t, e]])
            acc_lo = acc_lo + a * lo
            acc_hi = acc_hi + a * hi
        o_ref[t] = x_ref[t] + jnp.concatenate([acc_lo, acc_hi], axis=0)
        return carry

    lax.fori_loop(0, x_ref.shape[0], tok, 0)


def _peer_experts(xn3, x1, experts, gates, u_packed, v_table, v_packed, n_chunks, n_sc):
    T, nsel = experts.shape
    n_exp = u_packed.shape[0]
    d = x1.shape[1]
    tc = T // n_chunks
    steps = tc // PEER_TT
    assert nsel == LANES and PEER_TT % PEER_GROUP == 0 and T % (n_chunks * PEER_TT) == 0
    tab_spec = pl.BlockSpec((n_exp, 4, LANES), lambda i: (0, 0, 0), pipeline_mode=pl.Buffered(1))
    cp = pltpu.CompilerParams(dimension_semantics=("arbitrary",), vmem_limit_bytes=VMEM_LIMIT)
    x13 = x1.reshape(T, SUBLANES, LANES) if n_sc < n_chunks else None
    outs = []
    for c in range(n_chunks):
        off = c * steps
        row = lambda i, off=off: (i + off, 0)
        row3 = lambda i, off=off: (i + off, 0, 0)
        act = pl.pallas_call(
            _peer_u_kernel,
            grid=(steps,),
            in_specs=[pl.BlockSpec((PEER_TT, nsel), row, memory_space=pltpu.SMEM), tab_spec,
                      pl.BlockSpec((PEER_TT, SUBLANES, LANES), row3), pl.BlockSpec((PEER_TT, nsel), row)],
            out_specs=pl.BlockSpec((PEER_TT, nsel), lambda i: (i, 0)),
            out_shape=jax.ShapeDtypeStruct((tc, nsel), jnp.float32),
            scratch_shapes=[pltpu.VMEM((4 * nsel, LANES), jnp.float32), pltpu.VMEM((4 * nsel, LANES), jnp.float32),
                            pltpu.VMEM((PEER_TT * nsel, LANES), jnp.float32)],
            compiler_params=cp,
            name="peer_u",
        )(experts, u_packed, xn3, gates)
        if c < n_sc:
            actrep = jnp.broadcast_to(act[:, :, None], (tc, nsel, SC_LANES)).reshape(tc, nsel * SC_LANES)
            outs.append(_sc_peer_v(v_table, experts, actrep, x1, c * tc, tc))
        else:
            out3 = pl.pallas_call(
                _peer_v_kernel,
                grid=(steps,),
                in_specs=[pl.BlockSpec((PEER_TT, nsel), row, memory_space=pltpu.SMEM),
                          pl.BlockSpec((PEER_TT, nsel), lambda i: (i, 0), memory_space=pltpu.SMEM),
                          tab_spec, pl.BlockSpec((PEER_TT, SUBLANES, LANES), row3)],
                out_specs=pl.BlockSpec((PEER_TT, SUBLANES, LANES), lambda i: (i, 0, 0)),
                out_shape=jax.ShapeDtypeStruct((tc, SUBLANES, LANES), jnp.float32),
                compiler_params=cp,
                name="peer_v",
            )(experts, act, v_packed, x13)
            outs.append(out3.reshape(tc, d))
    return jnp.concatenate(outs, axis=0)


def _alibi_slopes(n):
    return 2.0 ** (-8.0 * np.arange(1, n + 1) / n)


def _lambda_init(layer):
    return 0.8 - 0.6 * math.exp(-0.3 * layer)


def kernel(x, norm1_g, w_in, moba_qk_norm_g, diff_qk_norm_g, diff_lambda, diff_subln_g, w_out, norm2_g,
           peer_w_query, peer_sub_keys, peer_u, peer_v):
    b, s, d = x.shape
    T = b * s
    depth = w_in.shape[0]
    mw = MOBA_HEADS * HEAD_DIM
    assert d == 1024 and s % PROJ_TM == 0 and mw == 512 and DIFF_HEADS * 2 * HEAD_DIM == 512
    f32, bf16 = jnp.float32, jnp.bfloat16
    scale = HEAD_DIM ** -0.5
    grp = lax.broadcasted_iota(jnp.int32, (LANES, LANES), 0) // HEAD_DIM
    gsum = (grp == grp.T).astype(bf16)
    moba_slopes = jnp.asarray(_alibi_slopes(MOBA_HEADS).reshape(MOBA_HEADS // 2, 2), f32)
    diff_slopes = jnp.asarray(np.repeat(_alibi_slopes(DIFF_HEADS)[:, None], 2, axis=1), f32)
    assert b % PIPE_GROUPS == 0
    bg = b // PIPE_GROUPS
    tg = bg * s
    nblk = PROJ_TM // MOBA_BLOCK
    x2 = x.reshape(T, d)
    for l in range(depth):
        w = w_in[l]
        wqk = jnp.concatenate([w[:, 0:2 * mw], w[:, 3 * mw:3 * mw + 1024]], axis=1).astype(bf16)
        wvt = jnp.concatenate([w[:, 2 * mw:3 * mw], w[:, 3 * mw + 1024:]], axis=1).T.astype(bf16)
        mg, dg = moba_qk_norm_g[l].astype(f32), diff_qk_norm_g[l].astype(f32)
        gqk = jnp.concatenate([jnp.tile(mg[0] * scale, 8), jnp.tile(mg[1], 8),
                               jnp.tile(dg[0] * scale, 8), jnp.tile(dg[1], 8)])[None, :]
        g1 = norm1_g[l][None, :].astype(f32)
        g2 = norm2_g[l][None, :].astype(f32)
        gsub = jnp.broadcast_to(diff_subln_g[l].astype(f32)[:, None], (LANES, MOBA_BLOCK))
        lam_p = diff_lambda[l].astype(f32)
        wo, wq = w_out[l].astype(bf16), peer_w_query[l].astype(bf16)
        keys = peer_sub_keys[l].reshape(2 * PEER_HEADS, PEER_N_KEYS, LANES).astype(bf16)
        u_packed = _pack_table(peer_u[l]).reshape(peer_u.shape[1], 4, LANES)
        v_packed = _pack_table(peer_v[l]).reshape(peer_v.shape[1], 4, LANES)
        v_table = peer_v[l].astype(f32)
        outs = []
        for g in range(PIPE_GROUPS):
            xg = x2[g * tg:(g + 1) * tg]
            mq, mk, dq, dk, mvt, dvt, kmean = _project(xg, g1, wqk, wvt, gqk, gsum)
            kmean = kmean.reshape(tg // PROJ_TM, SUBLANES, 512)[:, :nblk].reshape(tg // MOBA_BLOCK, 512)
            moba_o = _moba_attention(mq, mk, mvt, kmean, moba_slopes, bg, s)
            diff_o = _diff_attention(dq, dk, dvt, lam_p, gsub, diff_slopes, _lambda_init(l), bg, s)
            x1, xn, experts, gates = _mix(xg, moba_o, diff_o, wo, g2, wq, keys)
            last = g == PIPE_GROUPS - 1
            n_chunks, n_sc = (LAST_CHUNKS, LAST_SC_CHUNKS) if last else (PEER_CHUNKS, PEER_CHUNKS)
            outs.append(_peer_experts(xn.reshape(tg, SUBLANES, LANES), x1, experts, gates, u_packed, v_table,
                                      v_packed, n_chunks, n_sc))
        x2 = jnp.concatenate(outs, axis=0)
    return x2.reshape(b, s, d)
```

```python
import functools
import math

import jax
import jax.numpy as jnp
import numpy as np
from jax import lax
from jax.experimental import pallas as pl
from jax.experimental.pallas import tpu as pltpu
from jax.experimental.pallas import tpu_sc as plsc

HEAD_DIM = 64
MOBA_HEADS = 8
MOBA_BLOCK = 256
MOBA_TOPK = 3
DIFF_HEADS = 4
PEER_HEADS = 8
PEER_N_KEYS = 128
PEER_TOPK = 16
RMS_EPS = 1e-6

LANES = 128
SUBLANES = 8
VMEM_LIMIT = 48 * 1024 * 1024

PROJ_TM = 512
MIX_TM = 256
PEER_TT = 64
PACK_ROWS = 512
ATT_QBLOCKS = 2
PEER_GROUP = 8
PIPE_GROUPS = 4
PEER_CHUNKS = 2
LAST_CHUNKS = 4
LAST_SC_CHUNKS = 3
SC_LANES = 16
SC_ROWS = 32
SC_TB = 8

NEG_INF = float("-inf")
_NT = (((1,), (1,)), ((), ()))


def _dot(a, b):
    return jnp.dot(a, b, preferred_element_type=jnp.float32)


def _dot_nt(a, b):
    return lax.dot_general(a, b, _NT, preferred_element_type=jnp.float32)


def _split_bf16(x):
    hi = x.astype(jnp.bfloat16)
    lo = (x - hi.astype(jnp.float32)).astype(jnp.bfloat16)
    return hi, lo


def _proj_kernel(x_ref, g1_ref, wqk_ref, wvt_ref, gqk_ref, gsum_ref,
                 mq_ref, mk_ref, dq_ref, dk_ref, mvt_ref, dvt_ref, kmean_ref):
    x = x_ref[...]
    ms = jnp.mean(x * x, axis=-1, keepdims=True)
    h = (x * lax.rsqrt(ms + RMS_EPS) * g1_ref[...]).astype(jnp.bfloat16)
    gsum = gsum_ref[...]
    outs = (mq_ref, mk_ref, dq_ref, dk_ref)
    for sec in range(4):
        for half in range(2):
            c0 = sec * 512 + half * 256
            p = _dot(h, wqk_ref[:, c0:c0 + 256])
            for s in range(2):
                ps = p[:, s * LANES:(s + 1) * LANES]
                sq_hi, sq_lo = _split_bf16(ps * ps)
                ssq = _dot(sq_hi, gsum) + _dot(sq_lo, gsum)
                g = gqk_ref[:, c0 + s * LANES:c0 + (s + 1) * LANES]
                y = ps * lax.rsqrt(ssq * (1.0 / HEAD_DIM) + RMS_EPS) * g
                l0 = half * 256 + s * LANES
                outs[sec][:, l0:l0 + LANES] = y.astype(jnp.bfloat16)
                if sec == 1:
                    for blk in range(PROJ_TM // MOBA_BLOCK):
                        kmean_ref[blk:blk + 1, l0:l0 + LANES] = jnp.mean(
                            y[blk * MOBA_BLOCK:(blk + 1) * MOBA_BLOCK], axis=0, keepdims=True)
    kmean_ref[PROJ_TM // MOBA_BLOCK:, :] = jnp.zeros((SUBLANES - PROJ_TM // MOBA_BLOCK, 512), jnp.float32)
    vt = _dot_nt(wvt_ref[...], h).astype(jnp.bfloat16)
    for blk in range(PROJ_TM // MOBA_BLOCK):
        mvt_ref[blk] = vt[0:512, blk * MOBA_BLOCK:(blk + 1) * MOBA_BLOCK]
        dvt_ref[blk] = vt[512:1024, blk * MOBA_BLOCK:(blk + 1) * MOBA_BLOCK]


def _project(x2, g1, wqk, wvt, gqk, gsum):
    T = x2.shape[0]
    nt = T // PROJ_TM
    nblk = PROJ_TM // MOBA_BLOCK
    row = lambda i: (i, 0)
    const = lambda i: (0, 0)
    qk_shape = jax.ShapeDtypeStruct((T, 512), jnp.bfloat16)
    vt_shape = jax.ShapeDtypeStruct((T // MOBA_BLOCK, 512, MOBA_BLOCK), jnp.bfloat16)
    return pl.pallas_call(
        _proj_kernel,
        grid=(nt,),
        in_specs=[
            pl.BlockSpec((PROJ_TM, 1024), row),
            pl.BlockSpec((1, 1024), const),
            pl.BlockSpec((1024, 2048), const),
            pl.BlockSpec((1024, 1024), const),
            pl.BlockSpec((1, 2048), const),
            pl.BlockSpec((LANES, LANES), const),
        ],
        out_specs=[
            pl.BlockSpec((PROJ_TM, 512), row),
            pl.BlockSpec((PROJ_TM, 512), row),
            pl.BlockSpec((PROJ_TM, 512), row),
            pl.BlockSpec((PROJ_TM, 512), row),
            pl.BlockSpec((nblk, 512, MOBA_BLOCK), lambda i: (i, 0, 0)),
            pl.BlockSpec((nblk, 512, MOBA_BLOCK), lambda i: (i, 0, 0)),
            pl.BlockSpec((SUBLANES, 512), row),
        ],
        out_shape=[qk_shape, qk_shape, qk_shape, qk_shape, vt_shape, vt_shape,
                   jax.ShapeDtypeStruct((nt * SUBLANES, 512), jnp.float32)],
        compiler_params=pltpu.CompilerParams(dimension_semantics=("arbitrary",), vmem_limit_bytes=VMEM_LIMIT),
        name="proj",
    )(x2, g1, wqk, wvt, gqk, gsum)


def _rel_qk():
    qc = lax.broadcasted_iota(jnp.int32, (MOBA_BLOCK, MOBA_BLOCK), 1)
    kr = lax.broadcasted_iota(jnp.int32, (MOBA_BLOCK, MOBA_BLOCK), 0)
    return qc - kr


def _attend(q_h, k_ref, vt_load, i, slope, rel, selbias):
    nb = rel.astype(jnp.float32) * (-slope)
    ts, rowbs, colmax = [], [], []
    for j in range(i + 1):
        t = _dot_nt(k_ref[j * MOBA_BLOCK:(j + 1) * MOBA_BLOCK, :], q_h) + nb
        if j == i:
            t = jnp.where(rel >= 0, t, NEG_INF)
            rowb = jnp.zeros((1, MOBA_BLOCK), jnp.float32)
        else:
            rowb = jnp.full((1, MOBA_BLOCK), (i - j) * MOBA_BLOCK, jnp.float32) * (-slope)
            if selbias is not None:
                rowb = rowb + selbias[j:j + 1, :]
        ts.append(t)
        rowbs.append(rowb)
        colmax.append(jnp.max(t, axis=0, keepdims=True) + rowb)
    m = functools.reduce(jnp.maximum, colmax)
    l, acc = None, None
    for j in range(i + 1):
        p = jnp.exp(ts[j] - (m - rowbs[j]))
        lj = jnp.sum(p, axis=0, keepdims=True)
        aj = _dot(vt_load(j), p.astype(jnp.bfloat16))
        l = lj if l is None else l + lj
        acc = aj if acc is None else acc + aj
    return acc / l


def _head_queries(q2):
    lane = lax.broadcasted_iota(jnp.int32, q2.shape, 1)
    return [jnp.where((lane >= hh * HEAD_DIM) & (lane < (hh + 1) * HEAD_DIM), q2, jnp.zeros_like(q2))
            for hh in range(2)]


def _block_selection(gate, i):
    blk = lax.broadcasted_iota(jnp.int32, gate.shape, 0)
    cnt = jnp.zeros(gate.shape, jnp.int32)
    for mblk in range(i):
        gm = gate[mblk:mblk + 1, :]
        beats = (gm > gate) | ((gm == gate) & (mblk < blk))
        cnt = cnt + jnp.where(beats, 1, 0)
    return jnp.where((blk < i) & (cnt < MOBA_TOPK), 0.0, NEG_INF)


def _moba_kernel(slopes_ref, q_ref, k_ref, vt_ref, kmean_ref, o_ref):
    pair = pl.program_id(1)
    step = pl.program_id(2)
    nq = vt_ref.shape[0]
    rel = _rel_qk()
    km_hi, km_lo = _split_bf16(kmean_ref[...])
    for st in range(nq // ATT_QBLOCKS):
        @pl.when(step == st)
        def _(st=st):
            for u in range(ATT_QBLOCKS):
                i = st * ATT_QBLOCKS + u
                rows = slice(u * MOBA_BLOCK, (u + 1) * MOBA_BLOCK)
                outs = []
                for hh, q_h in enumerate(_head_queries(q_ref[rows, :])):
                    gate = _dot_nt(km_hi, q_h) + _dot_nt(km_lo, q_h)
                    vt_load = lambda j, hh=hh: vt_ref[j, hh * HEAD_DIM:(hh + 1) * HEAD_DIM, :]
                    outs.append(_attend(q_h, k_ref, vt_load, i, slopes_ref[pair, hh], rel,
                                        _block_selection(gate, i)))
                o_t = jnp.concatenate(outs, axis=0)
                o_ref[rows, :] = o_t.T.astype(o_ref.dtype)


def _diff_kernel(slopes_ref, lam_ref, q_ref, k_ref, vt_ref, gsub_ref, o_ref, *, lam_init):
    head = pl.program_id(1)
    step = pl.program_id(2)
    nq = vt_ref.shape[0]
    rel = _rel_qk()
    slope = slopes_ref[head, 0]
    lp = lam_ref[...]
    lam = (jnp.exp(jnp.sum(lp[0:1] * lp[1:2], axis=-1, keepdims=True))
           - jnp.exp(jnp.sum(lp[2:3] * lp[3:4], axis=-1, keepdims=True)) + lam_init)
    vt_load = lambda j: vt_ref[j]
    for st in range(nq // ATT_QBLOCKS):
        @pl.when(step == st)
        def _(st=st):
            for u in range(ATT_QBLOCKS):
                i = st * ATT_QBLOCKS + u
                rows = slice(u * MOBA_BLOCK, (u + 1) * MOBA_BLOCK)
                parts = [_attend(q_h, k_ref, vt_load, i, slope, rel, None)
                         for q_h in _head_queries(q_ref[rows, :])]
                o_t = parts[0] - lam * parts[1]
                ms = jnp.mean(o_t * o_t, axis=0, keepdims=True)
                o_t = o_t * lax.rsqrt(ms + RMS_EPS) * gsub_ref[...] * (1.0 - lam_init)
                o_ref[rows, :] = o_t.T.astype(o_ref.dtype)


def _attention_specs(nq):
    nst = nq // ATT_QBLOCKS
    q_spec = pl.BlockSpec((ATT_QBLOCKS * MOBA_BLOCK, LANES), lambda b, p, i: (b * nst + i, p))
    k_spec = pl.BlockSpec((nq * MOBA_BLOCK, LANES), lambda b, p, i: (b, p))
    vt_spec = pl.BlockSpec((nq, LANES, MOBA_BLOCK), lambda b, p, i: (b, p, 0))
    return q_spec, k_spec, vt_spec


def _moba_attention(q, k, vt, kmean, slopes, batch, seq):
    nq = seq // MOBA_BLOCK
    q_spec, k_spec, vt_spec = _attention_specs(nq)
    return pl.pallas_call(
        _moba_kernel,
        grid=(batch, MOBA_HEADS // 2, nq // ATT_QBLOCKS),
        in_specs=[
            pl.BlockSpec(memory_space=pltpu.SMEM),
            q_spec, k_spec, vt_spec,
            pl.BlockSpec((SUBLANES, LANES), lambda b, p, i: (b, p)),
        ],
        out_specs=q_spec,
        out_shape=jax.ShapeDtypeStruct(q.shape, jnp.bfloat16),
        compiler_params=pltpu.CompilerParams(
            dimension_semantics=("arbitrary", "arbitrary", "arbitrary"), vmem_limit_bytes=VMEM_LIMIT),
        name="moba",
    )(slopes, q, k, vt, kmean)


def _diff_attention(q, k, vt, lam_p, gsub, slopes, lam_init, batch, seq):
    nq = seq // MOBA_BLOCK
    q_spec, k_spec, vt_spec = _attention_specs(nq)
    return pl.pallas_call(
        functools.partial(_diff_kernel, lam_init=lam_init),
        grid=(batch, DIFF_HEADS, nq // ATT_QBLOCKS),
        in_specs=[
            pl.BlockSpec(memory_space=pltpu.SMEM),
            pl.BlockSpec((4, HEAD_DIM), lambda b, p, i: (0, 0)),
            q_spec, k_spec, vt_spec,
            pl.BlockSpec((LANES, MOBA_BLOCK), lambda b, p, i: (0, 0)),
        ],
        out_specs=q_spec,
        out_shape=jax.ShapeDtypeStruct(q.shape, jnp.bfloat16),
        compiler_params=pltpu.CompilerParams(
            dimension_semantics=("arbitrary", "arbitrary", "arbitrary"), vmem_limit_bytes=VMEM_LIMIT),
        name="diff",
    )(slopes, lam_p, q, k, vt, gsub)


def _knockout_topk(sc, k, payload=None):
    n = sc.shape[0]
    row = lax.broadcasted_iota(jnp.int32, sc.shape, 0).astype(jnp.float32)
    vals, picks = [], []
    for _ in range(k):
        m = jnp.max(sc, axis=0, keepdims=True)
        idx = jnp.min(jnp.where(sc == m, row, float(n)), axis=0, keepdims=True)
        hit = row == idx
        vals.append(m)
        picks.append(idx if payload is None else jnp.sum(jnp.where(hit, payload, 0.0), axis=0, keepdims=True))
        sc = jnp.where(hit, NEG_INF, sc)
    return vals, picks


def _mix_kernel(x_ref, am_ref, ad_ref, wo_ref, g2_ref, wq_ref, keys_ref,
                x1_ref, xn_ref, exp_ref, gate_ref):
    x1 = x_ref[...] + _dot(am_ref[...], wo_ref[0:512, :]) + _dot(ad_ref[...], wo_ref[512:1024, :])
    x1_ref[...] = x1
    ms = jnp.mean(x1 * x1, axis=-1, keepdims=True)
    xn = x1 * lax.rsqrt(ms + RMS_EPS) * g2_ref[...]
    xn_ref[...] = xn
    qq = _dot(xn.astype(jnp.bfloat16), wq_ref[...]).astype(jnp.bfloat16)
    tm = qq.shape[0]
    exp_rows, gate_rows = [], []
    for h in range(PEER_HEADS):
        tops = []
        for half in range(2):
            s = h * 2 + half
            sc = _dot_nt(keys_ref[s], qq[:, s * LANES:(s + 1) * LANES])
            tops.append(_knockout_topk(sc, PEER_TOPK))
        (s1, i1), (s2, i2) = tops
        s2a = jnp.concatenate(s2, axis=0)
        i2a = jnp.concatenate(i2, axis=0)
        cand, cidx = [], []
        for a in range(PEER_TOPK):
            nb = PEER_TOPK // (a + 1)
            cand.append(s1[a] + s2a[0:nb])
            cidx.append(i1[a] * float(PEER_N_KEYS) + i2a[0:nb])
        n_cand = sum(c.shape[0] for c in cand)
        pad = -n_cand % SUBLANES
        cand.append(jnp.full((pad, tm), NEG_INF, jnp.float32))
        cidx.append(jnp.zeros((pad, tm), jnp.float32))
        best, experts = _knockout_topk(jnp.concatenate(cand, axis=0), PEER_TOPK,
                                       payload=jnp.concatenate(cidx, axis=0))
        b = jnp.concatenate(best, axis=0)
        e = jnp.exp(b - b[0:1])
        gate_rows.append(e / jnp.sum(e, axis=0, keepdims=True))
        exp_rows.append(jnp.concatenate(experts, axis=0))
    exp_ref[...] = jnp.concatenate(exp_rows, axis=0).T.astype(jnp.int32)
    gate_ref[...] = jnp.concatenate(gate_rows, axis=0).T


def _mix(x2, am, ad, wo, g2, wq, keys):
    T = x2.shape[0]
    row = lambda i: (i, 0)
    const = lambda i: (0, 0)
    return pl.pallas_call(
        _mix_kernel,
        grid=(T // MIX_TM,),
        in_specs=[
            pl.BlockSpec((MIX_TM, 1024), row),
            pl.BlockSpec((MIX_TM, 512), row),
            pl.BlockSpec((MIX_TM, 512), row),
            pl.BlockSpec((1024, 1024), const),
            pl.BlockSpec((1, 1024), const),
            pl.BlockSpec((1024, 2048), const),
            pl.BlockSpec((2 * PEER_HEADS, PEER_N_KEYS, LANES), lambda i: (0, 0, 0)),
        ],
        out_specs=[
            pl.BlockSpec((MIX_TM, 1024), row),
            pl.BlockSpec((MIX_TM, 1024), row),
            pl.BlockSpec((MIX_TM, LANES), row),
            pl.BlockSpec((MIX_TM, LANES), row),
        ],
        out_shape=[
            jax.ShapeDtypeStruct((T, 1024), jnp.float32),
            jax.ShapeDtypeStruct((T, 1024), jnp.float32),
            jax.ShapeDtypeStruct((T, LANES), jnp.int32),
            jax.ShapeDtypeStruct((T, LANES), jnp.float32),
        ],
        compiler_params=pltpu.CompilerParams(dimension_semantics=("arbitrary",), vmem_limit_bytes=VMEM_LIMIT),
        name="mix",
    )(x2, am, ad, wo, g2, wq, keys)


def _pack_kernel(t_ref, o_ref):
    t = t_ref[...]
    lo = pltpu.bitcast(t[:, 0:512].astype(jnp.bfloat16).astype(jnp.float32), jnp.int32)
    hi = pltpu.bitcast(t[:, 512:1024].astype(jnp.bfloat16).astype(jnp.float32), jnp.int32)
    o_ref[...] = lax.shift_right_logical(lo, 16) | (hi & jnp.int32(-65536))


def _pack_table(tab):
    n = tab.shape[0]
    return pl.pallas_call(
        _pack_kernel,
        grid=(n // PACK_ROWS,),
        in_specs=[pl.BlockSpec((PACK_ROWS, 1024), lambda i: (i, 0))],
        out_specs=pl.BlockSpec((PACK_ROWS, 512), lambda i: (i, 0)),
        out_shape=jax.ShapeDtypeStruct((n, 512), jnp.int32),
        compiler_params=pltpu.CompilerParams(dimension_semantics=("arbitrary",)),
        name="pack",
    )(tab)


def _decode(w):
    lo = pltpu.bitcast(w << 16, jnp.float32)
    hi = pltpu.bitcast(w & jnp.int32(-65536), jnp.float32)
    return lo, hi


def _peer_u_kernel(exp_ref, tab_ref, x_ref, g_ref, act_ref, q0_ref, q1_ref, qs_ref):
    ntok, nsel = exp_ref.shape
    ones = jnp.ones((SUBLANES, LANES), jnp.bfloat16)

    def products(t, q_ref):
        x = x_ref[t]
        xlo = x[0:4]
        xhi = x[4:8]
        for e in range(nsel):
            lo, hi = _decode(tab_ref[exp_ref[t, e]])
            q_ref[pl.ds(4 * e, 4), :] = lo * xlo + hi * xhi

    def fold(t, q_ref):
        q = q_ref[pl.ds(0, nsel, stride=4), :]
        for s in range(1, 4):
            q = q + q_ref[pl.ds(s, nsel, stride=4), :]
        qs_ref[pl.ds(pl.multiple_of(t * nsel, nsel), nsel), :] = q

    def gather_group(g):
        for k in range(PEER_GROUP):
            q_ref = (q0_ref, q1_ref)[k % 2]
            products(g * PEER_GROUP + k, q_ref)
            fold(g * PEER_GROUP + k, q_ref)

    def finish_group(g):
        rows8 = PEER_GROUP * nsel
        q_hi, q_lo = _split_bf16(qs_ref[pl.ds(pl.multiple_of(g * rows8, rows8), rows8), :])
        row = (_dot_nt(ones, q_hi) + _dot_nt(ones, q_lo))[0:1]
        h = jnp.concatenate([row[:, k * nsel:(k + 1) * nsel] for k in range(PEER_GROUP)], axis=0)
        gelu = 0.5 * h * (1.0 + lax.erf(h * (1.0 / math.sqrt(2.0))))
        rows = pl.ds(pl.multiple_of(g * PEER_GROUP, PEER_GROUP), PEER_GROUP)
        act_ref[rows, :] = gelu * g_ref[rows, :]

    ngroups = ntok // PEER_GROUP
    gather_group(0)

    def body(g, carry):
        finish_group(g - 1)
        gather_group(g)
        return carry

    lax.fori_loop(1, ngroups, body, 0)
    finish_group(ngroups - 1)


def _sc_peer_v(vtab, idx, actrep, x1, tok0, ntok):
    d = x1.shape[1]
    nsel = idx.shape[1]
    info = plsc.get_sparse_core_info()
    nw = info.num_cores * info.num_subcores
    per_w = ntok // nw
    nchunk = nsel // SC_ROWS
    nitem = SC_TB * nchunk
    assert info.num_lanes == SC_LANES and ntok % (nw * SC_TB) == 0 and nitem % 2 == 0
    mesh = plsc.VectorSubcoreMesh(core_axis_name="c", subcore_axis_name="s")

    @functools.partial(
        pl.kernel, mesh=mesh,
        out_type=jax.ShapeDtypeStruct((ntok, d), jnp.float32),
        scratch_types=[
            pltpu.VMEM((SC_TB, nsel), jnp.int32),
            pltpu.VMEM((SC_TB, nsel * SC_LANES), jnp.float32),
            pltpu.VMEM((SC_ROWS, d), jnp.float32),
            pltpu.VMEM((SC_ROWS, d), jnp.float32),
            pltpu.VMEM((SC_TB, d), jnp.float32),
            pltpu.SemaphoreType.DMA,
            pltpu.SemaphoreType.DMA,
        ],
        name="sc_peer_v",
    )
    def k(vtab_hbm, idx_hbm, act_hbm, x1_hbm, out_hbm, idx_v, act_v, rows0, rows1, acc_v, sem0, sem1):
        wid = lax.axis_index("s") * info.num_cores + lax.axis_index("c")

        def gather(item, rows, sem):
            tl = item // nchunk
            ch = item % nchunk
            return pltpu.make_async_copy(vtab_hbm.at[idx_v.at[tl, pl.ds(ch * SC_ROWS, SC_ROWS)]], rows, sem)

        def accumulate(item, rows):
            tl = item // nchunk
            r0 = (item % nchunk) * SC_ROWS
            a = [act_v[tl, pl.ds((r0 + e) * SC_LANES, SC_LANES)] for e in range(SC_ROWS)]

            @plsc.parallel_loop(0, d // SC_LANES, 1, unroll=2)
            def _(c):
                col = pl.ds(c * SC_LANES, SC_LANES)
                terms = [a[e] * rows[e, col] for e in range(SC_ROWS)]
                while len(terms) > 1:
                    terms = [terms[j] + terms[j + 1] for j in range(0, len(terms), 2)]
                acc_v[tl, col] = acc_v[tl, col] + terms[0]

        @pl.loop(0, per_w // SC_TB)
        def _(bi):
            loc = pl.multiple_of(wid * per_w + bi * SC_TB, SC_TB)
            src = pl.ds(tok0 + loc, SC_TB)
            pltpu.sync_copy(idx_hbm.at[src], idx_v)
            pltpu.sync_copy(act_hbm.at[pl.ds(loc, SC_TB)], act_v)
            pltpu.sync_copy(x1_hbm.at[src], acc_v)
            gather(0, rows0, sem0).start()

            @pl.loop(0, nitem // 2)
            def _(p):
                gather(2 * p + 1, rows1, sem1).start()
                gather(2 * p, rows0, sem0).wait()
                accumulate(2 * p, rows0)

                @pl.when(p < nitem // 2 - 1)
                def _():
                    gather(2 * p + 2, rows0, sem0).start()

                gather(2 * p + 1, rows1, sem1).wait()
                accumulate(2 * p + 1, rows1)

            pltpu.sync_copy(acc_v, out_hbm.at[pl.ds(loc, SC_TB)])

    return k(vtab, idx, actrep, x1)


def _peer_v_kernel(exp_ref, act_ref, tab_ref, x_ref, o_ref):
    nsel = exp_ref.shape[1]

    def tok(t, carry):
        acc_lo = jnp.zeros((4, LANES), jnp.float32)
        acc_hi = jnp.zeros((4, LANES), jnp.float32)
        for e in range(nsel):
            a = act_ref[t, e]
            lo, hi = _decode(tab_ref[exp_ref[t, e]])
            acc_lo = acc_lo + a * lo
            acc_hi = acc_hi + a * hi
        o_ref[t] = x_ref[t] + jnp.concatenate([acc_lo, acc_hi], axis=0)
        return carry

    lax.fori_loop(0, x_ref.shape[0], tok, 0)


def _peer_experts(xn3, x1, experts, gates, u_packed, v_table, v_packed, n_chunks, n_sc):
    T, nsel = experts.shape
    n_exp = u_packed.shape[0]
    d = x1.shape[1]
    tc = T // n_chunks
    steps = tc // PEER_TT
    assert nsel == LANES and PEER_TT % PEER_GROUP == 0 and T % (n_chunks * PEER_TT) == 0
    tab_spec = pl.BlockSpec((n_exp, 4, LANES), lambda i: (0, 0, 0), pipeline_mode=pl.Buffered(1))
    cp = pltpu.CompilerParams(dimension_semantics=("arbitrary",), vmem_limit_bytes=VMEM_LIMIT)
    x13 = x1.reshape(T, SUBLANES, LANES) if n_sc < n_chunks else None
    outs = []
    for c in range(n_chunks):
        off = c * steps
        row = lambda i, off=off: (i + off, 0)
        row3 = lambda i, off=off: (i + off, 0, 0)
        act = pl.pallas_call(
            _peer_u_kernel,
            grid=(steps,),
            in_specs=[pl.BlockSpec((PEER_TT, nsel), row, memory_space=pltpu.SMEM), tab_spec,
                      pl.BlockSpec((PEER_TT, SUBLANES, LANES), row3), pl.BlockSpec((PEER_TT, nsel), row)],
            out_specs=pl.BlockSpec((PEER_TT, nsel), lambda i: (i, 0)),
            out_shape=jax.ShapeDtypeStruct((tc, nsel), jnp.float32),
            scratch_shapes=[pltpu.VMEM((4 * nsel, LANES), jnp.float32), pltpu.VMEM((4 * nsel, LANES), jnp.float32),
                            pltpu.VMEM((PEER_TT * nsel, LANES), jnp.float32)],
            compiler_params=cp,
            name="peer_u",
        )(experts, u_packed, xn3, gates)
        if c < n_sc:
            actrep = jnp.broadcast_to(act[:, :, None], (tc, nsel, SC_LANES)).reshape(tc, nsel * SC_LANES)
            outs.append(_sc_peer_v(v_table, experts, actrep, x1, c * tc, tc))
        else:
            out3 = pl.pallas_call(
                _peer_v_kernel,
                grid=(steps,),
                in_specs=[pl.BlockSpec((PEER_TT, nsel), row, memory_space=pltpu.SMEM),
                          pl.BlockSpec((PEER_TT, nsel), lambda i: (i, 0), memory_space=pltpu.SMEM),
                          tab_spec, pl.BlockSpec((PEER_TT, SUBLANES, LANES), row3)],
                out_specs=pl.BlockSpec((PEER_TT, SUBLANES, LANES), lambda i: (i, 0, 0)),
                out_shape=jax.ShapeDtypeStruct((tc, SUBLANES, LANES), jnp.float32),
                compiler_params=cp,
                name="peer_v",
            )(experts, act, v_packed, x13)
            outs.append(out3.reshape(tc, d))
    return jnp.concatenate(outs, axis=0), act


def _alibi_slopes(n):
    return 2.0 ** (-8.0 * np.arange(1, n + 1) / n)


def _lambda_init(layer):
    return 0.8 - 0.6 * math.exp(-0.3 * layer)


def kernel(x, norm1_g, w_in, moba_qk_norm_g, diff_qk_norm_g, diff_lambda, diff_subln_g, w_out, norm2_g,
           peer_w_query, peer_sub_keys, peer_u, peer_v):
    b, s, d = x.shape
    T = b * s
    depth = w_in.shape[0]
    mw = MOBA_HEADS * HEAD_DIM
    assert d == 1024 and s % PROJ_TM == 0 and mw == 512 and DIFF_HEADS * 2 * HEAD_DIM == 512
    f32, bf16 = jnp.float32, jnp.bfloat16
    scale = HEAD_DIM ** -0.5
    grp = lax.broadcasted_iota(jnp.int32, (LANES, LANES), 0) // HEAD_DIM
    gsum = (grp == grp.T).astype(bf16)
    moba_slopes = jnp.asarray(_alibi_slopes(MOBA_HEADS).reshape(MOBA_HEADS // 2, 2), f32)
    diff_slopes = jnp.asarray(np.repeat(_alibi_slopes(DIFF_HEADS)[:, None], 2, axis=1), f32)
    assert b % PIPE_GROUPS == 0
    bg = b // PIPE_GROUPS
    tg = bg * s
    nblk = PROJ_TM // MOBA_BLOCK
    x2 = x.reshape(T, d)
    for l in range(depth):
        w = w_in[l]
        wqk = jnp.concatenate([w[:, 0:2 * mw], w[:, 3 * mw:3 * mw + 1024]], axis=1).astype(bf16)
        wvt = jnp.concatenate([w[:, 2 * mw:3 * mw], w[:, 3 * mw + 1024:]], axis=1).T.astype(bf16)
        mg, dg = moba_qk_norm_g[l].astype(f32), diff_qk_norm_g[l].astype(f32)
        gqk = jnp.concatenate([jnp.tile(mg[0] * scale, 8), jnp.tile(mg[1], 8),
                               jnp.tile(dg[0] * scale, 8), jnp.tile(dg[1], 8)])[None, :]
        g1 = norm1_g[l][None, :].astype(f32)
        g2 = norm2_g[l][None, :].astype(f32)
        gsub = jnp.broadcast_to(diff_subln_g[l].astype(f32)[:, None], (LANES, MOBA_BLOCK))
        lam_p = diff_lambda[l].astype(f32)
        wo, wq = w_out[l].astype(bf16), peer_w_query[l].astype(bf16)
        keys = peer_sub_keys[l].reshape(2 * PEER_HEADS, PEER_N_KEYS, LANES).astype(bf16)
        u_packed = _pack_table(peer_u[l]).reshape(peer_u.shape[1], 4, LANES)
        v_packed = _pack_table(peer_v[l]).reshape(peer_v.shape[1], 4, LANES)
        v_table = peer_v[l].astype(f32)
        outs = []
        last_act = None
        for g in range(PIPE_GROUPS):
            xg = x2[g * tg:(g + 1) * tg]
            if last_act is not None:
                xg, _ = lax.optimization_barrier((xg, last_act))
            mq, mk, dq, dk, mvt, dvt, kmean = _project(xg, g1, wqk, wvt, gqk, gsum)
            kmean = kmean.reshape(tg // PROJ_TM, SUBLANES, 512)[:, :nblk].reshape(tg // MOBA_BLOCK, 512)
            moba_o = _moba_attention(mq, mk, mvt, kmean, moba_slopes, bg, s)
            diff_o = _diff_attention(dq, dk, dvt, lam_p, gsub, diff_slopes, _lambda_init(l), bg, s)
            x1, xn, experts, gates = _mix(xg, moba_o, diff_o, wo, g2, wq, keys)
            last = g == PIPE_GROUPS - 1
            n_chunks, n_sc = (LAST_CHUNKS, LAST_SC_CHUNKS) if last else (PEER_CHUNKS, PEER_CHUNKS)
            out_g, last_act = _peer_experts(xn.reshape(tg, SUBLANES, LANES), x1, experts, gates, u_packed, v_table,
                                            v_packed, n_chunks, n_sc)
            outs.append(out_g)
        x2 = jnp.concatenate(outs, axis=0)
    return x2.reshape(b, s, d)
```

```python
import functools
import math

import jax
import jax.numpy as jnp
import numpy as np
from jax import lax
from jax.experimental import pallas as pl
from jax.experimental.pallas import tpu as pltpu
from jax.experimental.pallas import tpu_sc as plsc

HEAD_DIM = 64
MOBA_HEADS = 8
MOBA_BLOCK = 256
MOBA_TOPK = 3
DIFF_HEADS = 4
PEER_HEADS = 8
PEER_N_KEYS = 128
PEER_TOPK = 16
RMS_EPS = 1e-6

LANES = 128
SUBLANES = 8
VMEM_LIMIT = 48 * 1024 * 1024

PROJ_TM = 512
MIX_TM = 256
PEER_TT = 64
PACK_ROWS = 512
ATT_QBLOCKS = 2
PEER_GROUP = 8
PEER_PLAN = ((1, True), (2, True), (2, True), (2, True), (2, True), (2, True), (2, True), (3, False))
SC_LANES = 16
SC_ROWS = 32
SC_TB = 8

NEG_INF = float("-inf")
_NT = (((1,), (1,)), ((), ()))


def _dot(a, b):
    return jnp.dot(a, b, preferred_element_type=jnp.float32)


def _dot_nt(a, b):
    return lax.dot_general(a, b, _NT, preferred_element_type=jnp.float32)


def _split_bf16(x):
    hi = x.astype(jnp.bfloat16)
    lo = (x - hi.astype(jnp.float32)).astype(jnp.bfloat16)
    return hi, lo


def _proj_kernel(x_ref, g1_ref, wqk_ref, wvt_ref, gqk_ref, gsum_ref,
                 mq_ref, mk_ref, dq_ref, dk_ref, mvt_ref, dvt_ref, kmean_ref):
    x = x_ref[...]
    ms = jnp.mean(x * x, axis=-1, keepdims=True)
    h = (x * lax.rsqrt(ms + RMS_EPS) * g1_ref[...]).astype(jnp.bfloat16)
    gsum = gsum_ref[...]
    outs = (mq_ref, mk_ref, dq_ref, dk_ref)
    for sec in range(4):
        for half in range(2):
            c0 = sec * 512 + half * 256
            p = _dot(h, wqk_ref[:, c0:c0 + 256])
            for s in range(2):
                ps = p[:, s * LANES:(s + 1) * LANES]
                sq_hi, sq_lo = _split_bf16(ps * ps)
                ssq = _dot(sq_hi, gsum) + _dot(sq_lo, gsum)
                g = gqk_ref[:, c0 + s * LANES:c0 + (s + 1) * LANES]
                y = ps * lax.rsqrt(ssq * (1.0 / HEAD_DIM) + RMS_EPS) * g
                l0 = half * 256 + s * LANES
                outs[sec][:, l0:l0 + LANES] = y.astype(jnp.bfloat16)
                if sec == 1:
                    for blk in range(PROJ_TM // MOBA_BLOCK):
                        kmean_ref[blk:blk + 1, l0:l0 + LANES] = jnp.mean(
                            y[blk * MOBA_BLOCK:(blk + 1) * MOBA_BLOCK], axis=0, keepdims=True)
    kmean_ref[PROJ_TM // MOBA_BLOCK:, :] = jnp.zeros((SUBLANES - PROJ_TM // MOBA_BLOCK, 512), jnp.float32)
    vt = _dot_nt(wvt_ref[...], h).astype(jnp.bfloat16)
    for blk in range(PROJ_TM // MOBA_BLOCK):
        mvt_ref[blk] = vt[0:512, blk * MOBA_BLOCK:(blk + 1) * MOBA_BLOCK]
        dvt_ref[blk] = vt[512:1024, blk * MOBA_BLOCK:(blk + 1) * MOBA_BLOCK]


def _project(x2, g1, wqk, wvt, gqk, gsum):
    T = x2.shape[0]
    nt = T // PROJ_TM
    nblk = PROJ_TM // MOBA_BLOCK
    row = lambda i: (i, 0)
    const = lambda i: (0, 0)
    qk_shape = jax.ShapeDtypeStruct((T, 512), jnp.bfloat16)
    vt_shape = jax.ShapeDtypeStruct((T // MOBA_BLOCK, 512, MOBA_BLOCK), jnp.bfloat16)
    return pl.pallas_call(
        _proj_kernel,
        grid=(nt,),
        in_specs=[
            pl.BlockSpec((PROJ_TM, 1024), row),
            pl.BlockSpec((1, 1024), const),
            pl.BlockSpec((1024, 2048), const),
            pl.BlockSpec((1024, 1024), const),
            pl.BlockSpec((1, 2048), const),
            pl.BlockSpec((LANES, LANES), const),
        ],
        out_specs=[
            pl.BlockSpec((PROJ_TM, 512), row),
            pl.BlockSpec((PROJ_TM, 512), row),
            pl.BlockSpec((PROJ_TM, 512), row),
            pl.BlockSpec((PROJ_TM, 512), row),
            pl.BlockSpec((nblk, 512, MOBA_BLOCK), lambda i: (i, 0, 0)),
            pl.BlockSpec((nblk, 512, MOBA_BLOCK), lambda i: (i, 0, 0)),
            pl.BlockSpec((SUBLANES, 512), row),
        ],
        out_shape=[qk_shape, qk_shape, qk_shape, qk_shape, vt_shape, vt_shape,
                   jax.ShapeDtypeStruct((nt * SUBLANES, 512), jnp.float32)],
        compiler_params=pltpu.CompilerParams(dimension_semantics=("arbitrary",), vmem_limit_bytes=VMEM_LIMIT),
        name="proj",
    )(x2, g1, wqk, wvt, gqk, gsum)


def _rel_qk():
    qc = lax.broadcasted_iota(jnp.int32, (MOBA_BLOCK, MOBA_BLOCK), 1)
    kr = lax.broadcasted_iota(jnp.int32, (MOBA_BLOCK, MOBA_BLOCK), 0)
    return qc - kr


def _attend(q_h, k_ref, vt_load, i, slope, rel, selbias):
    nb = rel.astype(jnp.float32) * (-slope)
    ts, rowbs, colmax = [], [], []
    for j in range(i + 1):
        t = _dot_nt(k_ref[j * MOBA_BLOCK:(j + 1) * MOBA_BLOCK, :], q_h) + nb
        if j == i:
            t = jnp.where(rel >= 0, t, NEG_INF)
            rowb = jnp.zeros((1, MOBA_BLOCK), jnp.float32)
        else:
            rowb = jnp.full((1, MOBA_BLOCK), (i - j) * MOBA_BLOCK, jnp.float32) * (-slope)
            if selbias is not None:
                rowb = rowb + selbias[j:j + 1, :]
        ts.append(t)
        rowbs.append(rowb)
        colmax.append(jnp.max(t, axis=0, keepdims=True) + rowb)
    m = functools.reduce(jnp.maximum, colmax)
    l, acc = None, None
    for j in range(i + 1):
        p = jnp.exp(ts[j] - (m - rowbs[j]))
        lj = jnp.sum(p, axis=0, keepdims=True)
        aj = _dot(vt_load(j), p.astype(jnp.bfloat16))
        l = lj if l is None else l + lj
        acc = aj if acc is None else acc + aj
    return acc / l


def _head_queries(q2):
    lane = lax.broadcasted_iota(jnp.int32, q2.shape, 1)
    return [jnp.where((lane >= hh * HEAD_DIM) & (lane < (hh + 1) * HEAD_DIM), q2, jnp.zeros_like(q2))
            for hh in range(2)]


def _block_selection(gate, i):
    blk = lax.broadcasted_iota(jnp.int32, gate.shape, 0)
    cnt = jnp.zeros(gate.shape, jnp.int32)
    for mblk in range(i):
        gm = gate[mblk:mblk + 1, :]
        beats = (gm > gate) | ((gm == gate) & (mblk < blk))
        cnt = cnt + jnp.where(beats, 1, 0)
    return jnp.where((blk < i) & (cnt < MOBA_TOPK), 0.0, NEG_INF)


def _moba_kernel(slopes_ref, q_ref, k_ref, vt_ref, kmean_ref, o_ref):
    pair = pl.program_id(1)
    step = pl.program_id(2)
    nq = vt_ref.shape[0]
    rel = _rel_qk()
    km_hi, km_lo = _split_bf16(kmean_ref[...])
    for st in range(nq // ATT_QBLOCKS):
        @pl.when(step == st)
        def _(st=st):
            for u in range(ATT_QBLOCKS):
                i = st * ATT_QBLOCKS + u
                rows = slice(u * MOBA_BLOCK, (u + 1) * MOBA_BLOCK)
                outs = []
                for hh, q_h in enumerate(_head_queries(q_ref[rows, :])):
                    gate = _dot_nt(km_hi, q_h) + _dot_nt(km_lo, q_h)
                    vt_load = lambda j, hh=hh: vt_ref[j, hh * HEAD_DIM:(hh + 1) * HEAD_DIM, :]
                    outs.append(_attend(q_h, k_ref, vt_load, i, slopes_ref[pair, hh], rel,
                                        _block_selection(gate, i)))
                o_t = jnp.concatenate(outs, axis=0)
                o_ref[rows, :] = o_t.T.astype(o_ref.dtype)


def _diff_kernel(slopes_ref, lam_ref, q_ref, k_ref, vt_ref, gsub_ref, o_ref, *, lam_init):
    head = pl.program_id(1)
    step = pl.program_id(2)
    nq = vt_ref.shape[0]
    rel = _rel_qk()
    slope = slopes_ref[head, 0]
    lp = lam_ref[...]
    lam = (jnp.exp(jnp.sum(lp[0:1] * lp[1:2], axis=-1, keepdims=True))
           - jnp.exp(jnp.sum(lp[2:3] * lp[3:4], axis=-1, keepdims=True)) + lam_init)
    vt_load = lambda j: vt_ref[j]
    for st in range(nq // ATT_QBLOCKS):
        @pl.when(step == st)
        def _(st=st):
            for u in range(ATT_QBLOCKS):
                i = st * ATT_QBLOCKS + u
                rows = slice(u * MOBA_BLOCK, (u + 1) * MOBA_BLOCK)
                parts = [_attend(q_h, k_ref, vt_load, i, slope, rel, None)
                         for q_h in _head_queries(q_ref[rows, :])]
                o_t = parts[0] - lam * parts[1]
                ms = jnp.mean(o_t * o_t, axis=0, keepdims=True)
                o_t = o_t * lax.rsqrt(ms + RMS_EPS) * gsub_ref[...] * (1.0 - lam_init)
                o_ref[rows, :] = o_t.T.astype(o_ref.dtype)


def _attention_specs(nq):
    nst = nq // ATT_QBLOCKS
    q_spec = pl.BlockSpec((ATT_QBLOCKS * MOBA_BLOCK, LANES), lambda b, p, i: (b * nst + i, p))
    k_spec = pl.BlockSpec((nq * MOBA_BLOCK, LANES), lambda b, p, i: (b, p))
    vt_spec = pl.BlockSpec((nq, LANES, MOBA_BLOCK), lambda b, p, i: (b, p, 0))
    return q_spec, k_spec, vt_spec


def _moba_attention(q, k, vt, kmean, slopes, batch, seq):
    nq = seq // MOBA_BLOCK
    q_spec, k_spec, vt_spec = _attention_specs(nq)
    return pl.pallas_call(
        _moba_kernel,
        grid=(batch, MOBA_HEADS // 2, nq // ATT_QBLOCKS),
        in_specs=[
            pl.BlockSpec(memory_space=pltpu.SMEM),
            q_spec, k_spec, vt_spec,
            pl.BlockSpec((SUBLANES, LANES), lambda b, p, i: (b, p)),
        ],
        out_specs=q_spec,
        out_shape=jax.ShapeDtypeStruct(q.shape, jnp.bfloat16),
        compiler_params=pltpu.CompilerParams(
            dimension_semantics=("arbitrary", "arbitrary", "arbitrary"), vmem_limit_bytes=VMEM_LIMIT),
        name="moba",
    )(slopes, q, k, vt, kmean)


def _diff_attention(q, k, vt, lam_p, gsub, slopes, lam_init, batch, seq):
    nq = seq // MOBA_BLOCK
    q_spec, k_spec, vt_spec = _attention_specs(nq)
    return pl.pallas_call(
        functools.partial(_diff_kernel, lam_init=lam_init),
        grid=(batch, DIFF_HEADS, nq // ATT_QBLOCKS),
        in_specs=[
            pl.BlockSpec(memory_space=pltpu.SMEM),
            pl.BlockSpec((4, HEAD_DIM), lambda b, p, i: (0, 0)),
            q_spec, k_spec, vt_spec,
            pl.BlockSpec((LANES, MOBA_BLOCK), lambda b, p, i: (0, 0)),
        ],
        out_specs=q_spec,
        out_shape=jax.ShapeDtypeStruct(q.shape, jnp.bfloat16),
        compiler_params=pltpu.CompilerParams(
            dimension_semantics=("arbitrary", "arbitrary", "arbitrary"), vmem_limit_bytes=VMEM_LIMIT),
        name="diff",
    )(slopes, lam_p, q, k, vt, gsub)


def _knockout_topk(sc, k, payload=None):
    n = sc.shape[0]
    row = lax.broadcasted_iota(jnp.int32, sc.shape, 0).astype(jnp.float32)
    vals, picks = [], []
    for _ in range(k):
        m = jnp.max(sc, axis=0, keepdims=True)
        idx = jnp.min(jnp.where(sc == m, row, float(n)), axis=0, keepdims=True)
        hit = row == idx
        vals.append(m)
        picks.append(idx if payload is None else jnp.sum(jnp.where(hit, payload, 0.0), axis=0, keepdims=True))
        sc = jnp.where(hit, NEG_INF, sc)
    return vals, picks


def _mix_kernel(x_ref, am_ref, ad_ref, wo_ref, g2_ref, wq_ref, keys_ref,
                x1_ref, xn_ref, exp_ref, gate_ref):
    x1 = x_ref[...] + _dot(am_ref[...], wo_ref[0:512, :]) + _dot(ad_ref[...], wo_ref[512:1024, :])
    x1_ref[...] = x1
    ms = jnp.mean(x1 * x1, axis=-1, keepdims=True)
    xn = x1 * lax.rsqrt(ms + RMS_EPS) * g2_ref[...]
    xn_ref[...] = xn
    qq = _dot(xn.astype(jnp.bfloat16), wq_ref[...]).astype(jnp.bfloat16)
    tm = qq.shape[0]
    exp_rows, gate_rows = [], []
    for h in range(PEER_HEADS):
        tops = []
        for half in range(2):
            s = h * 2 + half
            sc = _dot_nt(keys_ref[s], qq[:, s * LANES:(s + 1) * LANES])
            tops.append(_knockout_topk(sc, PEER_TOPK))
        (s1, i1), (s2, i2) = tops
        s2a = jnp.concatenate(s2, axis=0)
        i2a = jnp.concatenate(i2, axis=0)
        cand, cidx = [], []
        for a in range(PEER_TOPK):
            nb = PEER_TOPK // (a + 1)
            cand.append(s1[a] + s2a[0:nb])
            cidx.append(i1[a] * float(PEER_N_KEYS) + i2a[0:nb])
        n_cand = sum(c.shape[0] for c in cand)
        pad = -n_cand % SUBLANES
        cand.append(jnp.full((pad, tm), NEG_INF, jnp.float32))
        cidx.append(jnp.zeros((pad, tm), jnp.float32))
        best, experts = _knockout_topk(jnp.concatenate(cand, axis=0), PEER_TOPK,
                                       payload=jnp.concatenate(cidx, axis=0))
        b = jnp.concatenate(best, axis=0)
        e = jnp.exp(b - b[0:1])
        gate_rows.append(e / jnp.sum(e, axis=0, keepdims=True))
        exp_rows.append(jnp.concatenate(experts, axis=0))
    exp_ref[...] = jnp.concatenate(exp_rows, axis=0).T.astype(jnp.int32)
    gate_ref[...] = jnp.concatenate(gate_rows, axis=0).T


def _mix(x2, am, ad, wo, g2, wq, keys):
    T = x2.shape[0]
    row = lambda i: (i, 0)
    const = lambda i: (0, 0)
    return pl.pallas_call(
        _mix_kernel,
        grid=(T // MIX_TM,),
        in_specs=[
            pl.BlockSpec((MIX_TM, 1024), row),
            pl.BlockSpec((MIX_TM, 512), row),
            pl.BlockSpec((MIX_TM, 512), row),
            pl.BlockSpec((1024, 1024), const),
            pl.BlockSpec((1, 1024), const),
            pl.BlockSpec((1024, 2048), const),
            pl.BlockSpec((2 * PEER_HEADS, PEER_N_KEYS, LANES), lambda i: (0, 0, 0)),
        ],
        out_specs=[
            pl.BlockSpec((MIX_TM, 1024), row),
            pl.BlockSpec((MIX_TM, 1024), row),
            pl.BlockSpec((MIX_TM, LANES), row),
            pl.BlockSpec((MIX_TM, LANES), row),
        ],
        out_shape=[
            jax.ShapeDtypeStruct((T, 1024), jnp.float32),
            jax.ShapeDtypeStruct((T, 1024), jnp.float32),
            jax.ShapeDtypeStruct((T, LANES), jnp.int32),
            jax.ShapeDtypeStruct((T, LANES), jnp.float32),
        ],
        compiler_params=pltpu.CompilerParams(dimension_semantics=("arbitrary",), vmem_limit_bytes=VMEM_LIMIT),
        name="mix",
    )(x2, am, ad, wo, g2, wq, keys)


def _pack_kernel(t_ref, o_ref):
    t = t_ref[...]
    lo = pltpu.bitcast(t[:, 0:512].astype(jnp.bfloat16).astype(jnp.float32), jnp.int32)
    hi = pltpu.bitcast(t[:, 512:1024].astype(jnp.bfloat16).astype(jnp.float32), jnp.int32)
    o_ref[...] = lax.shift_right_logical(lo, 16) | (hi & jnp.int32(-65536))


def _pack_table(tab):
    n = tab.shape[0]
    return pl.pallas_call(
        _pack_kernel,
        grid=(n // PACK_ROWS,),
        in_specs=[pl.BlockSpec((PACK_ROWS, 1024), lambda i: (i, 0))],
        out_specs=pl.BlockSpec((PACK_ROWS, 512), lambda i: (i, 0)),
        out_shape=jax.ShapeDtypeStruct((n, 512), jnp.int32),
        compiler_params=pltpu.CompilerParams(dimension_semantics=("arbitrary",)),
        name="pack",
    )(tab)


def _decode(w):
    lo = pltpu.bitcast(w << 16, jnp.float32)
    hi = pltpu.bitcast(w & jnp.int32(-65536), jnp.float32)
    return lo, hi


def _peer_u_kernel(exp_ref, tab_ref, x_ref, g_ref, act_ref, q0_ref, q1_ref, qs_ref):
    ntok, nsel = exp_ref.shape
    ones = jnp.ones((SUBLANES, LANES), jnp.bfloat16)

    def products(t, q_ref):
        x = x_ref[t]
        xlo = x[0:4]
        xhi = x[4:8]
        for e in range(nsel):
            lo, hi = _decode(tab_ref[exp_ref[t, e]])
            q_ref[pl.ds(4 * e, 4), :] = lo * xlo + hi * xhi

    def fold(t, q_ref):
        q = q_ref[pl.ds(0, nsel, stride=4), :]
        for s in range(1, 4):
            q = q + q_ref[pl.ds(s, nsel, stride=4), :]
        qs_ref[pl.ds(pl.multiple_of(t * nsel, nsel), nsel), :] = q

    def gather_group(g):
        for k in range(PEER_GROUP):
            q_ref = (q0_ref, q1_ref)[k % 2]
            products(g * PEER_GROUP + k, q_ref)
            fold(g * PEER_GROUP + k, q_ref)

    def finish_group(g):
        rows8 = PEER_GROUP * nsel
        q_hi, q_lo = _split_bf16(qs_ref[pl.ds(pl.multiple_of(g * rows8, rows8), rows8), :])
        row = (_dot_nt(ones, q_hi) + _dot_nt(ones, q_lo))[0:1]
        h = jnp.concatenate([row[:, k * nsel:(k + 1) * nsel] for k in range(PEER_GROUP)], axis=0)
        gelu = 0.5 * h * (1.0 + lax.erf(h * (1.0 / math.sqrt(2.0))))
        rows = pl.ds(pl.multiple_of(g * PEER_GROUP, PEER_GROUP), PEER_GROUP)
        act_ref[rows, :] = gelu * g_ref[rows, :]

    ngroups = ntok // PEER_GROUP
    gather_group(0)

    def body(g, carry):
        finish_group(g - 1)
        gather_group(g)
        return carry

    lax.fori_loop(1, ngroups, body, 0)
    finish_group(ngroups - 1)


def _sc_peer_v(vtab, idx, actrep, x1, tok0, ntok):
    d = x1.shape[1]
    nsel = idx.shape[1]
    info = plsc.get_sparse_core_info()
    nw = info.num_cores * info.num_subcores
    per_w = ntok // nw
    nchunk = nsel // SC_ROWS
    nitem = SC_TB * nchunk
    assert info.num_lanes == SC_LANES and ntok % (nw * SC_TB) == 0 and nitem % 2 == 0
    mesh = plsc.VectorSubcoreMesh(core_axis_name="c", subcore_axis_name="s")

    @functools.partial(
        pl.kernel, mesh=mesh,
        out_type=jax.ShapeDtypeStruct((ntok, d), jnp.float32),
        scratch_types=[
            pltpu.VMEM((SC_TB, nsel), jnp.int32),
            pltpu.VMEM((SC_TB, nsel * SC_LANES), jnp.float32),
            pltpu.VMEM((SC_ROWS, d), jnp.float32),
            pltpu.VMEM((SC_ROWS, d), jnp.float32),
            pltpu.VMEM((SC_TB, d), jnp.float32),
            pltpu.SemaphoreType.DMA,
            pltpu.SemaphoreType.DMA,
        ],
        name="sc_peer_v",
    )
    def k(vtab_hbm, idx_hbm, act_hbm, x1_hbm, out_hbm, idx_v, act_v, rows0, rows1, acc_v, sem0, sem1):
        wid = lax.axis_index("s") * info.num_cores + lax.axis_index("c")

        def gather(item, rows, sem):
            tl = item // nchunk
            ch = item % nchunk
            return pltpu.make_async_copy(vtab_hbm.at[idx_v.at[tl, pl.ds(ch * SC_ROWS, SC_ROWS)]], rows, sem)

        def accumulate(item, rows):
            tl = item // nchunk
            r0 = (item % nchunk) * SC_ROWS
            a = [act_v[tl, pl.ds((r0 + e) * SC_LANES, SC_LANES)] for e in range(SC_ROWS)]

            @plsc.parallel_loop(0, d // SC_LANES, 1, unroll=2)
            def _(c):
                col = pl.ds(c * SC_LANES, SC_LANES)
                terms = [a[e] * rows[e, col] for e in range(SC_ROWS)]
                while len(terms) > 1:
                    terms = [terms[j] + terms[j + 1] for j in range(0, len(terms), 2)]
                acc_v[tl, col] = acc_v[tl, col] + terms[0]

        @pl.loop(0, per_w // SC_TB)
        def _(bi):
            loc = pl.multiple_of(wid * per_w + bi * SC_TB, SC_TB)
            src = pl.ds(tok0 + loc, SC_TB)
            pltpu.sync_copy(idx_hbm.at[src], idx_v)
            pltpu.sync_copy(act_hbm.at[pl.ds(loc, SC_TB)], act_v)
            pltpu.sync_copy(x1_hbm.at[src], acc_v)
            gather(0, rows0, sem0).start()

            @pl.loop(0, nitem // 2)
            def _(p):
                gather(2 * p + 1, rows1, sem1).start()
                gather(2 * p, rows0, sem0).wait()
                accumulate(2 * p, rows0)

                @pl.when(p < nitem // 2 - 1)
                def _():
                    gather(2 * p + 2, rows0, sem0).start()

                gather(2 * p + 1, rows1, sem1).wait()
                accumulate(2 * p + 1, rows1)

            pltpu.sync_copy(acc_v, out_hbm.at[pl.ds(loc, SC_TB)])

    return k(vtab, idx, actrep, x1)


def _peer_v_kernel(exp_ref, act_ref, tab_ref, x_ref, o_ref):
    nsel = exp_ref.shape[1]

    def tok(t, carry):
        acc_lo = jnp.zeros((4, LANES), jnp.float32)
        acc_hi = jnp.zeros((4, LANES), jnp.float32)
        for e in range(nsel):
            a = act_ref[t, e]
            lo, hi = _decode(tab_ref[exp_ref[t, e]])
            acc_lo = acc_lo + a * lo
            acc_hi = acc_hi + a * hi
        o_ref[t] = x_ref[t] + jnp.concatenate([acc_lo, acc_hi], axis=0)
        return carry

    lax.fori_loop(0, x_ref.shape[0], tok, 0)


def _peer_experts(xn3, x1, experts, gates, u_packed, v_table, v_packed):
    T, nsel = experts.shape
    n_exp = u_packed.shape[0]
    d = x1.shape[1]
    unit = T // sum(u for u, _ in PEER_PLAN)
    assert nsel == LANES and PEER_TT % PEER_GROUP == 0 and unit % PEER_TT == 0
    tab_spec = pl.BlockSpec((n_exp, 4, LANES), lambda i: (0, 0, 0), pipeline_mode=pl.Buffered(1))
    cp = pltpu.CompilerParams(dimension_semantics=("arbitrary",), vmem_limit_bytes=VMEM_LIMIT)
    x13 = x1.reshape(T, SUBLANES, LANES)
    outs = []
    tok0 = 0
    for units, on_sc in PEER_PLAN:
        tc = units * unit
        steps = tc // PEER_TT
        off = tok0 // PEER_TT
        row = lambda i, off=off: (i + off, 0)
        row3 = lambda i, off=off: (i + off, 0, 0)
        act = pl.pallas_call(
            _peer_u_kernel,
            grid=(steps,),
            in_specs=[pl.BlockSpec((PEER_TT, nsel), row, memory_space=pltpu.SMEM), tab_spec,
                      pl.BlockSpec((PEER_TT, SUBLANES, LANES), row3), pl.BlockSpec((PEER_TT, nsel), row)],
            out_specs=pl.BlockSpec((PEER_TT, nsel), lambda i: (i, 0)),
            out_shape=jax.ShapeDtypeStruct((tc, nsel), jnp.float32),
            scratch_shapes=[pltpu.VMEM((4 * nsel, LANES), jnp.float32), pltpu.VMEM((4 * nsel, LANES), jnp.float32),
                            pltpu.VMEM((PEER_TT * nsel, LANES), jnp.float32)],
            compiler_params=cp,
            name="peer_u",
        )(experts, u_packed, xn3, gates)
        if on_sc:
            actrep = jnp.broadcast_to(act[:, :, None], (tc, nsel, SC_LANES)).reshape(tc, nsel * SC_LANES)
            outs.append(_sc_peer_v(v_table, experts, actrep, x1, tok0, tc))
        else:
            out3 = pl.pallas_call(
                _peer_v_kernel,
                grid=(steps,),
                in_specs=[pl.BlockSpec((PEER_TT, nsel), row, memory_space=pltpu.SMEM),
                          pl.BlockSpec((PEER_TT, nsel), lambda i: (i, 0), memory_space=pltpu.SMEM),
                          tab_spec, pl.BlockSpec((PEER_TT, SUBLANES, LANES), row3)],
                out_specs=pl.BlockSpec((PEER_TT, SUBLANES, LANES), lambda i: (i, 0, 0)),
                out_shape=jax.ShapeDtypeStruct((tc, SUBLANES, LANES), jnp.float32),
                compiler_params=cp,
                name="peer_v",
            )(experts, act, v_packed, x13)
            outs.append(out3.reshape(tc, d))
        tok0 += tc
    return jnp.concatenate(outs, axis=0)


def _alibi_slopes(n):
    return 2.0 ** (-8.0 * np.arange(1, n + 1) / n)


def _lambda_init(layer):
    return 0.8 - 0.6 * math.exp(-0.3 * layer)


def kernel(x, norm1_g, w_in, moba_qk_norm_g, diff_qk_norm_g, diff_lambda, diff_subln_g, w_out, norm2_g,
           peer_w_query, peer_sub_keys, peer_u, peer_v):
    b, s, d = x.shape
    T = b * s
    depth = w_in.shape[0]
    mw = MOBA_HEADS * HEAD_DIM
    assert d == 1024 and s % PROJ_TM == 0 and mw == 512 and DIFF_HEADS * 2 * HEAD_DIM == 512
    f32, bf16 = jnp.float32, jnp.bfloat16
    scale = HEAD_DIM ** -0.5
    grp = lax.broadcasted_iota(jnp.int32, (LANES, LANES), 0) // HEAD_DIM
    gsum = (grp == grp.T).astype(bf16)
    moba_slopes = jnp.asarray(_alibi_slopes(MOBA_HEADS).reshape(MOBA_HEADS // 2, 2), f32)
    diff_slopes = jnp.asarray(np.repeat(_alibi_slopes(DIFF_HEADS)[:, None], 2, axis=1), f32)
    nblk = PROJ_TM // MOBA_BLOCK
    x2 = x.reshape(T, d)
    for l in range(depth):
        w = w_in[l]
        wqk = jnp.concatenate([w[:, 0:2 * mw], w[:, 3 * mw:3 * mw + 1024]], axis=1).astype(bf16)
        wvt = jnp.concatenate([w[:, 2 * mw:3 * mw], w[:, 3 * mw + 1024:]], axis=1).T.astype(bf16)
        mg, dg = moba_qk_norm_g[l].astype(f32), diff_qk_norm_g[l].astype(f32)
        gqk = jnp.concatenate([jnp.tile(mg[0] * scale, 8), jnp.tile(mg[1], 8),
                               jnp.tile(dg[0] * scale, 8), jnp.tile(dg[1], 8)])[None, :]
        g1 = norm1_g[l][None, :].astype(f32)
        g2 = norm2_g[l][None, :].astype(f32)
        gsub = jnp.broadcast_to(diff_subln_g[l].astype(f32)[:, None], (LANES, MOBA_BLOCK))
        lam_p = diff_lambda[l].astype(f32)
        wo, wq = w_out[l].astype(bf16), peer_w_query[l].astype(bf16)
        keys = peer_sub_keys[l].reshape(2 * PEER_HEADS, PEER_N_KEYS, LANES).astype(bf16)
        u_packed = _pack_table(peer_u[l]).reshape(peer_u.shape[1], 4, LANES)
        v_packed = _pack_table(peer_v[l]).reshape(peer_v.shape[1], 4, LANES)
        v_table = peer_v[l].astype(f32)
        mq, mk, dq, dk, mvt, dvt, kmean = _project(x2, g1, wqk, wvt, gqk, gsum)
        kmean = kmean.reshape(T // PROJ_TM, SUBLANES, 512)[:, :nblk].reshape(T // MOBA_BLOCK, 512)
        moba_o = _moba_attention(mq, mk, mvt, kmean, moba_slopes, b, s)
        diff_o = _diff_attention(dq, dk, dvt, lam_p, gsub, diff_slopes, _lambda_init(l), b, s)
        x1, xn, experts, gates = _mix(x2, moba_o, diff_o, wo, g2, wq, keys)
        x2 = _peer_experts(xn.reshape(T, SUBLANES, LANES), x1, experts, gates, u_packed, v_table, v_packed)
    return x2.reshape(b, s, d)
```

```python
import functools
import math

import jax
import jax.numpy as jnp
import numpy as np
from jax import lax
from jax.experimental import pallas as pl
from jax.experimental.pallas import tpu as pltpu
from jax.experimental.pallas import tpu_sc as plsc

HEAD_DIM = 64
MOBA_HEADS = 8
MOBA_BLOCK = 256
MOBA_TOPK = 3
DIFF_HEADS = 4
PEER_HEADS = 8
PEER_N_KEYS = 128
PEER_TOPK = 16
RMS_EPS = 1e-6

LANES = 128
SUBLANES = 8
VMEM_LIMIT = 48 * 1024 * 1024

PROJ_TM = 512
MIX_TM = 256
PEER_TT = 64
PACK_ROWS = 512
ATT_QBLOCKS = 2
PEER_GROUP = 8
PIPE_GROUPS = 4
GROUP_PLAN = ((1, True), (1, True))
LAST_PLAN = ((1, True), (1, True), (1, True), (1, False))
SC_LANES = 16
SC_ROWS = 32
SC_TB = 8

NEG_INF = float("-inf")
_NT = (((1,), (1,)), ((), ()))


def _dot(a, b):
    return jnp.dot(a, b, preferred_element_type=jnp.float32)


def _dot_nt(a, b):
    return lax.dot_general(a, b, _NT, preferred_element_type=jnp.float32)


def _split_bf16(x):
    hi = x.astype(jnp.bfloat16)
    lo = (x - hi.astype(jnp.float32)).astype(jnp.bfloat16)
    return hi, lo


def _proj_kernel(x_ref, g1_ref, wqk_ref, wvt_ref, gqk_ref, gsum_ref, after_ref,
                 mq_ref, mk_ref, dq_ref, dk_ref, mvt_ref, dvt_ref, kmean_ref):
    x = x_ref[...]
    ms = jnp.mean(x * x, axis=-1, keepdims=True)
    h = (x * lax.rsqrt(ms + RMS_EPS) * g1_ref[...]).astype(jnp.bfloat16)
    gsum = gsum_ref[...]
    outs = (mq_ref, mk_ref, dq_ref, dk_ref)
    for sec in range(4):
        for half in range(2):
            c0 = sec * 512 + half * 256
            p = _dot(h, wqk_ref[:, c0:c0 + 256])
            for s in range(2):
                ps = p[:, s * LANES:(s + 1) * LANES]
                sq_hi, sq_lo = _split_bf16(ps * ps)
                ssq = _dot(sq_hi, gsum) + _dot(sq_lo, gsum)
                g = gqk_ref[:, c0 + s * LANES:c0 + (s + 1) * LANES]
                y = ps * lax.rsqrt(ssq * (1.0 / HEAD_DIM) + RMS_EPS) * g
                l0 = half * 256 + s * LANES
                outs[sec][:, l0:l0 + LANES] = y.astype(jnp.bfloat16)
                if sec == 1:
                    for blk in range(PROJ_TM // MOBA_BLOCK):
                        kmean_ref[blk:blk + 1, l0:l0 + LANES] = jnp.mean(
                            y[blk * MOBA_BLOCK:(blk + 1) * MOBA_BLOCK], axis=0, keepdims=True)
    pad = SUBLANES - PROJ_TM // MOBA_BLOCK
    kmean_ref[PROJ_TM // MOBA_BLOCK:, :] = jnp.concatenate([after_ref[0:pad, :]] * (512 // LANES), axis=1)
    vt = _dot_nt(wvt_ref[...], h).astype(jnp.bfloat16)
    for blk in range(PROJ_TM // MOBA_BLOCK):
        mvt_ref[blk] = vt[0:512, blk * MOBA_BLOCK:(blk + 1) * MOBA_BLOCK]
        dvt_ref[blk] = vt[512:1024, blk * MOBA_BLOCK:(blk + 1) * MOBA_BLOCK]


def _project(x2, g1, wqk, wvt, gqk, gsum, after):
    T = x2.shape[0]
    nt = T // PROJ_TM
    nblk = PROJ_TM // MOBA_BLOCK
    row = lambda i: (i, 0)
    const = lambda i: (0, 0)
    qk_shape = jax.ShapeDtypeStruct((T, 512), jnp.bfloat16)
    vt_shape = jax.ShapeDtypeStruct((T // MOBA_BLOCK, 512, MOBA_BLOCK), jnp.bfloat16)
    return pl.pallas_call(
        _proj_kernel,
        grid=(nt,),
        in_specs=[
            pl.BlockSpec((PROJ_TM, 1024), row),
            pl.BlockSpec((1, 1024), const),
            pl.BlockSpec((1024, 2048), const),
            pl.BlockSpec((1024, 1024), const),
            pl.BlockSpec((1, 2048), const),
            pl.BlockSpec((LANES, LANES), const),
            pl.BlockSpec((SUBLANES, LANES), const),
        ],
        out_specs=[
            pl.BlockSpec((PROJ_TM, 512), row),
            pl.BlockSpec((PROJ_TM, 512), row),
            pl.BlockSpec((PROJ_TM, 512), row),
            pl.BlockSpec((PROJ_TM, 512), row),
            pl.BlockSpec((nblk, 512, MOBA_BLOCK), lambda i: (i, 0, 0)),
            pl.BlockSpec((nblk, 512, MOBA_BLOCK), lambda i: (i, 0, 0)),
            pl.BlockSpec((SUBLANES, 512), row),
        ],
        out_shape=[qk_shape, qk_shape, qk_shape, qk_shape, vt_shape, vt_shape,
                   jax.ShapeDtypeStruct((nt * SUBLANES, 512), jnp.float32)],
        compiler_params=pltpu.CompilerParams(dimension_semantics=("arbitrary",), vmem_limit_bytes=VMEM_LIMIT),
        name="proj",
    )(x2, g1, wqk, wvt, gqk, gsum, after)


def _rel_qk():
    qc = lax.broadcasted_iota(jnp.int32, (MOBA_BLOCK, MOBA_BLOCK), 1)
    kr = lax.broadcasted_iota(jnp.int32, (MOBA_BLOCK, MOBA_BLOCK), 0)
    return qc - kr


def _attend(q_h, k_ref, vt_load, i, slope, rel, selbias):
    nb = rel.astype(jnp.float32) * (-slope)
    ts, rowbs, colmax = [], [], []
    for j in range(i + 1):
        t = _dot_nt(k_ref[j * MOBA_BLOCK:(j + 1) * MOBA_BLOCK, :], q_h) + nb
        if j == i:
            t = jnp.where(rel >= 0, t, NEG_INF)
            rowb = jnp.zeros((1, MOBA_BLOCK), jnp.float32)
        else:
            rowb = jnp.full((1, MOBA_BLOCK), (i - j) * MOBA_BLOCK, jnp.float32) * (-slope)
            if selbias is not None:
                rowb = rowb + selbias[j:j + 1, :]
        ts.append(t)
        rowbs.append(rowb)
        colmax.append(jnp.max(t, axis=0, keepdims=True) + rowb)
    m = functools.reduce(jnp.maximum, colmax)
    l, acc = None, None
    for j in range(i + 1):
        p = jnp.exp(ts[j] - (m - rowbs[j]))
        lj = jnp.sum(p, axis=0, keepdims=True)
        aj = _dot(vt_load(j), p.astype(jnp.bfloat16))
        l = lj if l is None else l + lj
        acc = aj if acc is None else acc + aj
    return acc / l


def _head_queries(q2):
    lane = lax.broadcasted_iota(jnp.int32, q2.shape, 1)
    return [jnp.where((lane >= hh * HEAD_DIM) & (lane < (hh + 1) * HEAD_DIM), q2, jnp.zeros_like(q2))
            for hh in range(2)]


def _block_selection(gate, i):
    blk = lax.broadcasted_iota(jnp.int32, gate.shape, 0)
    cnt = jnp.zeros(gate.shape, jnp.int32)
    for mblk in range(i):
        gm = gate[mblk:mblk + 1, :]
        beats = (gm > gate) | ((gm == gate) & (mblk < blk))
        cnt = cnt + jnp.where(beats, 1, 0)
    return jnp.where((blk < i) & (cnt < MOBA_TOPK), 0.0, NEG_INF)


def _moba_kernel(slopes_ref, q_ref, k_ref, vt_ref, kmean_ref, o_ref):
    pair = pl.program_id(1)
    step = pl.program_id(2)
    nq = vt_ref.shape[0]
    rel = _rel_qk()
    km_hi, km_lo = _split_bf16(kmean_ref[...])
    for st in range(nq // ATT_QBLOCKS):
        @pl.when(step == st)
        def _(st=st):
            for u in range(ATT_QBLOCKS):
                i = st * ATT_QBLOCKS + u
                rows = slice(u * MOBA_BLOCK, (u + 1) * MOBA_BLOCK)
                outs = []
                for hh, q_h in enumerate(_head_queries(q_ref[rows, :])):
                    gate = _dot_nt(km_hi, q_h) + _dot_nt(km_lo, q_h)
                    vt_load = lambda j, hh=hh: vt_ref[j, hh * HEAD_DIM:(hh + 1) * HEAD_DIM, :]
                    outs.append(_attend(q_h, k_ref, vt_load, i, slopes_ref[pair, hh], rel,
                                        _block_selection(gate, i)))
                o_t = jnp.concatenate(outs, axis=0)
                o_ref[rows, :] = o_t.T.astype(o_ref.dtype)


def _diff_kernel(slopes_ref, lam_ref, q_ref, k_ref, vt_ref, gsub_ref, o_ref, *, lam_init):
    head = pl.program_id(1)
    step = pl.program_id(2)
    nq = vt_ref.shape[0]
    rel = _rel_qk()
    slope = slopes_ref[head, 0]
    lp = lam_ref[...]
    lam = (jnp.exp(jnp.sum(lp[0:1] * lp[1:2], axis=-1, keepdims=True))
           - jnp.exp(jnp.sum(lp[2:3] * lp[3:4], axis=-1, keepdims=True)) + lam_init)
    vt_load = lambda j: vt_ref[j]
    for st in range(nq // ATT_QBLOCKS):
        @pl.when(step == st)
        def _(st=st):
            for u in range(ATT_QBLOCKS):
                i = st * ATT_QBLOCKS + u
                rows = slice(u * MOBA_BLOCK, (u + 1) * MOBA_BLOCK)
                parts = [_attend(q_h, k_ref, vt_load, i, slope, rel, None)
                         for q_h in _head_queries(q_ref[rows, :])]
                o_t = parts[0] - lam * parts[1]
                ms = jnp.mean(o_t * o_t, axis=0, keepdims=True)
                o_t = o_t * lax.rsqrt(ms + RMS_EPS) * gsub_ref[...] * (1.0 - lam_init)
                o_ref[rows, :] = o_t.T.astype(o_ref.dtype)


def _attention_specs(nq):
    nst = nq // ATT_QBLOCKS
    q_spec = pl.BlockSpec((ATT_QBLOCKS * MOBA_BLOCK, LANES), lambda b, p, i: (b * nst + i, p))
    k_spec = pl.BlockSpec((nq * MOBA_BLOCK, LANES), lambda b, p, i: (b, p))
    vt_spec = pl.BlockSpec((nq, LANES, MOBA_BLOCK), lambda b, p, i: (b, p, 0))
    return q_spec, k_spec, vt_spec


def _moba_attention(q, k, vt, kmean, slopes, batch, seq):
    nq = seq // MOBA_BLOCK
    q_spec, k_spec, vt_spec = _attention_specs(nq)
    return pl.pallas_call(
        _moba_kernel,
        grid=(batch, MOBA_HEADS // 2, nq // ATT_QBLOCKS),
        in_specs=[
            pl.BlockSpec(memory_space=pltpu.SMEM),
            q_spec, k_spec, vt_spec,
            pl.BlockSpec((SUBLANES, LANES), lambda b, p, i: (b, p)),
        ],
        out_specs=q_spec,
        out_shape=jax.ShapeDtypeStruct(q.shape, jnp.bfloat16),
        compiler_params=pltpu.CompilerParams(
            dimension_semantics=("arbitrary", "arbitrary", "arbitrary"), vmem_limit_bytes=VMEM_LIMIT),
        name="moba",
    )(slopes, q, k, vt, kmean)


def _diff_attention(q, k, vt, lam_p, gsub, slopes, lam_init, batch, seq):
    nq = seq // MOBA_BLOCK
    q_spec, k_spec, vt_spec = _attention_specs(nq)
    return pl.pallas_call(
        functools.partial(_diff_kernel, lam_init=lam_init),
        grid=(batch, DIFF_HEADS, nq // ATT_QBLOCKS),
        in_specs=[
            pl.BlockSpec(memory_space=pltpu.SMEM),
            pl.BlockSpec((4, HEAD_DIM), lambda b, p, i: (0, 0)),
            q_spec, k_spec, vt_spec,
            pl.BlockSpec((LANES, MOBA_BLOCK), lambda b, p, i: (0, 0)),
        ],
        out_specs=q_spec,
        out_shape=jax.ShapeDtypeStruct(q.shape, jnp.bfloat16),
        compiler_params=pltpu.CompilerParams(
            dimension_semantics=("arbitrary", "arbitrary", "arbitrary"), vmem_limit_bytes=VMEM_LIMIT),
        name="diff",
    )(slopes, lam_p, q, k, vt, gsub)


def _knockout_topk(sc, k, payload=None):
    n = sc.shape[0]
    row = lax.broadcasted_iota(jnp.int32, sc.shape, 0).astype(jnp.float32)
    vals, picks = [], []
    for _ in range(k):
        m = jnp.max(sc, axis=0, keepdims=True)
        idx = jnp.min(jnp.where(sc == m, row, float(n)), axis=0, keepdims=True)
        hit = row == idx
        vals.append(m)
        picks.append(idx if payload is None else jnp.sum(jnp.where(hit, payload, 0.0), axis=0, keepdims=True))
        sc = jnp.where(hit, NEG_INF, sc)
    return vals, picks


def _mix_kernel(x_ref, am_ref, ad_ref, wo_ref, g2_ref, wq_ref, keys_ref,
                x1_ref, xn_ref, exp_ref, gate_ref):
    x1 = x_ref[...] + _dot(am_ref[...], wo_ref[0:512, :]) + _dot(ad_ref[...], wo_ref[512:1024, :])
    x1_ref[...] = x1
    ms = jnp.mean(x1 * x1, axis=-1, keepdims=True)
    xn = x1 * lax.rsqrt(ms + RMS_EPS) * g2_ref[...]
    xn_ref[...] = xn
    qq = _dot(xn.astype(jnp.bfloat16), wq_ref[...]).astype(jnp.bfloat16)
    tm = qq.shape[0]
    exp_rows, gate_rows = [], []
    for h in range(PEER_HEADS):
        tops = []
        for half in range(2):
            s = h * 2 + half
            sc = _dot_nt(keys_ref[s], qq[:, s * LANES:(s + 1) * LANES])
            tops.append(_knockout_topk(sc, PEER_TOPK))
        (s1, i1), (s2, i2) = tops
        s2a = jnp.concatenate(s2, axis=0)
        i2a = jnp.concatenate(i2, axis=0)
        cand, cidx = [], []
        for a in range(PEER_TOPK):
            nb = PEER_TOPK // (a + 1)
            cand.append(s1[a] + s2a[0:nb])
            cidx.append(i1[a] * float(PEER_N_KEYS) + i2a[0:nb])
        n_cand = sum(c.shape[0] for c in cand)
        pad = -n_cand % SUBLANES
        cand.append(jnp.full((pad, tm), NEG_INF, jnp.float32))
        cidx.append(jnp.zeros((pad, tm), jnp.float32))
        best, experts = _knockout_topk(jnp.concatenate(cand, axis=0), PEER_TOPK,
                                       payload=jnp.concatenate(cidx, axis=0))
        b = jnp.concatenate(best, axis=0)
        e = jnp.exp(b - b[0:1])
        gate_rows.append(e / jnp.sum(e, axis=0, keepdims=True))
        exp_rows.append(jnp.concatenate(experts, axis=0))
    exp_ref[...] = jnp.concatenate(exp_rows, axis=0).T.astype(jnp.int32)
    gate_ref[...] = jnp.concatenate(gate_rows, axis=0).T


def _mix(x2, am, ad, wo, g2, wq, keys):
    T = x2.shape[0]
    row = lambda i: (i, 0)
    const = lambda i: (0, 0)
    return pl.pallas_call(
        _mix_kernel,
        grid=(T // MIX_TM,),
        in_specs=[
            pl.BlockSpec((MIX_TM, 1024), row),
            pl.BlockSpec((MIX_TM, 512), row),
            pl.BlockSpec((MIX_TM, 512), row),
            pl.BlockSpec((1024, 1024), const),
            pl.BlockSpec((1, 1024), const),
            pl.BlockSpec((1024, 2048), const),
            pl.BlockSpec((2 * PEER_HEADS, PEER_N_KEYS, LANES), lambda i: (0, 0, 0)),
        ],
        out_specs=[
            pl.BlockSpec((MIX_TM, 1024), row),
            pl.BlockSpec((MIX_TM, 1024), row),
            pl.BlockSpec((MIX_TM, LANES), row),
            pl.BlockSpec((MIX_TM, LANES), row),
        ],
        out_shape=[
            jax.ShapeDtypeStruct((T, 1024), jnp.float32),
            jax.ShapeDtypeStruct((T, 1024), jnp.float32),
            jax.ShapeDtypeStruct((T, LANES), jnp.int32),
            jax.ShapeDtypeStruct((T, LANES), jnp.float32),
        ],
        compiler_params=pltpu.CompilerParams(dimension_semantics=("arbitrary",), vmem_limit_bytes=VMEM_LIMIT),
        name="mix",
    )(x2, am, ad, wo, g2, wq, keys)


def _pack_kernel(t_ref, o_ref):
    t = t_ref[...]
    lo = pltpu.bitcast(t[:, 0:512].astype(jnp.bfloat16).astype(jnp.float32), jnp.int32)
    hi = pltpu.bitcast(t[:, 512:1024].astype(jnp.bfloat16).astype(jnp.float32), jnp.int32)
    o_ref[...] = lax.shift_right_logical(lo, 16) | (hi & jnp.int32(-65536))


def _pack_table(tab):
    n = tab.shape[0]
    return pl.pallas_call(
        _pack_kernel,
        grid=(n // PACK_ROWS,),
        in_specs=[pl.BlockSpec((PACK_ROWS, 1024), lambda i: (i, 0))],
        out_specs=pl.BlockSpec((PACK_ROWS, 512), lambda i: (i, 0)),
        out_shape=jax.ShapeDtypeStruct((n, 512), jnp.int32),
        compiler_params=pltpu.CompilerParams(dimension_semantics=("arbitrary",)),
        name="pack",
    )(tab)


def _decode(w):
    lo = pltpu.bitcast(w << 16, jnp.float32)
    hi = pltpu.bitcast(w & jnp.int32(-65536), jnp.float32)
    return lo, hi


def _peer_u_kernel(exp_ref, tab_ref, x_ref, g_ref, act_ref, q0_ref, q1_ref, qs_ref):
    ntok, nsel = exp_ref.shape
    ones = jnp.ones((SUBLANES, LANES), jnp.bfloat16)

    def products(t, q_ref):
        x = x_ref[t]
        xlo = x[0:4]
        xhi = x[4:8]
        for e in range(nsel):
            lo, hi = _decode(tab_ref[exp_ref[t, e]])
            q_ref[pl.ds(4 * e, 4), :] = lo * xlo + hi * xhi

    def fold(t, q_ref):
        q = q_ref[pl.ds(0, nsel, stride=4), :]
        for s in range(1, 4):
            q = q + q_ref[pl.ds(s, nsel, stride=4), :]
        qs_ref[pl.ds(pl.multiple_of(t * nsel, nsel), nsel), :] = q

    def gather_group(g):
        for k in range(PEER_GROUP):
            q_ref = (q0_ref, q1_ref)[k % 2]
            products(g * PEER_GROUP + k, q_ref)
            fold(g * PEER_GROUP + k, q_ref)

    def finish_group(g):
        rows8 = PEER_GROUP * nsel
        q_hi, q_lo = _split_bf16(qs_ref[pl.ds(pl.multiple_of(g * rows8, rows8), rows8), :])
        row = (_dot_nt(ones, q_hi) + _dot_nt(ones, q_lo))[0:1]
        h = jnp.concatenate([row[:, k * nsel:(k + 1) * nsel] for k in range(PEER_GROUP)], axis=0)
        gelu = 0.5 * h * (1.0 + lax.erf(h * (1.0 / math.sqrt(2.0))))
        rows = pl.ds(pl.multiple_of(g * PEER_GROUP, PEER_GROUP), PEER_GROUP)
        act_ref[rows, :] = gelu * g_ref[rows, :]

    ngroups = ntok // PEER_GROUP
    gather_group(0)

    def body(g, carry):
        finish_group(g - 1)
        gather_group(g)
        return carry

    lax.fori_loop(1, ngroups, body, 0)
    finish_group(ngroups - 1)


def _sc_peer_v(vtab, idx, actrep, x1, tok0, ntok):
    d = x1.shape[1]
    nsel = idx.shape[1]
    info = plsc.get_sparse_core_info()
    nw = info.num_cores * info.num_subcores
    per_w = ntok // nw
    nchunk = nsel // SC_ROWS
    nitem = SC_TB * nchunk
    assert info.num_lanes == SC_LANES and ntok % (nw * SC_TB) == 0 and nitem % 2 == 0
    mesh = plsc.VectorSubcoreMesh(core_axis_name="c", subcore_axis_name="s")

    @functools.partial(
        pl.kernel, mesh=mesh,
        out_type=jax.ShapeDtypeStruct((ntok, d), jnp.float32),
        scratch_types=[
            pltpu.VMEM((SC_TB, nsel), jnp.int32),
            pltpu.VMEM((SC_TB, nsel * SC_LANES), jnp.float32),
            pltpu.VMEM((SC_ROWS, d), jnp.float32),
            pltpu.VMEM((SC_ROWS, d), jnp.float32),
            pltpu.VMEM((SC_TB, d), jnp.float32),
            pltpu.SemaphoreType.DMA,
            pltpu.SemaphoreType.DMA,
        ],
        name="sc_peer_v",
    )
    def k(vtab_hbm, idx_hbm, act_hbm, x1_hbm, out_hbm, idx_v, act_v, rows0, rows1, acc_v, sem0, sem1):
        wid = lax.axis_index("s") * info.num_cores + lax.axis_index("c")

        def gather(item, rows, sem):
            tl = item // nchunk
            ch = item % nchunk
            return pltpu.make_async_copy(vtab_hbm.at[idx_v.at[tl, pl.ds(ch * SC_ROWS, SC_ROWS)]], rows, sem)

        def accumulate(item, rows):
            tl = item // nchunk
            r0 = (item % nchunk) * SC_ROWS
            a = [act_v[tl, pl.ds((r0 + e) * SC_LANES, SC_LANES)] for e in range(SC_ROWS)]

            @plsc.parallel_loop(0, d // SC_LANES, 1, unroll=2)
            def _(c):
                col = pl.ds(c * SC_LANES, SC_LANES)
                terms = [a[e] * rows[e, col] for e in range(SC_ROWS)]
                while len(terms) > 1:
                    terms = [terms[j] + terms[j + 1] for j in range(0, len(terms), 2)]
                acc_v[tl, col] = acc_v[tl, col] + terms[0]

        @pl.loop(0, per_w // SC_TB)
        def _(bi):
            loc = pl.multiple_of(wid * per_w + bi * SC_TB, SC_TB)
            src = pl.ds(tok0 + loc, SC_TB)
            pltpu.sync_copy(idx_hbm.at[src], idx_v)
            pltpu.sync_copy(act_hbm.at[pl.ds(loc, SC_TB)], act_v)
            pltpu.sync_copy(x1_hbm.at[src], acc_v)
            gather(0, rows0, sem0).start()

            @pl.loop(0, nitem // 2)
            def _(p):
                gather(2 * p + 1, rows1, sem1).start()
                gather(2 * p, rows0, sem0).wait()
                accumulate(2 * p, rows0)

                @pl.when(p < nitem // 2 - 1)
                def _():
                    gather(2 * p + 2, rows0, sem0).start()

                gather(2 * p + 1, rows1, sem1).wait()
                accumulate(2 * p + 1, rows1)

            pltpu.sync_copy(acc_v, out_hbm.at[pl.ds(loc, SC_TB)])

    return k(vtab, idx, actrep, x1)


def _peer_v_kernel(exp_ref, act_ref, tab_ref, x_ref, o_ref):
    nsel = exp_ref.shape[1]

    def tok(t, carry):
        acc_lo = jnp.zeros((4, LANES), jnp.float32)
        acc_hi = jnp.zeros((4, LANES), jnp.float32)
        for e in range(nsel):
            a = act_ref[t, e]
            lo, hi = _decode(tab_ref[exp_ref[t, e]])
            acc_lo = acc_lo + a * lo
            acc_hi = acc_hi + a * hi
        o_ref[t] = x_ref[t] + jnp.concatenate([acc_lo, acc_hi], axis=0)
        return carry

    lax.fori_loop(0, x_ref.shape[0], tok, 0)


def _peer_experts(xn3, x1, experts, gates, u_packed, v_table, v_packed, plan):
    T, nsel = experts.shape
    n_exp = u_packed.shape[0]
    d = x1.shape[1]
    unit = T // sum(u for u, _ in plan)
    assert nsel == LANES and PEER_TT % PEER_GROUP == 0 and unit % PEER_TT == 0
    tab_spec = pl.BlockSpec((n_exp, 4, LANES), lambda i: (0, 0, 0), pipeline_mode=pl.Buffered(1))
    cp = pltpu.CompilerParams(dimension_semantics=("arbitrary",), vmem_limit_bytes=VMEM_LIMIT)
    x13 = x1.reshape(T, SUBLANES, LANES) if not all(on_sc for _, on_sc in plan) else None
    outs = []
    tok0 = 0
    for units, on_sc in plan:
        tc = units * unit
        steps = tc // PEER_TT
        off = tok0 // PEER_TT
        row = lambda i, off=off: (i + off, 0)
        row3 = lambda i, off=off: (i + off, 0, 0)
        act = pl.pallas_call(
            _peer_u_kernel,
            grid=(steps,),
            in_specs=[pl.BlockSpec((PEER_TT, nsel), row, memory_space=pltpu.SMEM), tab_spec,
                      pl.BlockSpec((PEER_TT, SUBLANES, LANES), row3), pl.BlockSpec((PEER_TT, nsel), row)],
            out_specs=pl.BlockSpec((PEER_TT, nsel), lambda i: (i, 0)),
            out_shape=jax.ShapeDtypeStruct((tc, nsel), jnp.float32),
            scratch_shapes=[pltpu.VMEM((4 * nsel, LANES), jnp.float32), pltpu.VMEM((4 * nsel, LANES), jnp.float32),
                            pltpu.VMEM((PEER_TT * nsel, LANES), jnp.float32)],
            compiler_params=cp,
            name="peer_u",
        )(experts, u_packed, xn3, gates)
        if on_sc:
            actrep = jnp.broadcast_to(act[:, :, None], (tc, nsel, SC_LANES)).reshape(tc, nsel * SC_LANES)
            outs.append(_sc_peer_v(v_table, experts, actrep, x1, tok0, tc))
        else:
            out3 = pl.pallas_call(
                _peer_v_kernel,
                grid=(steps,),
                in_specs=[pl.BlockSpec((PEER_TT, nsel), row, memory_space=pltpu.SMEM),
                          pl.BlockSpec((PEER_TT, nsel), lambda i: (i, 0), memory_space=pltpu.SMEM),
                          tab_spec, pl.BlockSpec((PEER_TT, SUBLANES, LANES), row3)],
                out_specs=pl.BlockSpec((PEER_TT, SUBLANES, LANES), lambda i: (i, 0, 0)),
                out_shape=jax.ShapeDtypeStruct((tc, SUBLANES, LANES), jnp.float32),
                compiler_params=cp,
                name="peer_v",
            )(experts, act, v_packed, x13)
            outs.append(out3.reshape(tc, d))
        tok0 += tc
    return jnp.concatenate(outs, axis=0), act


def _alibi_slopes(n):
    return 2.0 ** (-8.0 * np.arange(1, n + 1) / n)


def _lambda_init(layer):
    return 0.8 - 0.6 * math.exp(-0.3 * layer)


def kernel(x, norm1_g, w_in, moba_qk_norm_g, diff_qk_norm_g, diff_lambda, diff_subln_g, w_out, norm2_g,
           peer_w_query, peer_sub_keys, peer_u, peer_v):
    b, s, d = x.shape
    T = b * s
    depth = w_in.shape[0]
    mw = MOBA_HEADS * HEAD_DIM
    assert d == 1024 and s % PROJ_TM == 0 and mw == 512 and DIFF_HEADS * 2 * HEAD_DIM == 512
    f32, bf16 = jnp.float32, jnp.bfloat16
    scale = HEAD_DIM ** -0.5
    grp = lax.broadcasted_iota(jnp.int32, (LANES, LANES), 0) // HEAD_DIM
    gsum = (grp == grp.T).astype(bf16)
    moba_slopes = jnp.asarray(_alibi_slopes(MOBA_HEADS).reshape(MOBA_HEADS // 2, 2), f32)
    diff_slopes = jnp.asarray(np.repeat(_alibi_slopes(DIFF_HEADS)[:, None], 2, axis=1), f32)
    nblk = PROJ_TM // MOBA_BLOCK
    assert b % PIPE_GROUPS == 0
    bg = b // PIPE_GROUPS
    tg = bg * s
    x2 = x.reshape(T, d)
    for l in range(depth):
        w = w_in[l]
        wqk = jnp.concatenate([w[:, 0:2 * mw], w[:, 3 * mw:3 * mw + 1024]], axis=1).astype(bf16)
        wvt = jnp.concatenate([w[:, 2 * mw:3 * mw], w[:, 3 * mw + 1024:]], axis=1).T.astype(bf16)
        mg, dg = moba_qk_norm_g[l].astype(f32), diff_qk_norm_g[l].astype(f32)
        gqk = jnp.concatenate([jnp.tile(mg[0] * scale, 8), jnp.tile(mg[1], 8),
                               jnp.tile(dg[0] * scale, 8), jnp.tile(dg[1], 8)])[None, :]
        g1 = norm1_g[l][None, :].astype(f32)
        g2 = norm2_g[l][None, :].astype(f32)
        gsub = jnp.broadcast_to(diff_subln_g[l].astype(f32)[:, None], (LANES, MOBA_BLOCK))
        lam_p = diff_lambda[l].astype(f32)
        wo, wq = w_out[l].astype(bf16), peer_w_query[l].astype(bf16)
        keys = peer_sub_keys[l].reshape(2 * PEER_HEADS, PEER_N_KEYS, LANES).astype(bf16)
        u_packed = _pack_table(peer_u[l]).reshape(peer_u.shape[1], 4, LANES)
        v_packed = _pack_table(peer_v[l]).reshape(peer_v.shape[1], 4, LANES)
        v_table = peer_v[l].astype(f32)
        outs = []
        after = jnp.zeros((SUBLANES, LANES), f32)
        for g in range(PIPE_GROUPS):
            xg = x2[g * tg:(g + 1) * tg]
            mq, mk, dq, dk, mvt, dvt, kmean = _project(xg, g1, wqk, wvt, gqk, gsum, after)
            kmean = kmean.reshape(tg // PROJ_TM, SUBLANES, 512)[:, :nblk].reshape(tg // MOBA_BLOCK, 512)
            moba_o = _moba_attention(mq, mk, mvt, kmean, moba_slopes, bg, s)
            diff_o = _diff_attention(dq, dk, dvt, lam_p, gsub, diff_slopes, _lambda_init(l), bg, s)
            x1, xn, experts, gates = _mix(xg, moba_o, diff_o, wo, g2, wq, keys)
            plan = LAST_PLAN if g == PIPE_GROUPS - 1 else GROUP_PLAN
            out_g, act = _peer_experts(xn.reshape(tg, SUBLANES, LANES), x1, experts, gates, u_packed, v_table,
                                       v_packed, plan)
            after = act[0:SUBLANES]
            outs.append(out_g)
        x2 = jnp.concatenate(outs, axis=0)
    return x2.reshape(b, s, d)
```

```python
import functools
import math

import jax
import jax.numpy as jnp
import numpy as np
from jax import lax
from jax.experimental import pallas as pl
from jax.experimental.pallas import tpu as pltpu
from jax.experimental.pallas import tpu_sc as plsc

HEAD_DIM = 64
MOBA_HEADS = 8
MOBA_BLOCK = 256
MOBA_TOPK = 3
DIFF_HEADS = 4
PEER_HEADS = 8
PEER_N_KEYS = 128
PEER_TOPK = 16
RMS_EPS = 1e-6

LANES = 128
SUBLANES = 8
VMEM_LIMIT = 48 * 1024 * 1024

PROJ_TM = 512
MIX_TM = 256
PEER_TT = 64
PACK_ROWS = 512
ATT_QBLOCKS = 2
PEER_GROUP = 8
PEER_PLAN = ((1, True), (2, True), (2, True), (2, True), (2, True), (2, True), (2, True), (3, False))
SC_LANES = 16
SC_ROWS = 16
SC_NBUF = 4
SC_TB = 8

NEG_INF = float("-inf")
_NT = (((1,), (1,)), ((), ()))


def _dot(a, b):
    return jnp.dot(a, b, preferred_element_type=jnp.float32)


def _dot_nt(a, b):
    return lax.dot_general(a, b, _NT, preferred_element_type=jnp.float32)


def _split_bf16(x):
    hi = x.astype(jnp.bfloat16)
    lo = (x - hi.astype(jnp.float32)).astype(jnp.bfloat16)
    return hi, lo


def _proj_kernel(x_ref, g1_ref, wqk_ref, wvt_ref, gqk_ref, gsum_ref,
                 mq_ref, mk_ref, dq_ref, dk_ref, mvt_ref, dvt_ref, kmean_ref):
    x = x_ref[...]
    ms = jnp.mean(x * x, axis=-1, keepdims=True)
    h = (x * lax.rsqrt(ms + RMS_EPS) * g1_ref[...]).astype(jnp.bfloat16)
    gsum = gsum_ref[...]
    outs = (mq_ref, mk_ref, dq_ref, dk_ref)
    for sec in range(4):
        for half in range(2):
            c0 = sec * 512 + half * 256
            p = _dot(h, wqk_ref[:, c0:c0 + 256])
            for s in range(2):
                ps = p[:, s * LANES:(s + 1) * LANES]
                sq_hi, sq_lo = _split_bf16(ps * ps)
                ssq = _dot(sq_hi, gsum) + _dot(sq_lo, gsum)
                g = gqk_ref[:, c0 + s * LANES:c0 + (s + 1) * LANES]
                y = ps * lax.rsqrt(ssq * (1.0 / HEAD_DIM) + RMS_EPS) * g
                l0 = half * 256 + s * LANES
                outs[sec][:, l0:l0 + LANES] = y.astype(jnp.bfloat16)
                if sec == 1:
                    for blk in range(PROJ_TM // MOBA_BLOCK):
                        kmean_ref[blk:blk + 1, l0:l0 + LANES] = jnp.mean(
                            y[blk * MOBA_BLOCK:(blk + 1) * MOBA_BLOCK], axis=0, keepdims=True)
    kmean_ref[PROJ_TM // MOBA_BLOCK:, :] = jnp.zeros((SUBLANES - PROJ_TM // MOBA_BLOCK, 512), jnp.float32)
    vt = _dot_nt(wvt_ref[...], h).astype(jnp.bfloat16)
    for blk in range(PROJ_TM // MOBA_BLOCK):
        mvt_ref[blk] = vt[0:512, blk * MOBA_BLOCK:(blk + 1) * MOBA_BLOCK]
        dvt_ref[blk] = vt[512:1024, blk * MOBA_BLOCK:(blk + 1) * MOBA_BLOCK]


def _project(x2, g1, wqk, wvt, gqk, gsum):
    T = x2.shape[0]
    nt = T // PROJ_TM
    nblk = PROJ_TM // MOBA_BLOCK
    row = lambda i: (i, 0)
    const = lambda i: (0, 0)
    qk_shape = jax.ShapeDtypeStruct((T, 512), jnp.bfloat16)
    vt_shape = jax.ShapeDtypeStruct((T // MOBA_BLOCK, 512, MOBA_BLOCK), jnp.bfloat16)
    return pl.pallas_call(
        _proj_kernel,
        grid=(nt,),
        in_specs=[
            pl.BlockSpec((PROJ_TM, 1024), row),
            pl.BlockSpec((1, 1024), const),
            pl.BlockSpec((1024, 2048), const),
            pl.BlockSpec((1024, 1024), const),
            pl.BlockSpec((1, 2048), const),
            pl.BlockSpec((LANES, LANES), const),
        ],
        out_specs=[
            pl.BlockSpec((PROJ_TM, 512), row),
            pl.BlockSpec((PROJ_TM, 512), row),
            pl.BlockSpec((PROJ_TM, 512), row),
            pl.BlockSpec((PROJ_TM, 512), row),
            pl.BlockSpec((nblk, 512, MOBA_BLOCK), lambda i: (i, 0, 0)),
            pl.BlockSpec((nblk, 512, MOBA_BLOCK), lambda i: (i, 0, 0)),
            pl.BlockSpec((SUBLANES, 512), row),
        ],
        out_shape=[qk_shape, qk_shape, qk_shape, qk_shape, vt_shape, vt_shape,
                   jax.ShapeDtypeStruct((nt * SUBLANES, 512), jnp.float32)],
        compiler_params=pltpu.CompilerParams(dimension_semantics=("arbitrary",), vmem_limit_bytes=VMEM_LIMIT),
        name="proj",
    )(x2, g1, wqk, wvt, gqk, gsum)


def _rel_qk():
    qc = lax.broadcasted_iota(jnp.int32, (MOBA_BLOCK, MOBA_BLOCK), 1)
    kr = lax.broadcasted_iota(jnp.int32, (MOBA_BLOCK, MOBA_BLOCK), 0)
    return qc - kr


def _attend(q_h, k_ref, vt_load, i, slope, rel, selbias):
    nb = rel.astype(jnp.float32) * (-slope)
    ts, rowbs, colmax = [], [], []
    for j in range(i + 1):
        t = _dot_nt(k_ref[j * MOBA_BLOCK:(j + 1) * MOBA_BLOCK, :], q_h) + nb
        if j == i:
            t = jnp.where(rel >= 0, t, NEG_INF)
            rowb = jnp.zeros((1, MOBA_BLOCK), jnp.float32)
        else:
            rowb = jnp.full((1, MOBA_BLOCK), (i - j) * MOBA_BLOCK, jnp.float32) * (-slope)
            if selbias is not None:
                rowb = rowb + selbias[j:j + 1, :]
        ts.append(t)
        rowbs.append(rowb)
        colmax.append(jnp.max(t, axis=0, keepdims=True) + rowb)
    m = functools.reduce(jnp.maximum, colmax)
    l, acc = None, None
    for j in range(i + 1):
        p = jnp.exp(ts[j] - (m - rowbs[j]))
        lj = jnp.sum(p, axis=0, keepdims=True)
        aj = _dot(vt_load(j), p.astype(jnp.bfloat16))
        l = lj if l is None else l + lj
        acc = aj if acc is None else acc + aj
    return acc / l


def _head_queries(q2):
    lane = lax.broadcasted_iota(jnp.int32, q2.shape, 1)
    return [jnp.where((lane >= hh * HEAD_DIM) & (lane < (hh + 1) * HEAD_DIM), q2, jnp.zeros_like(q2))
            for hh in range(2)]


def _block_selection(gate, i):
    blk = lax.broadcasted_iota(jnp.int32, gate.shape, 0)
    cnt = jnp.zeros(gate.shape, jnp.int32)
    for mblk in range(i):
        gm = gate[mblk:mblk + 1, :]
        beats = (gm > gate) | ((gm == gate) & (mblk < blk))
        cnt = cnt + jnp.where(beats, 1, 0)
    return jnp.where((blk < i) & (cnt < MOBA_TOPK), 0.0, NEG_INF)


def _moba_kernel(slopes_ref, q_ref, k_ref, vt_ref, kmean_ref, o_ref):
    pair = pl.program_id(1)
    step = pl.program_id(2)
    nq = vt_ref.shape[0]
    rel = _rel_qk()
    km_hi, km_lo = _split_bf16(kmean_ref[...])
    for st in range(nq // ATT_QBLOCKS):
        @pl.when(step == st)
        def _(st=st):
            for u in range(ATT_QBLOCKS):
                i = st * ATT_QBLOCKS + u
                rows = slice(u * MOBA_BLOCK, (u + 1) * MOBA_BLOCK)
                outs = []
                for hh, q_h in enumerate(_head_queries(q_ref[rows, :])):
                    gate = _dot_nt(km_hi, q_h) + _dot_nt(km_lo, q_h)
                    vt_load = lambda j, hh=hh: vt_ref[j, hh * HEAD_DIM:(hh + 1) * HEAD_DIM, :]
                    outs.append(_attend(q_h, k_ref, vt_load, i, slopes_ref[pair, hh], rel,
                                        _block_selection(gate, i)))
                o_t = jnp.concatenate(outs, axis=0)
                o_ref[rows, :] = o_t.T.astype(o_ref.dtype)


def _diff_kernel(slopes_ref, lam_ref, q_ref, k_ref, vt_ref, gsub_ref, o_ref, *, lam_init):
    head = pl.program_id(1)
    step = pl.program_id(2)
    nq = vt_ref.shape[0]
    rel = _rel_qk()
    slope = slopes_ref[head, 0]
    lp = lam_ref[...]
    lam = (jnp.exp(jnp.sum(lp[0:1] * lp[1:2], axis=-1, keepdims=True))
           - jnp.exp(jnp.sum(lp[2:3] * lp[3:4], axis=-1, keepdims=True)) + lam_init)
    vt_load = lambda j: vt_ref[j]
    for st in range(nq // ATT_QBLOCKS):
        @pl.when(step == st)
        def _(st=st):
            for u in range(ATT_QBLOCKS):
                i = st * ATT_QBLOCKS + u
                rows = slice(u * MOBA_BLOCK, (u + 1) * MOBA_BLOCK)
                parts = [_attend(q_h, k_ref, vt_load, i, slope, rel, None)
                         for q_h in _head_queries(q_ref[rows, :])]
                o_t = parts[0] - lam * parts[1]
                ms = jnp.mean(o_t * o_t, axis=0, keepdims=True)
                o_t = o_t * lax.rsqrt(ms + RMS_EPS) * gsub_ref[...] * (1.0 - lam_init)
                o_ref[rows, :] = o_t.T.astype(o_ref.dtype)


def _attention_specs(nq):
    nst = nq // ATT_QBLOCKS
    q_spec = pl.BlockSpec((ATT_QBLOCKS * MOBA_BLOCK, LANES), lambda b, p, i: (b * nst + i, p))
    k_spec = pl.BlockSpec((nq * MOBA_BLOCK, LANES), lambda b, p, i: (b, p))
    vt_spec = pl.BlockSpec((nq, LANES, MOBA_BLOCK), lambda b, p, i: (b, p, 0))
    return q_spec, k_spec, vt_spec


def _moba_attention(q, k, vt, kmean, slopes, batch, seq):
    nq = seq // MOBA_BLOCK
    q_spec, k_spec, vt_spec = _attention_specs(nq)
    return pl.pallas_call(
        _moba_kernel,
        grid=(batch, MOBA_HEADS // 2, nq // ATT_QBLOCKS),
        in_specs=[
            pl.BlockSpec(memory_space=pltpu.SMEM),
            q_spec, k_spec, vt_spec,
            pl.BlockSpec((SUBLANES, LANES), lambda b, p, i: (b, p)),
        ],
        out_specs=q_spec,
        out_shape=jax.ShapeDtypeStruct(q.shape, jnp.bfloat16),
        compiler_params=pltpu.CompilerParams(
            dimension_semantics=("arbitrary", "arbitrary", "arbitrary"), vmem_limit_bytes=VMEM_LIMIT),
        name="moba",
    )(slopes, q, k, vt, kmean)


def _diff_attention(q, k, vt, lam_p, gsub, slopes, lam_init, batch, seq):
    nq = seq // MOBA_BLOCK
    q_spec, k_spec, vt_spec = _attention_specs(nq)
    return pl.pallas_call(
        functools.partial(_diff_kernel, lam_init=lam_init),
        grid=(batch, DIFF_HEADS, nq // ATT_QBLOCKS),
        in_specs=[
            pl.BlockSpec(memory_space=pltpu.SMEM),
            pl.BlockSpec((4, HEAD_DIM), lambda b, p, i: (0, 0)),
            q_spec, k_spec, vt_spec,
            pl.BlockSpec((LANES, MOBA_BLOCK), lambda b, p, i: (0, 0)),
        ],
        out_specs=q_spec,
        out_shape=jax.ShapeDtypeStruct(q.shape, jnp.bfloat16),
        compiler_params=pltpu.CompilerParams(
            dimension_semantics=("arbitrary", "arbitrary", "arbitrary"), vmem_limit_bytes=VMEM_LIMIT),
        name="diff",
    )(slopes, lam_p, q, k, vt, gsub)


def _knockout_topk(sc, k, payload=None):
    n = sc.shape[0]
    row = lax.broadcasted_iota(jnp.int32, sc.shape, 0).astype(jnp.float32)
    vals, picks = [], []
    for _ in range(k):
        m = jnp.max(sc, axis=0, keepdims=True)
        idx = jnp.min(jnp.where(sc == m, row, float(n)), axis=0, keepdims=True)
        hit = row == idx
        vals.append(m)
        picks.append(idx if payload is None else jnp.sum(jnp.where(hit, payload, 0.0), axis=0, keepdims=True))
        sc = jnp.where(hit, NEG_INF, sc)
    return vals, picks


def _mix_kernel(x_ref, am_ref, ad_ref, wo_ref, g2_ref, wq_ref, keys_ref,
                x1_ref, xn_ref, exp_ref, gate_ref):
    x1 = x_ref[...] + _dot(am_ref[...], wo_ref[0:512, :]) + _dot(ad_ref[...], wo_ref[512:1024, :])
    x1_ref[...] = x1
    ms = jnp.mean(x1 * x1, axis=-1, keepdims=True)
    xn = x1 * lax.rsqrt(ms + RMS_EPS) * g2_ref[...]
    xn_ref[...] = xn
    qq = _dot(xn.astype(jnp.bfloat16), wq_ref[...]).astype(jnp.bfloat16)
    tm = qq.shape[0]
    exp_rows, gate_rows = [], []
    for h in range(PEER_HEADS):
        tops = []
        for half in range(2):
            s = h * 2 + half
            sc = _dot_nt(keys_ref[s], qq[:, s * LANES:(s + 1) * LANES])
            tops.append(_knockout_topk(sc, PEER_TOPK))
        (s1, i1), (s2, i2) = tops
        s2a = jnp.concatenate(s2, axis=0)
        i2a = jnp.concatenate(i2, axis=0)
        cand, cidx = [], []
        for a in range(PEER_TOPK):
            nb = PEER_TOPK // (a + 1)
            cand.append(s1[a] + s2a[0:nb])
            cidx.append(i1[a] * float(PEER_N_KEYS) + i2a[0:nb])
        n_cand = sum(c.shape[0] for c in cand)
        pad = -n_cand % SUBLANES
        cand.append(jnp.full((pad, tm), NEG_INF, jnp.float32))
        cidx.append(jnp.zeros((pad, tm), jnp.float32))
        best, experts = _knockout_topk(jnp.concatenate(cand, axis=0), PEER_TOPK,
                                       payload=jnp.concatenate(cidx, axis=0))
        b = jnp.concatenate(best, axis=0)
        e = jnp.exp(b - b[0:1])
        gate_rows.append(e / jnp.sum(e, axis=0, keepdims=True))
        exp_rows.append(jnp.concatenate(experts, axis=0))
    exp_ref[...] = jnp.concatenate(exp_rows, axis=0).T.astype(jnp.int32)
    gate_ref[...] = jnp.concatenate(gate_rows, axis=0).T


def _mix(x2, am, ad, wo, g2, wq, keys):
    T = x2.shape[0]
    row = lambda i: (i, 0)
    const = lambda i: (0, 0)
    return pl.pallas_call(
        _mix_kernel,
        grid=(T // MIX_TM,),
        in_specs=[
            pl.BlockSpec((MIX_TM, 1024), row),
            pl.BlockSpec((MIX_TM, 512), row),
            pl.BlockSpec((MIX_TM, 512), row),
            pl.BlockSpec((1024, 1024), const),
            pl.BlockSpec((1, 1024), const),
            pl.BlockSpec((1024, 2048), const),
            pl.BlockSpec((2 * PEER_HEADS, PEER_N_KEYS, LANES), lambda i: (0, 0, 0)),
        ],
        out_specs=[
            pl.BlockSpec((MIX_TM, 1024), row),
            pl.BlockSpec((MIX_TM, 1024), row),
            pl.BlockSpec((MIX_TM, LANES), row),
            pl.BlockSpec((MIX_TM, LANES), row),
        ],
        out_shape=[
            jax.ShapeDtypeStruct((T, 1024), jnp.float32),
            jax.ShapeDtypeStruct((T, 1024), jnp.float32),
            jax.ShapeDtypeStruct((T, LANES), jnp.int32),
            jax.ShapeDtypeStruct((T, LANES), jnp.float32),
        ],
        compiler_params=pltpu.CompilerParams(dimension_semantics=("arbitrary",), vmem_limit_bytes=VMEM_LIMIT),
        name="mix",
    )(x2, am, ad, wo, g2, wq, keys)


def _pack_kernel(t_ref, o_ref):
    t = t_ref[...]
    lo = pltpu.bitcast(t[:, 0:512].astype(jnp.bfloat16).astype(jnp.float32), jnp.int32)
    hi = pltpu.bitcast(t[:, 512:1024].astype(jnp.bfloat16).astype(jnp.float32), jnp.int32)
    o_ref[...] = lax.shift_right_logical(lo, 16) | (hi & jnp.int32(-65536))


def _pack_table(tab):
    n = tab.shape[0]
    return pl.pallas_call(
        _pack_kernel,
        grid=(n // PACK_ROWS,),
        in_specs=[pl.BlockSpec((PACK_ROWS, 1024), lambda i: (i, 0))],
        out_specs=pl.BlockSpec((PACK_ROWS, 512), lambda i: (i, 0)),
        out_shape=jax.ShapeDtypeStruct((n, 512), jnp.int32),
        compiler_params=pltpu.CompilerParams(dimension_semantics=("arbitrary",)),
        name="pack",
    )(tab)


def _decode(w):
    lo = pltpu.bitcast(w << 16, jnp.float32)
    hi = pltpu.bitcast(w & jnp.int32(-65536), jnp.float32)
    return lo, hi


def _peer_u_kernel(exp_ref, tab_ref, x_ref, g_ref, act_ref, q0_ref, q1_ref, qs_ref):
    ntok, nsel = exp_ref.shape
    ones = jnp.ones((SUBLANES, LANES), jnp.bfloat16)

    def products(t, q_ref):
        x = x_ref[t]
        xlo = x[0:4]
        xhi = x[4:8]
        for e in range(nsel):
            lo, hi = _decode(tab_ref[exp_ref[t, e]])
            q_ref[pl.ds(4 * e, 4), :] = lo * xlo + hi * xhi

    def fold(t, q_ref):
        q = q_ref[pl.ds(0, nsel, stride=4), :]
        for s in range(1, 4):
            q = q + q_ref[pl.ds(s, nsel, stride=4), :]
        qs_ref[pl.ds(pl.multiple_of(t * nsel, nsel), nsel), :] = q

    def gather_group(g):
        for k in range(PEER_GROUP):
            q_ref = (q0_ref, q1_ref)[k % 2]
            products(g * PEER_GROUP + k, q_ref)
            fold(g * PEER_GROUP + k, q_ref)

    def finish_group(g):
        rows8 = PEER_GROUP * nsel
        q_hi, q_lo = _split_bf16(qs_ref[pl.ds(pl.multiple_of(g * rows8, rows8), rows8), :])
        row = (_dot_nt(ones, q_hi) + _dot_nt(ones, q_lo))[0:1]
        h = jnp.concatenate([row[:, k * nsel:(k + 1) * nsel] for k in range(PEER_GROUP)], axis=0)
        gelu = 0.5 * h * (1.0 + lax.erf(h * (1.0 / math.sqrt(2.0))))
        rows = pl.ds(pl.multiple_of(g * PEER_GROUP, PEER_GROUP), PEER_GROUP)
        act_ref[rows, :] = gelu * g_ref[rows, :]

    ngroups = ntok // PEER_GROUP
    gather_group(0)

    def body(g, carry):
        finish_group(g - 1)
        gather_group(g)
        return carry

    lax.fori_loop(1, ngroups, body, 0)
    finish_group(ngroups - 1)


def _sc_peer_v(vtab, idx, actrep, x1, tok0, ntok):
    d = x1.shape[1]
    nsel = idx.shape[1]
    info = plsc.get_sparse_core_info()
    nw = info.num_cores * info.num_subcores
    per_w = ntok // nw
    nchunk = nsel // SC_ROWS
    nitem = SC_TB * nchunk
    assert info.num_lanes == SC_LANES and ntok % (nw * SC_TB) == 0 and nitem % SC_NBUF == 0
    mesh = plsc.VectorSubcoreMesh(core_axis_name="c", subcore_axis_name="s")

    @functools.partial(
        pl.kernel, mesh=mesh,
        out_type=jax.ShapeDtypeStruct((ntok, d), jnp.float32),
        scratch_types=[
            pltpu.VMEM((SC_TB, nsel), jnp.int32),
            pltpu.VMEM((SC_TB, nsel * SC_LANES), jnp.float32),
            pltpu.VMEM((SC_TB, d), jnp.float32),
        ] + [pltpu.VMEM((SC_ROWS, d), jnp.float32)] * SC_NBUF + [pltpu.SemaphoreType.DMA] * SC_NBUF,
        name="sc_peer_v",
    )
    def k(vtab_hbm, idx_hbm, act_hbm, x1_hbm, out_hbm, idx_v, act_v, acc_v, *bufs_sems):
        bufs, sems = bufs_sems[:SC_NBUF], bufs_sems[SC_NBUF:]
        wid = lax.axis_index("s") * info.num_cores + lax.axis_index("c")

        def gather(item, rows, sem):
            tl = item // nchunk
            ch = item % nchunk
            return pltpu.make_async_copy(vtab_hbm.at[idx_v.at[tl, pl.ds(ch * SC_ROWS, SC_ROWS)]], rows, sem)

        def accumulate(item, rows):
            tl = item // nchunk
            r0 = (item % nchunk) * SC_ROWS
            a = [act_v[tl, pl.ds((r0 + e) * SC_LANES, SC_LANES)] for e in range(SC_ROWS)]

            @plsc.parallel_loop(0, d // SC_LANES, 1, unroll=2)
            def _(c):
                col = pl.ds(c * SC_LANES, SC_LANES)
                terms = [a[e] * rows[e, col] for e in range(SC_ROWS)]
                while len(terms) > 1:
                    terms = [terms[j] + terms[j + 1] for j in range(0, len(terms), 2)]
                acc_v[tl, col] = acc_v[tl, col] + terms[0]

        @pl.loop(0, per_w // SC_TB)
        def _(bi):
            loc = pl.multiple_of(wid * per_w + bi * SC_TB, SC_TB)
            src = pl.ds(tok0 + loc, SC_TB)
            pltpu.sync_copy(idx_hbm.at[src], idx_v)
            pltpu.sync_copy(act_hbm.at[pl.ds(loc, SC_TB)], act_v)
            pltpu.sync_copy(x1_hbm.at[src], acc_v)
            for j in range(SC_NBUF - 1):
                gather(j, bufs[j], sems[j]).start()

            @pl.loop(0, nitem // SC_NBUF)
            def _(p):
                for j in range(SC_NBUF):
                    item = p * SC_NBUF + j
                    nxt = (j + SC_NBUF - 1) % SC_NBUF

                    @pl.when(item + SC_NBUF - 1 < nitem)
                    def _():
                        gather(item + SC_NBUF - 1, bufs[nxt], sems[nxt]).start()

                    gather(item, bufs[j], sems[j]).wait()
                    accumulate(item, bufs[j])

            pltpu.sync_copy(acc_v, out_hbm.at[pl.ds(loc, SC_TB)])

    return k(vtab, idx, actrep, x1)


def _peer_v_kernel(exp_ref, act_ref, tab_ref, x_ref, o_ref):
    nsel = exp_ref.shape[1]

    def tok(t, carry):
        acc_lo = jnp.zeros((4, LANES), jnp.float32)
        acc_hi = jnp.zeros((4, LANES), jnp.float32)
        for e in range(nsel):
            a = act_ref[t, e]
            lo, hi = _decode(tab_ref[exp_ref[t, e]])
            acc_lo = acc_lo + a * lo
            acc_hi = acc_hi + a * hi
        o_ref[t] = x_ref[t] + jnp.concatenate([acc_lo, acc_hi], axis=0)
        return carry

    lax.fori_loop(0, x_ref.shape[0], tok, 0)


def _peer_experts(xn3, x1, experts, gates, u_packed, v_table, v_packed):
    T, nsel = experts.shape
    n_exp = u_packed.shape[0]
    d = x1.shape[1]
    unit = T // sum(u for u, _ in PEER_PLAN)
    assert nsel == LANES and PEER_TT % PEER_GROUP == 0 and unit % PEER_TT == 0
    tab_spec = pl.BlockSpec((n_exp, 4, LANES), lambda i: (0, 0, 0), pipeline_mode=pl.Buffered(1))
    cp = pltpu.CompilerParams(dimension_semantics=("arbitrary",), vmem_limit_bytes=VMEM_LIMIT)
    x13 = x1.reshape(T, SUBLANES, LANES)
    outs = []
    tok0 = 0
    for units, on_sc in PEER_PLAN:
        tc = units * unit
        steps = tc // PEER_TT
        off = tok0 // PEER_TT
        row = lambda i, off=off: (i + off, 0)
        row3 = lambda i, off=off: (i + off, 0, 0)
        act = pl.pallas_call(
            _peer_u_kernel,
            grid=(steps,),
            in_specs=[pl.BlockSpec((PEER_TT, nsel), row, memory_space=pltpu.SMEM), tab_spec,
                      pl.BlockSpec((PEER_TT, SUBLANES, LANES), row3), pl.BlockSpec((PEER_TT, nsel), row)],
            out_specs=pl.BlockSpec((PEER_TT, nsel), lambda i: (i, 0)),
            out_shape=jax.ShapeDtypeStruct((tc, nsel), jnp.float32),
            scratch_shapes=[pltpu.VMEM((4 * nsel, LANES), jnp.float32), pltpu.VMEM((4 * nsel, LANES), jnp.float32),
                            pltpu.VMEM((PEER_TT * nsel, LANES), jnp.float32)],
            compiler_params=cp,
            name="peer_u",
        )(experts, u_packed, xn3, gates)
        if on_sc:
            actrep = jnp.broadcast_to(act[:, :, None], (tc, nsel, SC_LANES)).reshape(tc, nsel * SC_LANES)
            outs.append(_sc_peer_v(v_table, experts, actrep, x1, tok0, tc))
        else:
            out3 = pl.pallas_call(
                _peer_v_kernel,
                grid=(steps,),
                in_specs=[pl.BlockSpec((PEER_TT, nsel), row, memory_space=pltpu.SMEM),
                          pl.BlockSpec((PEER_TT, nsel), lambda i: (i, 0), memory_space=pltpu.SMEM),
                          tab_spec, pl.BlockSpec((PEER_TT, SUBLANES, LANES), row3)],
                out_specs=pl.BlockSpec((PEER_TT, SUBLANES, LANES), lambda i: (i, 0, 0)),
                out_shape=jax.ShapeDtypeStruct((tc, SUBLANES, LANES), jnp.float32),
                compiler_params=cp,
                name="peer_v",
            )(experts, act, v_packed, x13)
            outs.append(out3.reshape(tc, d))
        tok0 += tc
    return jnp.concatenate(outs, axis=0)


def _alibi_slopes(n):
    return 2.0 ** (-8.0 * np.arange(1, n + 1) / n)


def _lambda_init(layer):
    return 0.8 - 0.6 * math.exp(-0.3 * layer)


def kernel(x, norm1_g, w_in, moba_qk_norm_g, diff_qk_norm_g, diff_lambda, diff_subln_g, w_out, norm2_g,
           peer_w_query, peer_sub_keys, peer_u, peer_v):
    b, s, d = x.shape
    T = b * s
    depth = w_in.shape[0]
    mw = MOBA_HEADS * HEAD_DIM
    assert d == 1024 and s % PROJ_TM == 0 and mw == 512 and DIFF_HEADS * 2 * HEAD_DIM == 512
    f32, bf16 = jnp.float32, jnp.bfloat16
    scale = HEAD_DIM ** -0.5
    grp = lax.broadcasted_iota(jnp.int32, (LANES, LANES), 0) // HEAD_DIM
    gsum = (grp == grp.T).astype(bf16)
    moba_slopes = jnp.asarray(_alibi_slopes(MOBA_HEADS).reshape(MOBA_HEADS // 2, 2), f32)
    diff_slopes = jnp.asarray(np.repeat(_alibi_slopes(DIFF_HEADS)[:, None], 2, axis=1), f32)
    nblk = PROJ_TM // MOBA_BLOCK
    x2 = x.reshape(T, d)
    for l in range(depth):
        w = w_in[l]
        wqk = jnp.concatenate([w[:, 0:2 * mw], w[:, 3 * mw:3 * mw + 1024]], axis=1).astype(bf16)
        wvt = jnp.concatenate([w[:, 2 * mw:3 * mw], w[:, 3 * mw + 1024:]], axis=1).T.astype(bf16)
        mg, dg = moba_qk_norm_g[l].astype(f32), diff_qk_norm_g[l].astype(f32)
        gqk = jnp.concatenate([jnp.tile(mg[0] * scale, 8), jnp.tile(mg[1], 8),
                               jnp.tile(dg[0] * scale, 8), jnp.tile(dg[1], 8)])[None, :]
        g1 = norm1_g[l][None, :].astype(f32)
        g2 = norm2_g[l][None, :].astype(f32)
        gsub = jnp.broadcast_to(diff_subln_g[l].astype(f32)[:, None], (LANES, MOBA_BLOCK))
        lam_p = diff_lambda[l].astype(f32)
        wo, wq = w_out[l].astype(bf16), peer_w_query[l].astype(bf16)
        keys = peer_sub_keys[l].reshape(2 * PEER_HEADS, PEER_N_KEYS, LANES).astype(bf16)
        u_packed = _pack_table(peer_u[l]).reshape(peer_u.shape[1], 4, LANES)
        v_packed = _pack_table(peer_v[l]).reshape(peer_v.shape[1], 4, LANES)
        v_table = peer_v[l].astype(f32)
        mq, mk, dq, dk, mvt, dvt, kmean = _project(x2, g1, wqk, wvt, gqk, gsum)
        kmean = kmean.reshape(T // PROJ_TM, SUBLANES, 512)[:, :nblk].reshape(T // MOBA_BLOCK, 512)
        moba_o = _moba_attention(mq, mk, mvt, kmean, moba_slopes, b, s)
        diff_o = _diff_attention(dq, dk, dvt, lam_p, gsub, diff_slopes, _lambda_init(l), b, s)
        x1, xn, experts, gates = _mix(x2, moba_o, diff_o, wo, g2, wq, keys)
        x2 = _peer_experts(xn.reshape(T, SUBLANES, LANES), x1, experts, gates, u_packed, v_table, v_packed)
    return x2.reshape(b, s, d)
```

```python
import functools
import math

import jax
import jax.numpy as jnp
import numpy as np
from jax import lax
from jax.experimental import pallas as pl
from jax.experimental.pallas import tpu as pltpu
from jax.experimental.pallas import tpu_sc as plsc

HEAD_DIM = 64
MOBA_HEADS = 8
MOBA_BLOCK = 256
MOBA_TOPK = 3
DIFF_HEADS = 4
PEER_HEADS = 8
PEER_N_KEYS = 128
PEER_TOPK = 16
RMS_EPS = 1e-6

LANES = 128
SUBLANES = 8
VMEM_LIMIT = 48 * 1024 * 1024

PROJ_TM = 512
MIX_TM = 256
PEER_TT = 64
PACK_ROWS = 512
ATT_QBLOCKS = 2
PEER_GROUP = 8
PEER_PLAN = ((1, True), (2, True), (2, True), (2, True), (2, True), (2, True), (2, True), (1, True), (2, False))
SC_LANES = 16
SC_ROWS = 16
SC_NBUF = 4
SC_TB = 8

NEG_INF = float("-inf")
_NT = (((1,), (1,)), ((), ()))


def _dot(a, b):
    return jnp.dot(a, b, preferred_element_type=jnp.float32)


def _dot_nt(a, b):
    return lax.dot_general(a, b, _NT, preferred_element_type=jnp.float32)


def _split_bf16(x):
    hi = x.astype(jnp.bfloat16)
    lo = (x - hi.astype(jnp.float32)).astype(jnp.bfloat16)
    return hi, lo


def _proj_kernel(x_ref, g1_ref, wqk_ref, wvt_ref, gqk_ref, gsum_ref,
                 mq_ref, mk_ref, dq_ref, dk_ref, mvt_ref, dvt_ref, kmean_ref):
    x = x_ref[...]
    ms = jnp.mean(x * x, axis=-1, keepdims=True)
    h = (x * lax.rsqrt(ms + RMS_EPS) * g1_ref[...]).astype(jnp.bfloat16)
    gsum = gsum_ref[...]
    outs = (mq_ref, mk_ref, dq_ref, dk_ref)
    for sec in range(4):
        for half in range(2):
            c0 = sec * 512 + half * 256
            p = _dot(h, wqk_ref[:, c0:c0 + 256])
            for s in range(2):
                ps = p[:, s * LANES:(s + 1) * LANES]
                sq_hi, sq_lo = _split_bf16(ps * ps)
                ssq = _dot(sq_hi, gsum) + _dot(sq_lo, gsum)
                g = gqk_ref[:, c0 + s * LANES:c0 + (s + 1) * LANES]
                y = ps * lax.rsqrt(ssq * (1.0 / HEAD_DIM) + RMS_EPS) * g
                l0 = half * 256 + s * LANES
                outs[sec][:, l0:l0 + LANES] = y.astype(jnp.bfloat16)
                if sec == 1:
                    for blk in range(PROJ_TM // MOBA_BLOCK):
                        kmean_ref[blk:blk + 1, l0:l0 + LANES] = jnp.mean(
                            y[blk * MOBA_BLOCK:(blk + 1) * MOBA_BLOCK], axis=0, keepdims=True)
    kmean_ref[PROJ_TM // MOBA_BLOCK:, :] = jnp.zeros((SUBLANES - PROJ_TM // MOBA_BLOCK, 512), jnp.float32)
    vt = _dot_nt(wvt_ref[...], h).astype(jnp.bfloat16)
    for blk in range(PROJ_TM // MOBA_BLOCK):
        mvt_ref[blk] = vt[0:512, blk * MOBA_BLOCK:(blk + 1) * MOBA_BLOCK]
        dvt_ref[blk] = vt[512:1024, blk * MOBA_BLOCK:(blk + 1) * MOBA_BLOCK]


def _project(x2, g1, wqk, wvt, gqk, gsum):
    T = x2.shape[0]
    nt = T // PROJ_TM
    nblk = PROJ_TM // MOBA_BLOCK
    row = lambda i: (i, 0)
    const = lambda i: (0, 0)
    qk_shape = jax.ShapeDtypeStruct((T, 512), jnp.bfloat16)
    vt_shape = jax.ShapeDtypeStruct((T // MOBA_BLOCK, 512, MOBA_BLOCK), jnp.bfloat16)
    return pl.pallas_call(
        _proj_kernel,
        grid=(nt,),
        in_specs=[
            pl.BlockSpec((PROJ_TM, 1024), row),
            pl.BlockSpec((1, 1024), const),
            pl.BlockSpec((1024, 2048), const),
            pl.BlockSpec((1024, 1024), const),
            pl.BlockSpec((1, 2048), const),
            pl.BlockSpec((LANES, LANES), const),
        ],
        out_specs=[
            pl.BlockSpec((PROJ_TM, 512), row),
            pl.BlockSpec((PROJ_TM, 512), row),
            pl.BlockSpec((PROJ_TM, 512), row),
            pl.BlockSpec((PROJ_TM, 512), row),
            pl.BlockSpec((nblk, 512, MOBA_BLOCK), lambda i: (i, 0, 0)),
            pl.BlockSpec((nblk, 512, MOBA_BLOCK), lambda i: (i, 0, 0)),
            pl.BlockSpec((SUBLANES, 512), row),
        ],
        out_shape=[qk_shape, qk_shape, qk_shape, qk_shape, vt_shape, vt_shape,
                   jax.ShapeDtypeStruct((nt * SUBLANES, 512), jnp.float32)],
        compiler_params=pltpu.CompilerParams(dimension_semantics=("arbitrary",), vmem_limit_bytes=VMEM_LIMIT),
        name="proj",
    )(x2, g1, wqk, wvt, gqk, gsum)


def _rel_qk():
    qc = lax.broadcasted_iota(jnp.int32, (MOBA_BLOCK, MOBA_BLOCK), 1)
    kr = lax.broadcasted_iota(jnp.int32, (MOBA_BLOCK, MOBA_BLOCK), 0)
    return qc - kr


def _attend(q_h, k_ref, vt_load, i, slope, rel, selbias):
    nb = rel.astype(jnp.float32) * (-slope)
    ts, rowbs, colmax = [], [], []
    for j in range(i + 1):
        t = _dot_nt(k_ref[j * MOBA_BLOCK:(j + 1) * MOBA_BLOCK, :], q_h) + nb
        if j == i:
            t = jnp.where(rel >= 0, t, NEG_INF)
            rowb = jnp.zeros((1, MOBA_BLOCK), jnp.float32)
        else:
            rowb = jnp.full((1, MOBA_BLOCK), (i - j) * MOBA_BLOCK, jnp.float32) * (-slope)
            if selbias is not None:
                rowb = rowb + selbias[j:j + 1, :]
        ts.append(t)
        rowbs.append(rowb)
        colmax.append(jnp.max(t, axis=0, keepdims=True) + rowb)
    m = functools.reduce(jnp.maximum, colmax)
    l, acc = None, None
    for j in range(i + 1):
        p = jnp.exp(ts[j] - (m - rowbs[j]))
        lj = jnp.sum(p, axis=0, keepdims=True)
        aj = _dot(vt_load(j), p.astype(jnp.bfloat16))
        l = lj if l is None else l + lj
        acc = aj if acc is None else acc + aj
    return acc / l


def _head_queries(q2):
    lane = lax.broadcasted_iota(jnp.int32, q2.shape, 1)
    return [jnp.where((lane >= hh * HEAD_DIM) & (lane < (hh + 1) * HEAD_DIM), q2, jnp.zeros_like(q2))
            for hh in range(2)]


def _block_selection(gate, i):
    blk = lax.broadcasted_iota(jnp.int32, gate.shape, 0)
    cnt = jnp.zeros(gate.shape, jnp.int32)
    for mblk in range(i):
        gm = gate[mblk:mblk + 1, :]
        beats = (gm > gate) | ((gm == gate) & (mblk < blk))
        cnt = cnt + jnp.where(beats, 1, 0)
    return jnp.where((blk < i) & (cnt < MOBA_TOPK), 0.0, NEG_INF)


def _moba_kernel(slopes_ref, q_ref, k_ref, vt_ref, kmean_ref, o_ref):
    pair = pl.program_id(1)
    step = pl.program_id(2)
    nq = vt_ref.shape[0]
    rel = _rel_qk()
    km_hi, km_lo = _split_bf16(kmean_ref[...])
    for st in range(nq // ATT_QBLOCKS):
        @pl.when(step == st)
        def _(st=st):
            for u in range(ATT_QBLOCKS):
                i = st * ATT_QBLOCKS + u
                rows = slice(u * MOBA_BLOCK, (u + 1) * MOBA_BLOCK)
                outs = []
                for hh, q_h in enumerate(_head_queries(q_ref[rows, :])):
                    gate = _dot_nt(km_hi, q_h) + _dot_nt(km_lo, q_h)
                    vt_load = lambda j, hh=hh: vt_ref[j, hh * HEAD_DIM:(hh + 1) * HEAD_DIM, :]
                    outs.append(_attend(q_h, k_ref, vt_load, i, slopes_ref[pair, hh], rel,
                                        _block_selection(gate, i)))
                o_t = jnp.concatenate(outs, axis=0)
                o_ref[rows, :] = o_t.T.astype(o_ref.dtype)


def _diff_kernel(slopes_ref, lam_ref, q_ref, k_ref, vt_ref, gsub_ref, o_ref, *, lam_init):
    head = pl.program_id(1)
    step = pl.program_id(2)
    nq = vt_ref.shape[0]
    rel = _rel_qk()
    slope = slopes_ref[head, 0]
    lp = lam_ref[...]
    lam = (jnp.exp(jnp.sum(lp[0:1] * lp[1:2], axis=-1, keepdims=True))
           - jnp.exp(jnp.sum(lp[2:3] * lp[3:4], axis=-1, keepdims=True)) + lam_init)
    vt_load = lambda j: vt_ref[j]
    for st in range(nq // ATT_QBLOCKS):
        @pl.when(step == st)
        def _(st=st):
            for u in range(ATT_QBLOCKS):
                i = st * ATT_QBLOCKS + u
                rows = slice(u * MOBA_BLOCK, (u + 1) * MOBA_BLOCK)
                parts = [_attend(q_h, k_ref, vt_load, i, slope, rel, None)
                         for q_h in _head_queries(q_ref[rows, :])]
                o_t = parts[0] - lam * parts[1]
                ms = jnp.mean(o_t * o_t, axis=0, keepdims=True)
                o_t = o_t * lax.rsqrt(ms + RMS_EPS) * gsub_ref[...] * (1.0 - lam_init)
                o_ref[rows, :] = o_t.T.astype(o_ref.dtype)


def _attention_specs(nq):
    nst = nq // ATT_QBLOCKS
    q_spec = pl.BlockSpec((ATT_QBLOCKS * MOBA_BLOCK, LANES), lambda b, p, i: (b * nst + i, p))
    k_spec = pl.BlockSpec((nq * MOBA_BLOCK, LANES), lambda b, p, i: (b, p))
    vt_spec = pl.BlockSpec((nq, LANES, MOBA_BLOCK), lambda b, p, i: (b, p, 0))
    return q_spec, k_spec, vt_spec


def _moba_attention(q, k, vt, kmean, slopes, batch, seq):
    nq = seq // MOBA_BLOCK
    q_spec, k_spec, vt_spec = _attention_specs(nq)
    return pl.pallas_call(
        _moba_kernel,
        grid=(batch, MOBA_HEADS // 2, nq // ATT_QBLOCKS),
        in_specs=[
            pl.BlockSpec(memory_space=pltpu.SMEM),
            q_spec, k_spec, vt_spec,
            pl.BlockSpec((SUBLANES, LANES), lambda b, p, i: (b, p)),
        ],
        out_specs=q_spec,
        out_shape=jax.ShapeDtypeStruct(q.shape, jnp.bfloat16),
        compiler_params=pltpu.CompilerParams(
            dimension_semantics=("arbitrary", "arbitrary", "arbitrary"), vmem_limit_bytes=VMEM_LIMIT),
        name="moba",
    )(slopes, q, k, vt, kmean)


def _diff_attention(q, k, vt, lam_p, gsub, slopes, lam_init, batch, seq):
    nq = seq // MOBA_BLOCK
    q_spec, k_spec, vt_spec = _attention_specs(nq)
    return pl.pallas_call(
        functools.partial(_diff_kernel, lam_init=lam_init),
        grid=(batch, DIFF_HEADS, nq // ATT_QBLOCKS),
        in_specs=[
            pl.BlockSpec(memory_space=pltpu.SMEM),
            pl.BlockSpec((4, HEAD_DIM), lambda b, p, i: (0, 0)),
            q_spec, k_spec, vt_spec,
            pl.BlockSpec((LANES, MOBA_BLOCK), lambda b, p, i: (0, 0)),
        ],
        out_specs=q_spec,
        out_shape=jax.ShapeDtypeStruct(q.shape, jnp.bfloat16),
        compiler_params=pltpu.CompilerParams(
            dimension_semantics=("arbitrary", "arbitrary", "arbitrary"), vmem_limit_bytes=VMEM_LIMIT),
        name="diff",
    )(slopes, lam_p, q, k, vt, gsub)


def _knockout_topk(sc, k, payload=None):
    n = sc.shape[0]
    row = lax.broadcasted_iota(jnp.int32, sc.shape, 0).astype(jnp.float32)
    vals, picks = [], []
    for _ in range(k):
        m = jnp.max(sc, axis=0, keepdims=True)
        idx = jnp.min(jnp.where(sc == m, row, float(n)), axis=0, keepdims=True)
        hit = row == idx
        vals.append(m)
        picks.append(idx if payload is None else jnp.sum(jnp.where(hit, payload, 0.0), axis=0, keepdims=True))
        sc = jnp.where(hit, NEG_INF, sc)
    return vals, picks


def _mix_kernel(x_ref, am_ref, ad_ref, wo_ref, g2_ref, wq_ref, keys_ref,
                x1_ref, xn_ref, exp_ref, gate_ref):
    x1 = x_ref[...] + _dot(am_ref[...], wo_ref[0:512, :]) + _dot(ad_ref[...], wo_ref[512:1024, :])
    x1_ref[...] = x1
    ms = jnp.mean(x1 * x1, axis=-1, keepdims=True)
    xn = x1 * lax.rsqrt(ms + RMS_EPS) * g2_ref[...]
    xn_ref[...] = xn
    qq = _dot(xn.astype(jnp.bfloat16), wq_ref[...]).astype(jnp.bfloat16)
    tm = qq.shape[0]
    exp_rows, gate_rows = [], []
    for h in range(PEER_HEADS):
        tops = []
        for half in range(2):
            s = h * 2 + half
            sc = _dot_nt(keys_ref[s], qq[:, s * LANES:(s + 1) * LANES])
            tops.append(_knockout_topk(sc, PEER_TOPK))
        (s1, i1), (s2, i2) = tops
        s2a = jnp.concatenate(s2, axis=0)
        i2a = jnp.concatenate(i2, axis=0)
        cand, cidx = [], []
        for a in range(PEER_TOPK):
            nb = PEER_TOPK // (a + 1)
            cand.append(s1[a] + s2a[0:nb])
            cidx.append(i1[a] * float(PEER_N_KEYS) + i2a[0:nb])
        n_cand = sum(c.shape[0] for c in cand)
        pad = -n_cand % SUBLANES
        cand.append(jnp.full((pad, tm), NEG_INF, jnp.float32))
        cidx.append(jnp.zeros((pad, tm), jnp.float32))
        best, experts = _knockout_topk(jnp.concatenate(cand, axis=0), PEER_TOPK,
                                       payload=jnp.concatenate(cidx, axis=0))
        b = jnp.concatenate(best, axis=0)
        e = jnp.exp(b - b[0:1])
        gate_rows.append(e / jnp.sum(e, axis=0, keepdims=True))
        exp_rows.append(jnp.concatenate(experts, axis=0))
    exp_ref[...] = jnp.concatenate(exp_rows, axis=0).T.astype(jnp.int32)
    gate_ref[...] = jnp.concatenate(gate_rows, axis=0).T


def _mix(x2, am, ad, wo, g2, wq, keys):
    T = x2.shape[0]
    row = lambda i: (i, 0)
    const = lambda i: (0, 0)
    return pl.pallas_call(
        _mix_kernel,
        grid=(T // MIX_TM,),
        in_specs=[
            pl.BlockSpec((MIX_TM, 1024), row),
            pl.BlockSpec((MIX_TM, 512), row),
            pl.BlockSpec((MIX_TM, 512), row),
            pl.BlockSpec((1024, 1024), const),
            pl.BlockSpec((1, 1024), const),
            pl.BlockSpec((1024, 2048), const),
            pl.BlockSpec((2 * PEER_HEADS, PEER_N_KEYS, LANES), lambda i: (0, 0, 0)),
        ],
        out_specs=[
            pl.BlockSpec((MIX_TM, 1024), row),
            pl.BlockSpec((MIX_TM, 1024), row),
            pl.BlockSpec((MIX_TM, LANES), row),
            pl.BlockSpec((MIX_TM, LANES), row),
        ],
        out_shape=[
            jax.ShapeDtypeStruct((T, 1024), jnp.float32),
            jax.ShapeDtypeStruct((T, 1024), jnp.float32),
            jax.ShapeDtypeStruct((T, LANES), jnp.int32),
            jax.ShapeDtypeStruct((T, LANES), jnp.float32),
        ],
        compiler_params=pltpu.CompilerParams(dimension_semantics=("arbitrary",), vmem_limit_bytes=VMEM_LIMIT),
        name="mix",
    )(x2, am, ad, wo, g2, wq, keys)


def _pack_kernel(t_ref, o_ref):
    t = t_ref[...]
    lo = pltpu.bitcast(t[:, 0:512].astype(jnp.bfloat16).astype(jnp.float32), jnp.int32)
    hi = pltpu.bitcast(t[:, 512:1024].astype(jnp.bfloat16).astype(jnp.float32), jnp.int32)
    o_ref[...] = lax.shift_right_logical(lo, 16) | (hi & jnp.int32(-65536))


def _pack_table(tab):
    n = tab.shape[0]
    return pl.pallas_call(
        _pack_kernel,
        grid=(n // PACK_ROWS,),
        in_specs=[pl.BlockSpec((PACK_ROWS, 1024), lambda i: (i, 0))],
        out_specs=pl.BlockSpec((PACK_ROWS, 512), lambda i: (i, 0)),
        out_shape=jax.ShapeDtypeStruct((n, 512), jnp.int32),
        compiler_params=pltpu.CompilerParams(dimension_semantics=("arbitrary",)),
        name="pack",
    )(tab)


def _decode(w):
    lo = pltpu.bitcast(w << 16, jnp.float32)
    hi = pltpu.bitcast(w & jnp.int32(-65536), jnp.float32)
    return lo, hi


def _peer_u_kernel(exp_ref, tab_ref, x_ref, g_ref, act_ref, q0_ref, q1_ref, qs_ref):
    ntok, nsel = exp_ref.shape
    ones = jnp.ones((SUBLANES, LANES), jnp.bfloat16)

    def products(t, q_ref):
        x = x_ref[t]
        xlo = x[0:4]
        xhi = x[4:8]
        for e in range(nsel):
            lo, hi = _decode(tab_ref[exp_ref[t, e]])
            q_ref[pl.ds(4 * e, 4), :] = lo * xlo + hi * xhi

    def fold(t, q_ref):
        q = q_ref[pl.ds(0, nsel, stride=4), :]
        for s in range(1, 4):
            q = q + q_ref[pl.ds(s, nsel, stride=4), :]
        qs_ref[pl.ds(pl.multiple_of(t * nsel, nsel), nsel), :] = q

    def gather_group(g):
        for k in range(PEER_GROUP):
            q_ref = (q0_ref, q1_ref)[k % 2]
            products(g * PEER_GROUP + k, q_ref)
            fold(g * PEER_GROUP + k, q_ref)

    def finish_group(g):
        rows8 = PEER_GROUP * nsel
        q_hi, q_lo = _split_bf16(qs_ref[pl.ds(pl.multiple_of(g * rows8, rows8), rows8), :])
        row = (_dot_nt(ones, q_hi) + _dot_nt(ones, q_lo))[0:1]
        h = jnp.concatenate([row[:, k * nsel:(k + 1) * nsel] for k in range(PEER_GROUP)], axis=0)
        gelu = 0.5 * h * (1.0 + lax.erf(h * (1.0 / math.sqrt(2.0))))
        rows = pl.ds(pl.multiple_of(g * PEER_GROUP, PEER_GROUP), PEER_GROUP)
        act_ref[rows, :] = gelu * g_ref[rows, :]

    ngroups = ntok // PEER_GROUP
    gather_group(0)

    def body(g, carry):
        finish_group(g - 1)
        gather_group(g)
        return carry

    lax.fori_loop(1, ngroups, body, 0)
    finish_group(ngroups - 1)


def _sc_peer_v(vtab, idx, actrep, x1, tok0, ntok):
    d = x1.shape[1]
    nsel = idx.shape[1]
    info = plsc.get_sparse_core_info()
    nw = info.num_cores * info.num_subcores
    per_w = ntok // nw
    nchunk = nsel // SC_ROWS
    nitem = SC_TB * nchunk
    assert info.num_lanes == SC_LANES and ntok % (nw * SC_TB) == 0 and nitem % SC_NBUF == 0
    mesh = plsc.VectorSubcoreMesh(core_axis_name="c", subcore_axis_name="s")

    @functools.partial(
        pl.kernel, mesh=mesh,
        out_type=jax.ShapeDtypeStruct((ntok, d), jnp.float32),
        scratch_types=[
            pltpu.VMEM((SC_TB, nsel), jnp.int32),
            pltpu.VMEM((SC_TB, nsel * SC_LANES), jnp.float32),
            pltpu.VMEM((SC_TB, d), jnp.float32),
        ] + [pltpu.VMEM((SC_ROWS, d), jnp.float32)] * SC_NBUF + [pltpu.SemaphoreType.DMA] * SC_NBUF,
        name="sc_peer_v",
    )
    def k(vtab_hbm, idx_hbm, act_hbm, x1_hbm, out_hbm, idx_v, act_v, acc_v, *bufs_sems):
        bufs, sems = bufs_sems[:SC_NBUF], bufs_sems[SC_NBUF:]
        wid = lax.axis_index("s") * info.num_cores + lax.axis_index("c")

        def gather(item, rows, sem):
            tl = item // nchunk
            ch = item % nchunk
            return pltpu.make_async_copy(vtab_hbm.at[idx_v.at[tl, pl.ds(ch * SC_ROWS, SC_ROWS)]], rows, sem)

        def accumulate(item, rows):
            tl = item // nchunk
            r0 = (item % nchunk) * SC_ROWS
            a = [act_v[tl, pl.ds((r0 + e) * SC_LANES, SC_LANES)] for e in range(SC_ROWS)]

            @plsc.parallel_loop(0, d // SC_LANES, 1, unroll=2)
            def _(c):
                col = pl.ds(c * SC_LANES, SC_LANES)
                terms = [a[e] * rows[e, col] for e in range(SC_ROWS)]
                while len(terms) > 1:
                    terms = [terms[j] + terms[j + 1] for j in range(0, len(terms), 2)]
                acc_v[tl, col] = acc_v[tl, col] + terms[0]

        @pl.loop(0, per_w // SC_TB)
        def _(bi):
            loc = pl.multiple_of(wid * per_w + bi * SC_TB, SC_TB)
            src = pl.ds(tok0 + loc, SC_TB)
            pltpu.sync_copy(idx_hbm.at[src], idx_v)
            pltpu.sync_copy(act_hbm.at[pl.ds(loc, SC_TB)], act_v)
            pltpu.sync_copy(x1_hbm.at[src], acc_v)
            for j in range(SC_NBUF - 1):
                gather(j, bufs[j], sems[j]).start()

            @pl.loop(0, nitem // SC_NBUF)
            def _(p):
                for j in range(SC_NBUF):
                    item = p * SC_NBUF + j
                    nxt = (j + SC_NBUF - 1) % SC_NBUF

                    @pl.when(item + SC_NBUF - 1 < nitem)
                    def _():
                        gather(item + SC_NBUF - 1, bufs[nxt], sems[nxt]).start()

                    gather(item, bufs[j], sems[j]).wait()
                    accumulate(item, bufs[j])

            pltpu.sync_copy(acc_v, out_hbm.at[pl.ds(loc, SC_TB)])

    return k(vtab, idx, actrep, x1)


def _peer_v_kernel(exp_ref, act_ref, tab_ref, x_ref, o_ref):
    nsel = exp_ref.shape[1]

    def tok(t, carry):
        acc_lo = jnp.zeros((4, LANES), jnp.float32)
        acc_hi = jnp.zeros((4, LANES), jnp.float32)
        for e in range(nsel):
            a = act_ref[t, e]
            lo, hi = _decode(tab_ref[exp_ref[t, e]])
            acc_lo = acc_lo + a * lo
            acc_hi = acc_hi + a * hi
        o_ref[t] = x_ref[t] + jnp.concatenate([acc_lo, acc_hi], axis=0)
        return carry

    lax.fori_loop(0, x_ref.shape[0], tok, 0)


def _peer_experts(xn3, x1, experts, gates, u_packed, v_table, v_packed):
    T, nsel = experts.shape
    n_exp = u_packed.shape[0]
    d = x1.shape[1]
    unit = T // sum(u for u, _ in PEER_PLAN)
    assert nsel == LANES and PEER_TT % PEER_GROUP == 0 and unit % PEER_TT == 0
    tab_spec = pl.BlockSpec((n_exp, 4, LANES), lambda i: (0, 0, 0), pipeline_mode=pl.Buffered(1))
    cp = pltpu.CompilerParams(dimension_semantics=("arbitrary",), vmem_limit_bytes=VMEM_LIMIT)
    x13 = x1.reshape(T, SUBLANES, LANES)
    outs = []
    tok0 = 0
    for units, on_sc in PEER_PLAN:
        tc = units * unit
        steps = tc // PEER_TT
        off = tok0 // PEER_TT
        row = lambda i, off=off: (i + off, 0)
        row3 = lambda i, off=off: (i + off, 0, 0)
        act = pl.pallas_call(
            _peer_u_kernel,
            grid=(steps,),
            in_specs=[pl.BlockSpec((PEER_TT, nsel), row, memory_space=pltpu.SMEM), tab_spec,
                      pl.BlockSpec((PEER_TT, SUBLANES, LANES), row3), pl.BlockSpec((PEER_TT, nsel), row)],
            out_specs=pl.BlockSpec((PEER_TT, nsel), lambda i: (i, 0)),
            out_shape=jax.ShapeDtypeStruct((tc, nsel), jnp.float32),
            scratch_shapes=[pltpu.VMEM((4 * nsel, LANES), jnp.float32), pltpu.VMEM((4 * nsel, LANES), jnp.float32),
                            pltpu.VMEM((PEER_TT * nsel, LANES), jnp.float32)],
            compiler_params=cp,
            name="peer_u",
        )(experts, u_packed, xn3, gates)
        if on_sc:
            actrep = jnp.broadcast_to(act[:, :, None], (tc, nsel, SC_LANES)).reshape(tc, nsel * SC_LANES)
            outs.append(_sc_peer_v(v_table, experts, actrep, x1, tok0, tc))
        else:
            out3 = pl.pallas_call(
                _peer_v_kernel,
                grid=(steps,),
                in_specs=[pl.BlockSpec((PEER_TT, nsel), row, memory_space=pltpu.SMEM),
                          pl.BlockSpec((PEER_TT, nsel), lambda i: (i, 0), memory_space=pltpu.SMEM),
                          tab_spec, pl.BlockSpec((PEER_TT, SUBLANES, LANES), row3)],
                out_specs=pl.BlockSpec((PEER_TT, SUBLANES, LANES), lambda i: (i, 0, 0)),
                out_shape=jax.ShapeDtypeStruct((tc, SUBLANES, LANES), jnp.float32),
                compiler_params=cp,
                name="peer_v",
            )(experts, act, v_packed, x13)
            outs.append(out3.reshape(tc, d))
        tok0 += tc
    return jnp.concatenate(outs, axis=0)


def _alibi_slopes(n):
    return 2.0 ** (-8.0 * np.arange(1, n + 1) / n)


def _lambda_init(layer):
    return 0.8 - 0.6 * math.exp(-0.3 * layer)


def kernel(x, norm1_g, w_in, moba_qk_norm_g, diff_qk_norm_g, diff_lambda, diff_subln_g, w_out, norm2_g,
           peer_w_query, peer_sub_keys, peer_u, peer_v):
    b, s, d = x.shape
    T = b * s
    depth = w_in.shape[0]
    mw = MOBA_HEADS * HEAD_DIM
    assert d == 1024 and s % PROJ_TM == 0 and mw == 512 and DIFF_HEADS * 2 * HEAD_DIM == 512
    f32, bf16 = jnp.float32, jnp.bfloat16
    scale = HEAD_DIM ** -0.5
    grp = lax.broadcasted_iota(jnp.int32, (LANES, LANES), 0) // HEAD_DIM
    gsum = (grp == grp.T).astype(bf16)
    moba_slopes = jnp.asarray(_alibi_slopes(MOBA_HEADS).reshape(MOBA_HEADS // 2, 2), f32)
    diff_slopes = jnp.asarray(np.repeat(_alibi_slopes(DIFF_HEADS)[:, None], 2, axis=1), f32)
    nblk = PROJ_TM // MOBA_BLOCK
    x2 = x.reshape(T, d)
    for l in range(depth):
        w = w_in[l]
        wqk = jnp.concatenate([w[:, 0:2 * mw], w[:, 3 * mw:3 * mw + 1024]], axis=1).astype(bf16)
        wvt = jnp.concatenate([w[:, 2 * mw:3 * mw], w[:, 3 * mw + 1024:]], axis=1).T.astype(bf16)
        mg, dg = moba_qk_norm_g[l].astype(f32), diff_qk_norm_g[l].astype(f32)
        gqk = jnp.concatenate([jnp.tile(mg[0] * scale, 8), jnp.tile(mg[1], 8),
                               jnp.tile(dg[0] * scale, 8), jnp.tile(dg[1], 8)])[None, :]
        g1 = norm1_g[l][None, :].astype(f32)
        g2 = norm2_g[l][None, :].astype(f32)
        gsub = jnp.broadcast_to(diff_subln_g[l].astype(f32)[:, None], (LANES, MOBA_BLOCK))
        lam_p = diff_lambda[l].astype(f32)
        wo, wq = w_out[l].astype(bf16), peer_w_query[l].astype(bf16)
        keys = peer_sub_keys[l].reshape(2 * PEER_HEADS, PEER_N_KEYS, LANES).astype(bf16)
        u_packed = _pack_table(peer_u[l]).reshape(peer_u.shape[1], 4, LANES)
        v_packed = _pack_table(peer_v[l]).reshape(peer_v.shape[1], 4, LANES)
        v_table = peer_v[l].astype(f32)
        mq, mk, dq, dk, mvt, dvt, kmean = _project(x2, g1, wqk, wvt, gqk, gsum)
        kmean = kmean.reshape(T // PROJ_TM, SUBLANES, 512)[:, :nblk].reshape(T // MOBA_BLOCK, 512)
        moba_o = _moba_attention(mq, mk, mvt, kmean, moba_slopes, b, s)
        diff_o = _diff_attention(dq, dk, dvt, lam_p, gsub, diff_slopes, _lambda_init(l), b, s)
        x1, xn, experts, gates = _mix(x2, moba_o, diff_o, wo, g2, wq, keys)
        x2 = _peer_experts(xn.reshape(T, SUBLANES, LANES), x1, experts, gates, u_packed, v_table, v_packed)
    return x2.reshape(b, s, d)
```

```python
import functools
import math

import jax
import jax.numpy as jnp
import numpy as np
from jax import lax
from jax.experimental import pallas as pl
from jax.experimental.pallas import tpu as pltpu
from jax.experimental.pallas import tpu_sc as plsc

HEAD_DIM = 64
MOBA_HEADS = 8
MOBA_BLOCK = 256
MOBA_TOPK = 3
DIFF_HEADS = 4
PEER_HEADS = 8
PEER_N_KEYS = 128
PEER_TOPK = 16
RMS_EPS = 1e-6

LANES = 128
SUBLANES = 8
VMEM_LIMIT = 48 * 1024 * 1024

PROJ_TM = 512
MIX_TM = 256
PEER_TT = 64
PACK_ROWS = 512
ATT_QBLOCKS = 2
PEER_GROUP = 8
PEER_PLAN = ((1, True), (2, True), (2, True), (2, True), (2, True), (2, True), (2, True), (2, True), (1, False))
SC_LANES = 16
SC_ROWS = 16
SC_NBUF = 4
SC_TB = 8

NEG_INF = float("-inf")
_NT = (((1,), (1,)), ((), ()))


def _dot(a, b):
    return jnp.dot(a, b, preferred_element_type=jnp.float32)


def _dot_nt(a, b):
    return lax.dot_general(a, b, _NT, preferred_element_type=jnp.float32)


def _split_bf16(x):
    hi = x.astype(jnp.bfloat16)
    lo = (x - hi.astype(jnp.float32)).astype(jnp.bfloat16)
    return hi, lo


def _proj_kernel(x_ref, g1_ref, wqk_ref, wvt_ref, gqk_ref, gsum_ref,
                 mq_ref, mk_ref, dq_ref, dk_ref, mvt_ref, dvt_ref, kmean_ref):
    x = x_ref[...]
    ms = jnp.mean(x * x, axis=-1, keepdims=True)
    h = (x * lax.rsqrt(ms + RMS_EPS) * g1_ref[...]).astype(jnp.bfloat16)
    gsum = gsum_ref[...]
    outs = (mq_ref, mk_ref, dq_ref, dk_ref)
    for sec in range(4):
        for half in range(2):
            c0 = sec * 512 + half * 256
            p = _dot(h, wqk_ref[:, c0:c0 + 256])
            for s in range(2):
                ps = p[:, s * LANES:(s + 1) * LANES]
                sq_hi, sq_lo = _split_bf16(ps * ps)
                ssq = _dot(sq_hi, gsum) + _dot(sq_lo, gsum)
                g = gqk_ref[:, c0 + s * LANES:c0 + (s + 1) * LANES]
                y = ps * lax.rsqrt(ssq * (1.0 / HEAD_DIM) + RMS_EPS) * g
                l0 = half * 256 + s * LANES
                outs[sec][:, l0:l0 + LANES] = y.astype(jnp.bfloat16)
                if sec == 1:
                    for blk in range(PROJ_TM // MOBA_BLOCK):
                        kmean_ref[blk:blk + 1, l0:l0 + LANES] = jnp.mean(
                            y[blk * MOBA_BLOCK:(blk + 1) * MOBA_BLOCK], axis=0, keepdims=True)
    kmean_ref[PROJ_TM // MOBA_BLOCK:, :] = jnp.zeros((SUBLANES - PROJ_TM // MOBA_BLOCK, 512), jnp.float32)
    vt = _dot_nt(wvt_ref[...], h).astype(jnp.bfloat16)
    for blk in range(PROJ_TM // MOBA_BLOCK):
        mvt_ref[blk] = vt[0:512, blk * MOBA_BLOCK:(blk + 1) * MOBA_BLOCK]
        dvt_ref[blk] = vt[512:1024, blk * MOBA_BLOCK:(blk + 1) * MOBA_BLOCK]


def _project(x2, g1, wqk, wvt, gqk, gsum):
    T = x2.shape[0]
    nt = T // PROJ_TM
    nblk = PROJ_TM // MOBA_BLOCK
    row = lambda i: (i, 0)
    const = lambda i: (0, 0)
    qk_shape = jax.ShapeDtypeStruct((T, 512), jnp.bfloat16)
    vt_shape = jax.ShapeDtypeStruct((T // MOBA_BLOCK, 512, MOBA_BLOCK), jnp.bfloat16)
    return pl.pallas_call(
        _proj_kernel,
        grid=(nt,),
        in_specs=[
            pl.BlockSpec((PROJ_TM, 1024), row),
            pl.BlockSpec((1, 1024), const),
            pl.BlockSpec((1024, 2048), const),
            pl.BlockSpec((1024, 1024), const),
            pl.BlockSpec((1, 2048), const),
            pl.BlockSpec((LANES, LANES), const),
        ],
        out_specs=[
            pl.BlockSpec((PROJ_TM, 512), row),
            pl.BlockSpec((PROJ_TM, 512), row),
            pl.BlockSpec((PROJ_TM, 512), row),
            pl.BlockSpec((PROJ_TM, 512), row),
            pl.BlockSpec((nblk, 512, MOBA_BLOCK), lambda i: (i, 0, 0)),
            pl.BlockSpec((nblk, 512, MOBA_BLOCK), lambda i: (i, 0, 0)),
            pl.BlockSpec((SUBLANES, 512), row),
        ],
        out_shape=[qk_shape, qk_shape, qk_shape, qk_shape, vt_shape, vt_shape,
                   jax.ShapeDtypeStruct((nt * SUBLANES, 512), jnp.float32)],
        compiler_params=pltpu.CompilerParams(dimension_semantics=("arbitrary",), vmem_limit_bytes=VMEM_LIMIT),
        name="proj",
    )(x2, g1, wqk, wvt, gqk, gsum)


def _rel_qk():
    qc = lax.broadcasted_iota(jnp.int32, (MOBA_BLOCK, MOBA_BLOCK), 1)
    kr = lax.broadcasted_iota(jnp.int32, (MOBA_BLOCK, MOBA_BLOCK), 0)
    return qc - kr


def _attend(q_h, k_ref, vt_load, i, slope, rel, selbias):
    nb = rel.astype(jnp.float32) * (-slope)
    ts, rowbs, colmax = [], [], []
    for j in range(i + 1):
        t = _dot_nt(k_ref[j * MOBA_BLOCK:(j + 1) * MOBA_BLOCK, :], q_h) + nb
        if j == i:
            t = jnp.where(rel >= 0, t, NEG_INF)
            rowb = jnp.zeros((1, MOBA_BLOCK), jnp.float32)
        else:
            rowb = jnp.full((1, MOBA_BLOCK), (i - j) * MOBA_BLOCK, jnp.float32) * (-slope)
            if selbias is not None:
                rowb = rowb + selbias[j:j + 1, :]
        ts.append(t)
        rowbs.append(rowb)
        colmax.append(jnp.max(t, axis=0, keepdims=True) + rowb)
    m = functools.reduce(jnp.maximum, colmax)
    l, acc = None, None
    for j in range(i + 1):
        p = jnp.exp(ts[j] - (m - rowbs[j]))
        lj = jnp.sum(p, axis=0, keepdims=True)
        aj = _dot(vt_load(j), p.astype(jnp.bfloat16))
        l = lj if l is None else l + lj
        acc = aj if acc is None else acc + aj
    return acc / l


def _head_queries(q2):
    lane = lax.broadcasted_iota(jnp.int32, q2.shape, 1)
    return [jnp.where((lane >= hh * HEAD_DIM) & (lane < (hh + 1) * HEAD_DIM), q2, jnp.zeros_like(q2))
            for hh in range(2)]


def _block_selection(gate, i):
    blk = lax.broadcasted_iota(jnp.int32, gate.shape, 0)
    cnt = jnp.zeros(gate.shape, jnp.int32)
    for mblk in range(i):
        gm = gate[mblk:mblk + 1, :]
        beats = (gm > gate) | ((gm == gate) & (mblk < blk))
        cnt = cnt + jnp.where(beats, 1, 0)
    return jnp.where((blk < i) & (cnt < MOBA_TOPK), 0.0, NEG_INF)


def _moba_kernel(slopes_ref, q_ref, k_ref, vt_ref, kmean_ref, o_ref):
    pair = pl.program_id(1)
    step = pl.program_id(2)
    nq = vt_ref.shape[0]
    rel = _rel_qk()
    km_hi, km_lo = _split_bf16(kmean_ref[...])
    for st in range(nq // ATT_QBLOCKS):
        @pl.when(step == st)
        def _(st=st):
            for u in range(ATT_QBLOCKS):
                i = st * ATT_QBLOCKS + u
                rows = slice(u * MOBA_BLOCK, (u + 1) * MOBA_BLOCK)
                outs = []
                for hh, q_h in enumerate(_head_queries(q_ref[rows, :])):
                    gate = _dot_nt(km_hi, q_h) + _dot_nt(km_lo, q_h)
                    vt_load = lambda j, hh=hh: vt_ref[j, hh * HEAD_DIM:(hh + 1) * HEAD_DIM, :]
                    outs.append(_attend(q_h, k_ref, vt_load, i, slopes_ref[pair, hh], rel,
                                        _block_selection(gate, i)))
                o_t = jnp.concatenate(outs, axis=0)
                o_ref[rows, :] = o_t.T.astype(o_ref.dtype)


def _diff_kernel(slopes_ref, lam_ref, q_ref, k_ref, vt_ref, gsub_ref, o_ref, *, lam_init):
    head = pl.program_id(1)
    step = pl.program_id(2)
    nq = vt_ref.shape[0]
    rel = _rel_qk()
    slope = slopes_ref[head, 0]
    lp = lam_ref[...]
    lam = (jnp.exp(jnp.sum(lp[0:1] * lp[1:2], axis=-1, keepdims=True))
           - jnp.exp(jnp.sum(lp[2:3] * lp[3:4], axis=-1, keepdims=True)) + lam_init)
    vt_load = lambda j: vt_ref[j]
    for st in range(nq // ATT_QBLOCKS):
        @pl.when(step == st)
        def _(st=st):
            for u in range(ATT_QBLOCKS):
                i = st * ATT_QBLOCKS + u
                rows = slice(u * MOBA_BLOCK, (u + 1) * MOBA_BLOCK)
                parts = [_attend(q_h, k_ref, vt_load, i, slope, rel, None)
                         for q_h in _head_queries(q_ref[rows, :])]
                o_t = parts[0] - lam * parts[1]
                ms = jnp.mean(o_t * o_t, axis=0, keepdims=True)
                o_t = o_t * lax.rsqrt(ms + RMS_EPS) * gsub_ref[...] * (1.0 - lam_init)
                o_ref[rows, :] = o_t.T.astype(o_ref.dtype)


def _attention_specs(nq):
    nst = nq // ATT_QBLOCKS
    q_spec = pl.BlockSpec((ATT_QBLOCKS * MOBA_BLOCK, LANES), lambda b, p, i: (b * nst + i, p))
    k_spec = pl.BlockSpec((nq * MOBA_BLOCK, LANES), lambda b, p, i: (b, p))
    vt_spec = pl.BlockSpec((nq, LANES, MOBA_BLOCK), lambda b, p, i: (b, p, 0))
    return q_spec, k_spec, vt_spec


def _moba_attention(q, k, vt, kmean, slopes, batch, seq):
    nq = seq // MOBA_BLOCK
    q_spec, k_spec, vt_spec = _attention_specs(nq)
    return pl.pallas_call(
        _moba_kernel,
        grid=(batch, MOBA_HEADS // 2, nq // ATT_QBLOCKS),
        in_specs=[
            pl.BlockSpec(memory_space=pltpu.SMEM),
            q_spec, k_spec, vt_spec,
            pl.BlockSpec((SUBLANES, LANES), lambda b, p, i: (b, p)),
        ],
        out_specs=q_spec,
        out_shape=jax.ShapeDtypeStruct(q.shape, jnp.bfloat16),
        compiler_params=pltpu.CompilerParams(
            dimension_semantics=("arbitrary", "arbitrary", "arbitrary"), vmem_limit_bytes=VMEM_LIMIT),
        name="moba",
    )(slopes, q, k, vt, kmean)


def _diff_attention(q, k, vt, lam_p, gsub, slopes, lam_init, batch, seq):
    nq = seq // MOBA_BLOCK
    q_spec, k_spec, vt_spec = _attention_specs(nq)
    return pl.pallas_call(
        functools.partial(_diff_kernel, lam_init=lam_init),
        grid=(batch, DIFF_HEADS, nq // ATT_QBLOCKS),
        in_specs=[
            pl.BlockSpec(memory_space=pltpu.SMEM),
            pl.BlockSpec((4, HEAD_DIM), lambda b, p, i: (0, 0)),
            q_spec, k_spec, vt_spec,
            pl.BlockSpec((LANES, MOBA_BLOCK), lambda b, p, i: (0, 0)),
        ],
        out_specs=q_spec,
        out_shape=jax.ShapeDtypeStruct(q.shape, jnp.bfloat16),
        compiler_params=pltpu.CompilerParams(
            dimension_semantics=("arbitrary", "arbitrary", "arbitrary"), vmem_limit_bytes=VMEM_LIMIT),
        name="diff",
    )(slopes, lam_p, q, k, vt, gsub)


def _knockout_topk(sc, k, payload=None):
    n = sc.shape[0]
    row = lax.broadcasted_iota(jnp.int32, sc.shape, 0).astype(jnp.float32)
    vals, picks = [], []
    for _ in range(k):
        m = jnp.max(sc, axis=0, keepdims=True)
        idx = jnp.min(jnp.where(sc == m, row, float(n)), axis=0, keepdims=True)
        hit = row == idx
        vals.append(m)
        picks.append(idx if payload is None else jnp.sum(jnp.where(hit, payload, 0.0), axis=0, keepdims=True))
        sc = jnp.where(hit, NEG_INF, sc)
    return vals, picks


def _mix_kernel(x_ref, am_ref, ad_ref, wo_ref, g2_ref, wq_ref, keys_ref,
                x1_ref, xn_ref, exp_ref, gate_ref):
    x1 = x_ref[...] + _dot(am_ref[...], wo_ref[0:512, :]) + _dot(ad_ref[...], wo_ref[512:1024, :])
    x1_ref[...] = x1
    ms = jnp.mean(x1 * x1, axis=-1, keepdims=True)
    xn = x1 * lax.rsqrt(ms + RMS_EPS) * g2_ref[...]
    xn_ref[...] = xn
    qq = _dot(xn.astype(jnp.bfloat16), wq_ref[...]).astype(jnp.bfloat16)
    tm = qq.shape[0]
    exp_rows, gate_rows = [], []
    for h in range(PEER_HEADS):
        tops = []
        for half in range(2):
            s = h * 2 + half
            sc = _dot_nt(keys_ref[s], qq[:, s * LANES:(s + 1) * LANES])
            tops.append(_knockout_topk(sc, PEER_TOPK))
        (s1, i1), (s2, i2) = tops
        s2a = jnp.concatenate(s2, axis=0)
        i2a = jnp.concatenate(i2, axis=0)
        cand, cidx = [], []
        for a in range(PEER_TOPK):
            nb = PEER_TOPK // (a + 1)
            cand.append(s1[a] + s2a[0:nb])
            cidx.append(i1[a] * float(PEER_N_KEYS) + i2a[0:nb])
        n_cand = sum(c.shape[0] for c in cand)
        pad = -n_cand % SUBLANES
        cand.append(jnp.full((pad, tm), NEG_INF, jnp.float32))
        cidx.append(jnp.zeros((pad, tm), jnp.float32))
        best, experts = _knockout_topk(jnp.concatenate(cand, axis=0), PEER_TOPK,
                                       payload=jnp.concatenate(cidx, axis=0))
        b = jnp.concatenate(best, axis=0)
        e = jnp.exp(b - b[0:1])
        gate_rows.append(e / jnp.sum(e, axis=0, keepdims=True))
        exp_rows.append(jnp.concatenate(experts, axis=0))
    exp_ref[...] = jnp.concatenate(exp_rows, axis=0).T.astype(jnp.int32)
    gate_ref[...] = jnp.concatenate(gate_rows, axis=0).T


def _mix(x2, am, ad, wo, g2, wq, keys):
    T = x2.shape[0]
    row = lambda i: (i, 0)
    const = lambda i: (0, 0)
    return pl.pallas_call(
        _mix_kernel,
        grid=(T // MIX_TM,),
        in_specs=[
            pl.BlockSpec((MIX_TM, 1024), row),
            pl.BlockSpec((MIX_TM, 512), row),
            pl.BlockSpec((MIX_TM, 512), row),
            pl.BlockSpec((1024, 1024), const),
            pl.BlockSpec((1, 1024), const),
            pl.BlockSpec((1024, 2048), const),
            pl.BlockSpec((2 * PEER_HEADS, PEER_N_KEYS, LANES), lambda i: (0, 0, 0)),
        ],
        out_specs=[
            pl.BlockSpec((MIX_TM, 1024), row),
            pl.BlockSpec((MIX_TM, 1024), row),
            pl.BlockSpec((MIX_TM, LANES), row),
            pl.BlockSpec((MIX_TM, LANES), row),
        ],
        out_shape=[
            jax.ShapeDtypeStruct((T, 1024), jnp.float32),
            jax.ShapeDtypeStruct((T, 1024), jnp.float32),
            jax.ShapeDtypeStruct((T, LANES), jnp.int32),
            jax.ShapeDtypeStruct((T, LANES), jnp.float32),
        ],
        compiler_params=pltpu.CompilerParams(dimension_semantics=("arbitrary",), vmem_limit_bytes=VMEM_LIMIT),
        name="mix",
    )(x2, am, ad, wo, g2, wq, keys)


def _pack_kernel(t_ref, o_ref):
    t = t_ref[...]
    lo = pltpu.bitcast(t[:, 0:512].astype(jnp.bfloat16).astype(jnp.float32), jnp.int32)
    hi = pltpu.bitcast(t[:, 512:1024].astype(jnp.bfloat16).astype(jnp.float32), jnp.int32)
    o_ref[...] = lax.shift_right_logical(lo, 16) | (hi & jnp.int32(-65536))


def _pack_table(tab):
    n = tab.shape[0]
    return pl.pallas_call(
        _pack_kernel,
        grid=(n // PACK_ROWS,),
        in_specs=[pl.BlockSpec((PACK_ROWS, 1024), lambda i: (i, 0))],
        out_specs=pl.BlockSpec((PACK_ROWS, 512), lambda i: (i, 0)),
        out_shape=jax.ShapeDtypeStruct((n, 512), jnp.int32),
        compiler_params=pltpu.CompilerParams(dimension_semantics=("arbitrary",)),
        name="pack",
    )(tab)


def _decode(w):
    lo = pltpu.bitcast(w << 16, jnp.float32)
    hi = pltpu.bitcast(w & jnp.int32(-65536), jnp.float32)
    return lo, hi


def _peer_u_kernel(exp_ref, tab_ref, x_ref, g_ref, act_ref, q0_ref, q1_ref, qs_ref):
    ntok, nsel = exp_ref.shape
    ones = jnp.ones((SUBLANES, LANES), jnp.bfloat16)

    def products(t, q_ref):
        x = x_ref[t]
        xlo = x[0:4]
        xhi = x[4:8]
        for e in range(nsel):
            lo, hi = _decode(tab_ref[exp_ref[t, e]])
            q_ref[pl.ds(4 * e, 4), :] = lo * xlo + hi * xhi

    def fold(t, q_ref):
        q = q_ref[pl.ds(0, nsel, stride=4), :]
        for s in range(1, 4):
            q = q + q_ref[pl.ds(s, nsel, stride=4), :]
        qs_ref[pl.ds(pl.multiple_of(t * nsel, nsel), nsel), :] = q

    def gather_group(g):
        for k in range(PEER_GROUP):
            q_ref = (q0_ref, q1_ref)[k % 2]
            products(g * PEER_GROUP + k, q_ref)
            fold(g * PEER_GROUP + k, q_ref)

    def finish_group(g):
        rows8 = PEER_GROUP * nsel
        q_hi, q_lo = _split_bf16(qs_ref[pl.ds(pl.multiple_of(g * rows8, rows8), rows8), :])
        row = (_dot_nt(ones, q_hi) + _dot_nt(ones, q_lo))[0:1]
        h = jnp.concatenate([row[:, k * nsel:(k + 1) * nsel] for k in range(PEER_GROUP)], axis=0)
        gelu = 0.5 * h * (1.0 + lax.erf(h * (1.0 / math.sqrt(2.0))))
        rows = pl.ds(pl.multiple_of(g * PEER_GROUP, PEER_GROUP), PEER_GROUP)
        act_ref[rows, :] = gelu * g_ref[rows, :]

    ngroups = ntok // PEER_GROUP
    gather_group(0)

    def body(g, carry):
        finish_group(g - 1)
        gather_group(g)
        return carry

    lax.fori_loop(1, ngroups, body, 0)
    finish_group(ngroups - 1)


def _sc_peer_v(vtab, idx, actrep, x1, tok0, ntok):
    d = x1.shape[1]
    nsel = idx.shape[1]
    info = plsc.get_sparse_core_info()
    nw = info.num_cores * info.num_subcores
    per_w = ntok // nw
    nchunk = nsel // SC_ROWS
    nitem = SC_TB * nchunk
    assert info.num_lanes == SC_LANES and ntok % (nw * SC_TB) == 0 and nitem % SC_NBUF == 0
    mesh = plsc.VectorSubcoreMesh(core_axis_name="c", subcore_axis_name="s")

    @functools.partial(
        pl.kernel, mesh=mesh,
        out_type=jax.ShapeDtypeStruct((ntok, d), jnp.float32),
        scratch_types=[
            pltpu.VMEM((SC_TB, nsel), jnp.int32),
            pltpu.VMEM((SC_TB, nsel * SC_LANES), jnp.float32),
            pltpu.VMEM((SC_TB, d), jnp.float32),
        ] + [pltpu.VMEM((SC_ROWS, d), jnp.float32)] * SC_NBUF + [pltpu.SemaphoreType.DMA] * SC_NBUF,
        name="sc_peer_v",
    )
    def k(vtab_hbm, idx_hbm, act_hbm, x1_hbm, out_hbm, idx_v, act_v, acc_v, *bufs_sems):
        bufs, sems = bufs_sems[:SC_NBUF], bufs_sems[SC_NBUF:]
        wid = lax.axis_index("s") * info.num_cores + lax.axis_index("c")

        def gather(item, rows, sem):
            tl = item // nchunk
            ch = item % nchunk
            return pltpu.make_async_copy(vtab_hbm.at[idx_v.at[tl, pl.ds(ch * SC_ROWS, SC_ROWS)]], rows, sem)

        def accumulate(item, rows):
            tl = item // nchunk
            r0 = (item % nchunk) * SC_ROWS
            a = [act_v[tl, pl.ds((r0 + e) * SC_LANES, SC_LANES)] for e in range(SC_ROWS)]

            @plsc.parallel_loop(0, d // SC_LANES, 1, unroll=2)
            def _(c):
                col = pl.ds(c * SC_LANES, SC_LANES)
                terms = [a[e] * rows[e, col] for e in range(SC_ROWS)]
                while len(terms) > 1:
                    terms = [terms[j] + terms[j + 1] for j in range(0, len(terms), 2)]
                acc_v[tl, col] = acc_v[tl, col] + terms[0]

        @pl.loop(0, per_w // SC_TB)
        def _(bi):
            loc = pl.multiple_of(wid * per_w + bi * SC_TB, SC_TB)
            src = pl.ds(tok0 + loc, SC_TB)
            pltpu.sync_copy(idx_hbm.at[src], idx_v)
            pltpu.sync_copy(act_hbm.at[pl.ds(loc, SC_TB)], act_v)
            pltpu.sync_copy(x1_hbm.at[src], acc_v)
            for j in range(SC_NBUF - 1):
                gather(j, bufs[j], sems[j]).start()

            @pl.loop(0, nitem // SC_NBUF)
            def _(p):
                for j in range(SC_NBUF):
                    item = p * SC_NBUF + j
                    nxt = (j + SC_NBUF - 1) % SC_NBUF

                    @pl.when(item + SC_NBUF - 1 < nitem)
                    def _():
                        gather(item + SC_NBUF - 1, bufs[nxt], sems[nxt]).start()

                    gather(item, bufs[j], sems[j]).wait()
                    accumulate(item, bufs[j])

            pltpu.sync_copy(acc_v, out_hbm.at[pl.ds(loc, SC_TB)])

    return k(vtab, idx, actrep, x1)


def _peer_v_kernel(exp_ref, act_ref, tab_ref, x_ref, o_ref):
    nsel = exp_ref.shape[1]

    def tok(t, carry):
        acc_lo = jnp.zeros((4, LANES), jnp.float32)
        acc_hi = jnp.zeros((4, LANES), jnp.float32)
        for e in range(nsel):
            a = act_ref[t, e]
            lo, hi = _decode(tab_ref[exp_ref[t, e]])
            acc_lo = acc_lo + a * lo
            acc_hi = acc_hi + a * hi
        o_ref[t] = x_ref[t] + jnp.concatenate([acc_lo, acc_hi], axis=0)
        return carry

    lax.fori_loop(0, x_ref.shape[0], tok, 0)


def _peer_experts(xn3, x1, experts, gates, u_packed, v_table, v_packed):
    T, nsel = experts.shape
    n_exp = u_packed.shape[0]
    d = x1.shape[1]
    unit = T // sum(u for u, _ in PEER_PLAN)
    assert nsel == LANES and PEER_TT % PEER_GROUP == 0 and unit % PEER_TT == 0
    tab_spec = pl.BlockSpec((n_exp, 4, LANES), lambda i: (0, 0, 0), pipeline_mode=pl.Buffered(1))
    cp = pltpu.CompilerParams(dimension_semantics=("arbitrary",), vmem_limit_bytes=VMEM_LIMIT)
    x13 = x1.reshape(T, SUBLANES, LANES)
    outs = []
    tok0 = 0
    for units, on_sc in PEER_PLAN:
        tc = units * unit
        steps = tc // PEER_TT
        off = tok0 // PEER_TT
        row = lambda i, off=off: (i + off, 0)
        row3 = lambda i, off=off: (i + off, 0, 0)
        act = pl.pallas_call(
            _peer_u_kernel,
            grid=(steps,),
            in_specs=[pl.BlockSpec((PEER_TT, nsel), row, memory_space=pltpu.SMEM), tab_spec,
                      pl.BlockSpec((PEER_TT, SUBLANES, LANES), row3), pl.BlockSpec((PEER_TT, nsel), row)],
            out_specs=pl.BlockSpec((PEER_TT, nsel), lambda i: (i, 0)),
            out_shape=jax.ShapeDtypeStruct((tc, nsel), jnp.float32),
            scratch_shapes=[pltpu.VMEM((4 * nsel, LANES), jnp.float32), pltpu.VMEM((4 * nsel, LANES), jnp.float32),
                            pltpu.VMEM((PEER_TT * nsel, LANES), jnp.float32)],
            compiler_params=cp,
            name="peer_u",
        )(experts, u_packed, xn3, gates)
        if on_sc:
            actrep = jnp.broadcast_to(act[:, :, None], (tc, nsel, SC_LANES)).reshape(tc, nsel * SC_LANES)
            outs.append(_sc_peer_v(v_table, experts, actrep, x1, tok0, tc))
        else:
            out3 = pl.pallas_call(
                _peer_v_kernel,
                grid=(steps,),
                in_specs=[pl.BlockSpec((PEER_TT, nsel), row, memory_space=pltpu.SMEM),
                          pl.BlockSpec((PEER_TT, nsel), lambda i: (i, 0), memory_space=pltpu.SMEM),
                          tab_spec, pl.BlockSpec((PEER_TT, SUBLANES, LANES), row3)],
                out_specs=pl.BlockSpec((PEER_TT, SUBLANES, LANES), lambda i: (i, 0, 0)),
                out_shape=jax.ShapeDtypeStruct((tc, SUBLANES, LANES), jnp.float32),
                compiler_params=cp,
                name="peer_v",
            )(experts, act, v_packed, x13)
            outs.append(out3.reshape(tc, d))
        tok0 += tc
    return jnp.concatenate(outs, axis=0)


def _alibi_slopes(n):
    return 2.0 ** (-8.0 * np.arange(1, n + 1) / n)


def _lambda_init(layer):
    return 0.8 - 0.6 * math.exp(-0.3 * layer)


def kernel(x, norm1_g, w_in, moba_qk_norm_g, diff_qk_norm_g, diff_lambda, diff_subln_g, w_out, norm2_g,
           peer_w_query, peer_sub_keys, peer_u, peer_v):
    b, s, d = x.shape
    T = b * s
    depth = w_in.shape[0]
    mw = MOBA_HEADS * HEAD_DIM
    assert d == 1024 and s % PROJ_TM == 0 and mw == 512 and DIFF_HEADS * 2 * HEAD_DIM == 512
    f32, bf16 = jnp.float32, jnp.bfloat16
    scale = HEAD_DIM ** -0.5
    grp = lax.broadcasted_iota(jnp.int32, (LANES, LANES), 0) // HEAD_DIM
    gsum = (grp == grp.T).astype(bf16)
    moba_slopes = jnp.asarray(_alibi_slopes(MOBA_HEADS).reshape(MOBA_HEADS // 2, 2), f32)
    diff_slopes = jnp.asarray(np.repeat(_alibi_slopes(DIFF_HEADS)[:, None], 2, axis=1), f32)
    nblk = PROJ_TM // MOBA_BLOCK
    x2 = x.reshape(T, d)
    for l in range(depth):
        w = w_in[l]
        wqk = jnp.concatenate([w[:, 0:2 * mw], w[:, 3 * mw:3 * mw + 1024]], axis=1).astype(bf16)
        wvt = jnp.concatenate([w[:, 2 * mw:3 * mw], w[:, 3 * mw + 1024:]], axis=1).T.astype(bf16)
        mg, dg = moba_qk_norm_g[l].astype(f32), diff_qk_norm_g[l].astype(f32)
        gqk = jnp.concatenate([jnp.tile(mg[0] * scale, 8), jnp.tile(mg[1], 8),
                               jnp.tile(dg[0] * scale, 8), jnp.tile(dg[1], 8)])[None, :]
        g1 = norm1_g[l][None, :].astype(f32)
        g2 = norm2_g[l][None, :].astype(f32)
        gsub = jnp.broadcast_to(diff_subln_g[l].astype(f32)[:, None], (LANES, MOBA_BLOCK))
        lam_p = diff_lambda[l].astype(f32)
        wo, wq = w_out[l].astype(bf16), peer_w_query[l].astype(bf16)
        keys = peer_sub_keys[l].reshape(2 * PEER_HEADS, PEER_N_KEYS, LANES).astype(bf16)
        u_packed = _pack_table(peer_u[l]).reshape(peer_u.shape[1], 4, LANES)
        v_packed = _pack_table(peer_v[l]).reshape(peer_v.shape[1], 4, LANES)
        v_table = peer_v[l].astype(f32)
        mq, mk, dq, dk, mvt, dvt, kmean = _project(x2, g1, wqk, wvt, gqk, gsum)
        kmean = kmean.reshape(T // PROJ_TM, SUBLANES, 512)[:, :nblk].reshape(T // MOBA_BLOCK, 512)
        moba_o = _moba_attention(mq, mk, mvt, kmean, moba_slopes, b, s)
        diff_o = _diff_attention(dq, dk, dvt, lam_p, gsub, diff_slopes, _lambda_init(l), b, s)
        x1, xn, experts, gates = _mix(x2, moba_o, diff_o, wo, g2, wq, keys)
        x2 = _peer_experts(xn.reshape(T, SUBLANES, LANES), x1, experts, gates, u_packed, v_table, v_packed)
    return x2.reshape(b, s, d)
```

```python
import functools
import math

import jax
import jax.numpy as jnp
import numpy as np
from jax import lax
from jax.experimental import pallas as pl
from jax.experimental.pallas import tpu as pltpu
from jax.experimental.pallas import tpu_sc as plsc

HEAD_DIM = 64
MOBA_HEADS = 8
MOBA_BLOCK = 256
MOBA_TOPK = 3
DIFF_HEADS = 4
PEER_HEADS = 8
PEER_N_KEYS = 128
PEER_TOPK = 16
RMS_EPS = 1e-6

LANES = 128
SUBLANES = 8
VMEM_LIMIT = 48 * 1024 * 1024

PROJ_TM = 512
MIX_TM = 256
PEER_TT = 64
PACK_ROWS = 512
ATT_QBLOCKS = 2
PEER_GROUP = 8
PEER_PLAN = ((1, True), (2, True), (2, True), (2, True), (2, True), (2, True), (2, True), (1, True), (2, False))
SC_LANES = 16
SC_ROWS = 16
SC_NBUF = 4
SC_TB = 8

NEG_INF = float("-inf")
_NT = (((1,), (1,)), ((), ()))


def _dot(a, b):
    return jnp.dot(a, b, preferred_element_type=jnp.float32)


def _dot_nt(a, b):
    return lax.dot_general(a, b, _NT, preferred_element_type=jnp.float32)


def _split_bf16(x):
    hi = x.astype(jnp.bfloat16)
    lo = (x - hi.astype(jnp.float32)).astype(jnp.bfloat16)
    return hi, lo


def _proj_kernel(x_ref, g1_ref, wqk_ref, wvt_ref, gqk_ref, gsum_ref,
                 mq_ref, mk_ref, dq_ref, dk_ref, mvt_ref, dvt_ref, kmean_ref):
    x = x_ref[...]
    ms = jnp.mean(x * x, axis=-1, keepdims=True)
    h = (x * lax.rsqrt(ms + RMS_EPS) * g1_ref[...]).astype(jnp.bfloat16)
    gsum = gsum_ref[...]
    outs = (mq_ref, mk_ref, dq_ref, dk_ref)
    for sec in range(4):
        for half in range(2):
            c0 = sec * 512 + half * 256
            p = _dot(h, wqk_ref[:, c0:c0 + 256])
            for s in range(2):
                ps = p[:, s * LANES:(s + 1) * LANES]
                sq_hi, sq_lo = _split_bf16(ps * ps)
                ssq = _dot(sq_hi, gsum) + _dot(sq_lo, gsum)
                g = gqk_ref[:, c0 + s * LANES:c0 + (s + 1) * LANES]
                y = ps * lax.rsqrt(ssq * (1.0 / HEAD_DIM) + RMS_EPS) * g
                l0 = half * 256 + s * LANES
                outs[sec][:, l0:l0 + LANES] = y.astype(jnp.bfloat16)
                if sec == 1:
                    for blk in range(PROJ_TM // MOBA_BLOCK):
                        kmean_ref[blk:blk + 1, l0:l0 + LANES] = jnp.mean(
                            y[blk * MOBA_BLOCK:(blk + 1) * MOBA_BLOCK], axis=0, keepdims=True)
    kmean_ref[PROJ_TM // MOBA_BLOCK:, :] = jnp.zeros((SUBLANES - PROJ_TM // MOBA_BLOCK, 512), jnp.float32)
    vt = _dot_nt(wvt_ref[...], h).astype(jnp.bfloat16)
    for blk in range(PROJ_TM // MOBA_BLOCK):
        mvt_ref[blk] = vt[0:512, blk * MOBA_BLOCK:(blk + 1) * MOBA_BLOCK]
        dvt_ref[blk] = vt[512:1024, blk * MOBA_BLOCK:(blk + 1) * MOBA_BLOCK]


def _project(x2, g1, wqk, wvt, gqk, gsum):
    T = x2.shape[0]
    nt = T // PROJ_TM
    nblk = PROJ_TM // MOBA_BLOCK
    row = lambda i: (i, 0)
    const = lambda i: (0, 0)
    qk_shape = jax.ShapeDtypeStruct((T, 512), jnp.bfloat16)
    vt_shape = jax.ShapeDtypeStruct((T // MOBA_BLOCK, 512, MOBA_BLOCK), jnp.bfloat16)
    return pl.pallas_call(
        _proj_kernel,
        grid=(nt,),
        in_specs=[
            pl.BlockSpec((PROJ_TM, 1024), row),
            pl.BlockSpec((1, 1024), const),
            pl.BlockSpec((1024, 2048), const),
            pl.BlockSpec((1024, 1024), const),
            pl.BlockSpec((1, 2048), const),
            pl.BlockSpec((LANES, LANES), const),
        ],
        out_specs=[
            pl.BlockSpec((PROJ_TM, 512), row),
            pl.BlockSpec((PROJ_TM, 512), row),
            pl.BlockSpec((PROJ_TM, 512), row),
            pl.BlockSpec((PROJ_TM, 512), row),
            pl.BlockSpec((nblk, 512, MOBA_BLOCK), lambda i: (i, 0, 0)),
            pl.BlockSpec((nblk, 512, MOBA_BLOCK), lambda i: (i, 0, 0)),
            pl.BlockSpec((SUBLANES, 512), row),
        ],
        out_shape=[qk_shape, qk_shape, qk_shape, qk_shape, vt_shape, vt_shape,
                   jax.ShapeDtypeStruct((nt * SUBLANES, 512), jnp.float32)],
        compiler_params=pltpu.CompilerParams(dimension_semantics=("arbitrary",), vmem_limit_bytes=VMEM_LIMIT),
        name="proj",
    )(x2, g1, wqk, wvt, gqk, gsum)


def _rel_qk():
    qc = lax.broadcasted_iota(jnp.int32, (MOBA_BLOCK, MOBA_BLOCK), 1)
    kr = lax.broadcasted_iota(jnp.int32, (MOBA_BLOCK, MOBA_BLOCK), 0)
    return qc - kr


def _attend(q_h, k_ref, vt_load, i, slope, rel, selbias):
    nb = rel.astype(jnp.float32) * (-slope)
    ts, rowbs, colmax = [], [], []
    for j in range(i + 1):
        t = _dot_nt(k_ref[j * MOBA_BLOCK:(j + 1) * MOBA_BLOCK, :], q_h) + nb
        if j == i:
            t = jnp.where(rel >= 0, t, NEG_INF)
            rowb = jnp.zeros((1, MOBA_BLOCK), jnp.float32)
        else:
            rowb = jnp.full((1, MOBA_BLOCK), (i - j) * MOBA_BLOCK, jnp.float32) * (-slope)
            if selbias is not None:
                rowb = rowb + selbias[j:j + 1, :]
        ts.append(t)
        rowbs.append(rowb)
        colmax.append(jnp.max(t, axis=0, keepdims=True) + rowb)
    m = functools.reduce(jnp.maximum, colmax)
    l, acc = None, None
    for j in range(i + 1):
        p = jnp.exp(ts[j] - (m - rowbs[j]))
        lj = jnp.sum(p, axis=0, keepdims=True)
        aj = _dot(vt_load(j), p.astype(jnp.bfloat16))
        l = lj if l is None else l + lj
        acc = aj if acc is None else acc + aj
    return acc / l


def _head_queries(q2):
    lane = lax.broadcasted_iota(jnp.int32, q2.shape, 1)
    return [jnp.where((lane >= hh * HEAD_DIM) & (lane < (hh + 1) * HEAD_DIM), q2, jnp.zeros_like(q2))
            for hh in range(2)]


def _block_selection(gate, i):
    blk = lax.broadcasted_iota(jnp.int32, gate.shape, 0)
    cnt = jnp.zeros(gate.shape, jnp.int32)
    for mblk in range(i):
        gm = gate[mblk:mblk + 1, :]
        beats = (gm > gate) | ((gm == gate) & (mblk < blk))
        cnt = cnt + jnp.where(beats, 1, 0)
    return jnp.where((blk < i) & (cnt < MOBA_TOPK), 0.0, NEG_INF)


def _moba_kernel(slopes_ref, q_ref, k_ref, vt_ref, kmean_ref, o_ref):
    pair = pl.program_id(1)
    step = pl.program_id(2)
    nq = vt_ref.shape[0]
    rel = _rel_qk()
    km_hi, km_lo = _split_bf16(kmean_ref[...])
    for st in range(nq // ATT_QBLOCKS):
        @pl.when(step == st)
        def _(st=st):
            for u in range(ATT_QBLOCKS):
                i = st * ATT_QBLOCKS + u
                rows = slice(u * MOBA_BLOCK, (u + 1) * MOBA_BLOCK)
                outs = []
                for hh, q_h in enumerate(_head_queries(q_ref[rows, :])):
                    gate = _dot_nt(km_hi, q_h) + _dot_nt(km_lo, q_h)
                    vt_load = lambda j, hh=hh: vt_ref[j, hh * HEAD_DIM:(hh + 1) * HEAD_DIM, :]
                    outs.append(_attend(q_h, k_ref, vt_load, i, slopes_ref[pair, hh], rel,
                                        _block_selection(gate, i)))
                o_t = jnp.concatenate(outs, axis=0)
                o_ref[rows, :] = o_t.T.astype(o_ref.dtype)


def _diff_kernel(slopes_ref, lam_ref, q_ref, k_ref, vt_ref, gsub_ref, o_ref, *, lam_init):
    head = pl.program_id(1)
    step = pl.program_id(2)
    nq = vt_ref.shape[0]
    rel = _rel_qk()
    slope = slopes_ref[head, 0]
    lp = lam_ref[...]
    lam = (jnp.exp(jnp.sum(lp[0:1] * lp[1:2], axis=-1, keepdims=True))
           - jnp.exp(jnp.sum(lp[2:3] * lp[3:4], axis=-1, keepdims=True)) + lam_init)
    vt_load = lambda j: vt_ref[j]
    for st in range(nq // ATT_QBLOCKS):
        @pl.when(step == st)
        def _(st=st):
            for u in range(ATT_QBLOCKS):
                i = st * ATT_QBLOCKS + u
                rows = slice(u * MOBA_BLOCK, (u + 1) * MOBA_BLOCK)
                parts = [_attend(q_h, k_ref, vt_load, i, slope, rel, None)
                         for q_h in _head_queries(q_ref[rows, :])]
                o_t = parts[0] - lam * parts[1]
                ms = jnp.mean(o_t * o_t, axis=0, keepdims=True)
                o_t = o_t * lax.rsqrt(ms + RMS_EPS) * gsub_ref[...] * (1.0 - lam_init)
                o_ref[rows, :] = o_t.T.astype(o_ref.dtype)


def _attention_specs(nq):
    nst = nq // ATT_QBLOCKS
    q_spec = pl.BlockSpec((ATT_QBLOCKS * MOBA_BLOCK, LANES), lambda b, p, i: (b * nst + i, p))
    k_spec = pl.BlockSpec((nq * MOBA_BLOCK, LANES), lambda b, p, i: (b, p))
    vt_spec = pl.BlockSpec((nq, LANES, MOBA_BLOCK), lambda b, p, i: (b, p, 0))
    return q_spec, k_spec, vt_spec


def _moba_attention(q, k, vt, kmean, slopes, batch, seq):
    nq = seq // MOBA_BLOCK
    q_spec, k_spec, vt_spec = _attention_specs(nq)
    return pl.pallas_call(
        _moba_kernel,
        grid=(batch, MOBA_HEADS // 2, nq // ATT_QBLOCKS),
        in_specs=[
            pl.BlockSpec(memory_space=pltpu.SMEM),
            q_spec, k_spec, vt_spec,
            pl.BlockSpec((SUBLANES, LANES), lambda b, p, i: (b, p)),
        ],
        out_specs=q_spec,
        out_shape=jax.ShapeDtypeStruct(q.shape, jnp.bfloat16),
        compiler_params=pltpu.CompilerParams(
            dimension_semantics=("arbitrary", "arbitrary", "arbitrary"), vmem_limit_bytes=VMEM_LIMIT),
        name="moba",
    )(slopes, q, k, vt, kmean)


def _diff_attention(q, k, vt, lam_p, gsub, slopes, lam_init, batch, seq):
    nq = seq // MOBA_BLOCK
    q_spec, k_spec, vt_spec = _attention_specs(nq)
    return pl.pallas_call(
        functools.partial(_diff_kernel, lam_init=lam_init),
        grid=(batch, DIFF_HEADS, nq // ATT_QBLOCKS),
        in_specs=[
            pl.BlockSpec(memory_space=pltpu.SMEM),
            pl.BlockSpec((4, HEAD_DIM), lambda b, p, i: (0, 0)),
            q_spec, k_spec, vt_spec,
            pl.BlockSpec((LANES, MOBA_BLOCK), lambda b, p, i: (0, 0)),
        ],
        out_specs=q_spec,
        out_shape=jax.ShapeDtypeStruct(q.shape, jnp.bfloat16),
        compiler_params=pltpu.CompilerParams(
            dimension_semantics=("arbitrary", "arbitrary", "arbitrary"), vmem_limit_bytes=VMEM_LIMIT),
        name="diff",
    )(slopes, lam_p, q, k, vt, gsub)


def _knockout_topk(sc, k, payload=None):
    n = sc.shape[0]
    row = lax.broadcasted_iota(jnp.int32, sc.shape, 0).astype(jnp.float32)
    vals, picks = [], []
    for _ in range(k):
        m = jnp.max(sc, axis=0, keepdims=True)
        idx = jnp.min(jnp.where(sc == m, row, float(n)), axis=0, keepdims=True)
        hit = row == idx
        vals.append(m)
        picks.append(idx if payload is None else jnp.sum(jnp.where(hit, payload, 0.0), axis=0, keepdims=True))
        sc = jnp.where(hit, NEG_INF, sc)
    return vals, picks


def _mix_kernel(x_ref, am_ref, ad_ref, wo_ref, g2_ref, wq_ref, keys_ref,
                x1_ref, xn_ref, exp_ref, gate_ref):
    x1 = x_ref[...] + _dot(am_ref[...], wo_ref[0:512, :]) + _dot(ad_ref[...], wo_ref[512:1024, :])
    x1_ref[...] = x1
    ms = jnp.mean(x1 * x1, axis=-1, keepdims=True)
    xn = x1 * lax.rsqrt(ms + RMS_EPS) * g2_ref[...]
    xn_ref[...] = xn
    qq = _dot(xn.astype(jnp.bfloat16), wq_ref[...]).astype(jnp.bfloat16)
    tm = qq.shape[0]
    exp_rows, gate_rows = [], []
    for h in range(PEER_HEADS):
        tops = []
        for half in range(2):
            s = h * 2 + half
            sc = _dot_nt(keys_ref[s], qq[:, s * LANES:(s + 1) * LANES])
            tops.append(_knockout_topk(sc, PEER_TOPK))
        (s1, i1), (s2, i2) = tops
        s2a = jnp.concatenate(s2, axis=0)
        i2a = jnp.concatenate(i2, axis=0)
        cand, cidx = [], []
        for a in range(PEER_TOPK):
            nb = PEER_TOPK // (a + 1)
            cand.append(s1[a] + s2a[0:nb])
            cidx.append(i1[a] * float(PEER_N_KEYS) + i2a[0:nb])
        n_cand = sum(c.shape[0] for c in cand)
        pad = -n_cand % SUBLANES
        cand.append(jnp.full((pad, tm), NEG_INF, jnp.float32))
        cidx.append(jnp.zeros((pad, tm), jnp.float32))
        best, experts = _knockout_topk(jnp.concatenate(cand, axis=0), PEER_TOPK,
                                       payload=jnp.concatenate(cidx, axis=0))
        b = jnp.concatenate(best, axis=0)
        e = jnp.exp(b - b[0:1])
        gate_rows.append(e / jnp.sum(e, axis=0, keepdims=True))
        exp_rows.append(jnp.concatenate(experts, axis=0))
    exp_ref[...] = jnp.concatenate(exp_rows, axis=0).T.astype(jnp.int32)
    gate_ref[...] = jnp.concatenate(gate_rows, axis=0).T


def _mix(x2, am, ad, wo, g2, wq, keys):
    T = x2.shape[0]
    row = lambda i: (i, 0)
    const = lambda i: (0, 0)
    return pl.pallas_call(
        _mix_kernel,
        grid=(T // MIX_TM,),
        in_specs=[
            pl.BlockSpec((MIX_TM, 1024), row),
            pl.BlockSpec((MIX_TM, 512), row),
            pl.BlockSpec((MIX_TM, 512), row),
            pl.BlockSpec((1024, 1024), const),
            pl.BlockSpec((1, 1024), const),
            pl.BlockSpec((1024, 2048), const),
            pl.BlockSpec((2 * PEER_HEADS, PEER_N_KEYS, LANES), lambda i: (0, 0, 0)),
        ],
        out_specs=[
            pl.BlockSpec((MIX_TM, 1024), row),
            pl.BlockSpec((MIX_TM, 1024), row),
            pl.BlockSpec((MIX_TM, LANES), row),
            pl.BlockSpec((MIX_TM, LANES), row),
        ],
        out_shape=[
            jax.ShapeDtypeStruct((T, 1024), jnp.float32),
            jax.ShapeDtypeStruct((T, 1024), jnp.float32),
            jax.ShapeDtypeStruct((T, LANES), jnp.int32),
            jax.ShapeDtypeStruct((T, LANES), jnp.float32),
        ],
        compiler_params=pltpu.CompilerParams(dimension_semantics=("arbitrary",), vmem_limit_bytes=VMEM_LIMIT),
        name="mix",
    )(x2, am, ad, wo, g2, wq, keys)


def _pack_kernel(t_ref, o_ref):
    t = t_ref[...]
    lo = pltpu.bitcast(t[:, 0:512].astype(jnp.bfloat16).astype(jnp.float32), jnp.int32)
    hi = pltpu.bitcast(t[:, 512:1024].astype(jnp.bfloat16).astype(jnp.float32), jnp.int32)
    o_ref[...] = lax.shift_right_logical(lo, 16) | (hi & jnp.int32(-65536))


def _pack_table(tab):
    n = tab.shape[0]
    return pl.pallas_call(
        _pack_kernel,
        grid=(n // PACK_ROWS,),
        in_specs=[pl.BlockSpec((PACK_ROWS, 1024), lambda i: (i, 0))],
        out_specs=pl.BlockSpec((PACK_ROWS, 512), lambda i: (i, 0)),
        out_shape=jax.ShapeDtypeStruct((n, 512), jnp.int32),
        compiler_params=pltpu.CompilerParams(dimension_semantics=("arbitrary",)),
        name="pack",
    )(tab)


def _decode(w):
    lo = pltpu.bitcast(w << 16, jnp.float32)
    hi = pltpu.bitcast(w & jnp.int32(-65536), jnp.float32)
    return lo, hi


def _peer_u_kernel(exp_ref, tab_ref, x_ref, g_ref, act_ref, q0_ref, q1_ref, qs_ref):
    ntok, nsel = exp_ref.shape
    ones = jnp.ones((SUBLANES, LANES), jnp.bfloat16)

    def products(t, q_ref):
        x = x_ref[t]
        xlo = x[0:4]
        xhi = x[4:8]
        for e in range(nsel):
            lo, hi = _decode(tab_ref[exp_ref[t, e]])
            q_ref[pl.ds(4 * e, 4), :] = lo * xlo + hi * xhi

    def fold(t, q_ref):
        q = q_ref[pl.ds(0, nsel, stride=4), :]
        for s in range(1, 4):
            q = q + q_ref[pl.ds(s, nsel, stride=4), :]
        qs_ref[pl.ds(pl.multiple_of(t * nsel, nsel), nsel), :] = q

    def gather_group(g):
        for k in range(PEER_GROUP):
            q_ref = (q0_ref, q1_ref)[k % 2]
            products(g * PEER_GROUP + k, q_ref)
            fold(g * PEER_GROUP + k, q_ref)

    def finish_group(g):
        rows8 = PEER_GROUP * nsel
        q_hi, q_lo = _split_bf16(qs_ref[pl.ds(pl.multiple_of(g * rows8, rows8), rows8), :])
        row = (_dot_nt(ones, q_hi) + _dot_nt(ones, q_lo))[0:1]
        h = jnp.concatenate([row[:, k * nsel:(k + 1) * nsel] for k in range(PEER_GROUP)], axis=0)
        gelu = 0.5 * h * (1.0 + lax.erf(h * (1.0 / math.sqrt(2.0))))
        rows = pl.ds(pl.multiple_of(g * PEER_GROUP, PEER_GROUP), PEER_GROUP)
        act_ref[rows, :] = gelu * g_ref[rows, :]

    ngroups = ntok // PEER_GROUP
    gather_group(0)

    def body(g, carry):
        finish_group(g - 1)
        gather_group(g)
        return carry

    lax.fori_loop(1, ngroups, body, 0)
    finish_group(ngroups - 1)


def _sc_peer_v(vtab, idx, act, x1, tok0, ntok):
    d = x1.shape[1]
    nsel = idx.shape[1]
    info = plsc.get_sparse_core_info()
    nw = info.num_cores * info.num_subcores
    per_w = ntok // nw
    nchunk = nsel // SC_ROWS
    nitem = SC_TB * nchunk
    assert info.num_lanes == SC_LANES and ntok % (nw * SC_TB) == 0 and nitem % SC_NBUF == 0
    mesh = plsc.VectorSubcoreMesh(core_axis_name="c", subcore_axis_name="s")

    @functools.partial(
        pl.kernel, mesh=mesh,
        out_type=jax.ShapeDtypeStruct((ntok, d), jnp.float32),
        scratch_types=[
            pltpu.VMEM((SC_TB, nsel), jnp.int32),
            pltpu.VMEM((SC_TB, nsel), jnp.float32),
            pltpu.VMEM((SC_TB, d), jnp.float32),
        ] + [pltpu.VMEM((SC_ROWS, d), jnp.float32)] * SC_NBUF + [pltpu.SemaphoreType.DMA] * SC_NBUF,
        name="sc_peer_v",
    )
    def k(vtab_hbm, idx_hbm, act_hbm, x1_hbm, out_hbm, idx_v, act_v, acc_v, *bufs_sems):
        bufs, sems = bufs_sems[:SC_NBUF], bufs_sems[SC_NBUF:]
        wid = lax.axis_index("s") * info.num_cores + lax.axis_index("c")

        def gather(item, rows, sem):
            tl = item // nchunk
            ch = item % nchunk
            return pltpu.make_async_copy(vtab_hbm.at[idx_v.at[tl, pl.ds(ch * SC_ROWS, SC_ROWS)]], rows, sem)

        def accumulate(item, rows):
            tl = item // nchunk
            r0 = (item % nchunk) * SC_ROWS
            a = []
            for e0 in range(0, SC_ROWS, SC_LANES):
                w16 = act_v[tl, pl.ds(r0 + e0, SC_LANES)]
                a += [jnp.full((SC_LANES,), w16[e], jnp.float32) for e in range(SC_LANES)]

            @plsc.parallel_loop(0, d // SC_LANES, 1, unroll=2)
            def _(c):
                col = pl.ds(c * SC_LANES, SC_LANES)
                terms = [a[e] * rows[e, col] for e in range(SC_ROWS)]
                while len(terms) > 1:
                    terms = [terms[j] + terms[j + 1] for j in range(0, len(terms), 2)]
                acc_v[tl, col] = acc_v[tl, col] + terms[0]

        @pl.loop(0, per_w // SC_TB)
        def _(bi):
            loc = pl.multiple_of(wid * per_w + bi * SC_TB, SC_TB)
            src = pl.ds(tok0 + loc, SC_TB)
            pltpu.sync_copy(idx_hbm.at[src], idx_v)
            pltpu.sync_copy(act_hbm.at[pl.ds(loc, SC_TB)], act_v)
            pltpu.sync_copy(x1_hbm.at[src], acc_v)
            for j in range(SC_NBUF - 1):
                gather(j, bufs[j], sems[j]).start()

            @pl.loop(0, nitem // SC_NBUF)
            def _(p):
                for j in range(SC_NBUF):
                    item = p * SC_NBUF + j
                    nxt = (j + SC_NBUF - 1) % SC_NBUF

                    @pl.when(item + SC_NBUF - 1 < nitem)
                    def _():
                        gather(item + SC_NBUF - 1, bufs[nxt], sems[nxt]).start()

                    gather(item, bufs[j], sems[j]).wait()
                    accumulate(item, bufs[j])

            pltpu.sync_copy(acc_v, out_hbm.at[pl.ds(loc, SC_TB)])

    return k(vtab, idx, act, x1)


def _peer_v_kernel(exp_ref, act_ref, tab_ref, x_ref, o_ref):
    nsel = exp_ref.shape[1]

    def tok(t, carry):
        acc_lo = jnp.zeros((4, LANES), jnp.float32)
        acc_hi = jnp.zeros((4, LANES), jnp.float32)
        for e in range(nsel):
            a = act_ref[t, e]
            lo, hi = _decode(tab_ref[exp_ref[t, e]])
            acc_lo = acc_lo + a * lo
            acc_hi = acc_hi + a * hi
        o_ref[t] = x_ref[t] + jnp.concatenate([acc_lo, acc_hi], axis=0)
        return carry

    lax.fori_loop(0, x_ref.shape[0], tok, 0)


def _peer_experts(xn3, x1, experts, gates, u_packed, v_table, v_packed):
    T, nsel = experts.shape
    n_exp = u_packed.shape[0]
    d = x1.shape[1]
    unit = T // sum(u for u, _ in PEER_PLAN)
    assert nsel == LANES and PEER_TT % PEER_GROUP == 0 and unit % PEER_TT == 0
    tab_spec = pl.BlockSpec((n_exp, 4, LANES), lambda i: (0, 0, 0), pipeline_mode=pl.Buffered(1))
    cp = pltpu.CompilerParams(dimension_semantics=("arbitrary",), vmem_limit_bytes=VMEM_LIMIT)
    x13 = x1.reshape(T, SUBLANES, LANES)
    outs = []
    tok0 = 0
    for units, on_sc in PEER_PLAN:
        tc = units * unit
        steps = tc // PEER_TT
        off = tok0 // PEER_TT
        row = lambda i, off=off: (i + off, 0)
        row3 = lambda i, off=off: (i + off, 0, 0)
        act = pl.pallas_call(
            _peer_u_kernel,
            grid=(steps,),
            in_specs=[pl.BlockSpec((PEER_TT, nsel), row, memory_space=pltpu.SMEM), tab_spec,
                      pl.BlockSpec((PEER_TT, SUBLANES, LANES), row3), pl.BlockSpec((PEER_TT, nsel), row)],
            out_specs=pl.BlockSpec((PEER_TT, nsel), lambda i: (i, 0)),
            out_shape=jax.ShapeDtypeStruct((tc, nsel), jnp.float32),
            scratch_shapes=[pltpu.VMEM((4 * nsel, LANES), jnp.float32), pltpu.VMEM((4 * nsel, LANES), jnp.float32),
                            pltpu.VMEM((PEER_TT * nsel, LANES), jnp.float32)],
            compiler_params=cp,
            name="peer_u",
        )(experts, u_packed, xn3, gates)
        if on_sc:
            outs.append(_sc_peer_v(v_table, experts, act, x1, tok0, tc))
        else:
            out3 = pl.pallas_call(
                _peer_v_kernel,
                grid=(steps,),
                in_specs=[pl.BlockSpec((PEER_TT, nsel), row, memory_space=pltpu.SMEM),
                          pl.BlockSpec((PEER_TT, nsel), lambda i: (i, 0), memory_space=pltpu.SMEM),
                          tab_spec, pl.BlockSpec((PEER_TT, SUBLANES, LANES), row3)],
                out_specs=pl.BlockSpec((PEER_TT, SUBLANES, LANES), lambda i: (i, 0, 0)),
                out_shape=jax.ShapeDtypeStruct((tc, SUBLANES, LANES), jnp.float32),
                compiler_params=cp,
                name="peer_v",
            )(experts, act, v_packed, x13)
            outs.append(out3.reshape(tc, d))
        tok0 += tc
    return jnp.concatenate(outs, axis=0)


def _alibi_slopes(n):
    return 2.0 ** (-8.0 * np.arange(1, n + 1) / n)


def _lambda_init(layer):
    return 0.8 - 0.6 * math.exp(-0.3 * layer)


def kernel(x, norm1_g, w_in, moba_qk_norm_g, diff_qk_norm_g, diff_lambda, diff_subln_g, w_out, norm2_g,
           peer_w_query, peer_sub_keys, peer_u, peer_v):
    b, s, d = x.shape
    T = b * s
    depth = w_in.shape[0]
    mw = MOBA_HEADS * HEAD_DIM
    assert d == 1024 and s % PROJ_TM == 0 and mw == 512 and DIFF_HEADS * 2 * HEAD_DIM == 512
    f32, bf16 = jnp.float32, jnp.bfloat16
    scale = HEAD_DIM ** -0.5
    grp = lax.broadcasted_iota(jnp.int32, (LANES, LANES), 0) // HEAD_DIM
    gsum = (grp == grp.T).astype(bf16)
    moba_slopes = jnp.asarray(_alibi_slopes(MOBA_HEADS).reshape(MOBA_HEADS // 2, 2), f32)
    diff_slopes = jnp.asarray(np.repeat(_alibi_slopes(DIFF_HEADS)[:, None], 2, axis=1), f32)
    nblk = PROJ_TM // MOBA_BLOCK
    x2 = x.reshape(T, d)
    for l in range(depth):
        w = w_in[l]
        wqk = jnp.concatenate([w[:, 0:2 * mw], w[:, 3 * mw:3 * mw + 1024]], axis=1).astype(bf16)
        wvt = jnp.concatenate([w[:, 2 * mw:3 * mw], w[:, 3 * mw + 1024:]], axis=1).T.astype(bf16)
        mg, dg = moba_qk_norm_g[l].astype(f32), diff_qk_norm_g[l].astype(f32)
        gqk = jnp.concatenate([jnp.tile(mg[0] * scale, 8), jnp.tile(mg[1], 8),
                               jnp.tile(dg[0] * scale, 8), jnp.tile(dg[1], 8)])[None, :]
        g1 = norm1_g[l][None, :].astype(f32)
        g2 = norm2_g[l][None, :].astype(f32)
        gsub = jnp.broadcast_to(diff_subln_g[l].astype(f32)[:, None], (LANES, MOBA_BLOCK))
        lam_p = diff_lambda[l].astype(f32)
        wo, wq = w_out[l].astype(bf16), peer_w_query[l].astype(bf16)
        keys = peer_sub_keys[l].reshape(2 * PEER_HEADS, PEER_N_KEYS, LANES).astype(bf16)
        u_packed = _pack_table(peer_u[l]).reshape(peer_u.shape[1], 4, LANES)
        v_packed = _pack_table(peer_v[l]).reshape(peer_v.shape[1], 4, LANES)
        v_table = peer_v[l].astype(f32)
        mq, mk, dq, dk, mvt, dvt, kmean = _project(x2, g1, wqk, wvt, gqk, gsum)
        kmean = kmean.reshape(T // PROJ_TM, SUBLANES, 512)[:, :nblk].reshape(T // MOBA_BLOCK, 512)
        moba_o = _moba_attention(mq, mk, mvt, kmean, moba_slopes, b, s)
        diff_o = _diff_attention(dq, dk, dvt, lam_p, gsub, diff_slopes, _lambda_init(l), b, s)
        x1, xn, experts, gates = _mix(x2, moba_o, diff_o, wo, g2, wq, keys)
        x2 = _peer_experts(xn.reshape(T, SUBLANES, LANES), x1, experts, gates, u_packed, v_table, v_packed)
    return x2.reshape(b, s, d)
```

```python
import functools
import math

import jax
import jax.numpy as jnp
import numpy as np
from jax import lax
from jax.experimental import pallas as pl
from jax.experimental.pallas import tpu as pltpu
from jax.experimental.pallas import tpu_sc as plsc

HEAD_DIM = 64
MOBA_HEADS = 8
MOBA_BLOCK = 256
MOBA_TOPK = 3
DIFF_HEADS = 4
PEER_HEADS = 8
PEER_N_KEYS = 128
PEER_TOPK = 16
RMS_EPS = 1e-6

LANES = 128
SUBLANES = 8
VMEM_LIMIT = 48 * 1024 * 1024

PROJ_TM = 512
MIX_TM = 256
PEER_TT = 64
PACK_ROWS = 512
ATT_QBLOCKS = 2
PEER_GROUP = 8
PEER_PLAN = ((1, True), (2, True), (2, True), (2, True), (2, True), (2, True), (2, True), (2, True), (1, False))
SC_LANES = 16
SC_ROWS = 16
SC_NBUF = 4
SC_TB = 16

NEG_INF = float("-inf")
_NT = (((1,), (1,)), ((), ()))


def _dot(a, b):
    return jnp.dot(a, b, preferred_element_type=jnp.float32)


def _dot_nt(a, b):
    return lax.dot_general(a, b, _NT, preferred_element_type=jnp.float32)


def _split_bf16(x):
    hi = x.astype(jnp.bfloat16)
    lo = (x - hi.astype(jnp.float32)).astype(jnp.bfloat16)
    return hi, lo


def _proj_kernel(x_ref, g1_ref, wqk_ref, wvt_ref, gqk_ref, gsum_ref,
                 mq_ref, mk_ref, dq_ref, dk_ref, mvt_ref, dvt_ref, kmean_ref):
    x = x_ref[...]
    ms = jnp.mean(x * x, axis=-1, keepdims=True)
    h = (x * lax.rsqrt(ms + RMS_EPS) * g1_ref[...]).astype(jnp.bfloat16)
    gsum = gsum_ref[...]
    outs = (mq_ref, mk_ref, dq_ref, dk_ref)
    for sec in range(4):
        for half in range(2):
            c0 = sec * 512 + half * 256
            p = _dot(h, wqk_ref[:, c0:c0 + 256])
            for s in range(2):
                ps = p[:, s * LANES:(s + 1) * LANES]
                sq_hi, sq_lo = _split_bf16(ps * ps)
                ssq = _dot(sq_hi, gsum) + _dot(sq_lo, gsum)
                g = gqk_ref[:, c0 + s * LANES:c0 + (s + 1) * LANES]
                y = ps * lax.rsqrt(ssq * (1.0 / HEAD_DIM) + RMS_EPS) * g
                l0 = half * 256 + s * LANES
                outs[sec][:, l0:l0 + LANES] = y.astype(jnp.bfloat16)
                if sec == 1:
                    for blk in range(PROJ_TM // MOBA_BLOCK):
                        kmean_ref[blk:blk + 1, l0:l0 + LANES] = jnp.mean(
                            y[blk * MOBA_BLOCK:(blk + 1) * MOBA_BLOCK], axis=0, keepdims=True)
    kmean_ref[PROJ_TM // MOBA_BLOCK:, :] = jnp.zeros((SUBLANES - PROJ_TM // MOBA_BLOCK, 512), jnp.float32)
    vt = _dot_nt(wvt_ref[...], h).astype(jnp.bfloat16)
    for blk in range(PROJ_TM // MOBA_BLOCK):
        mvt_ref[blk] = vt[0:512, blk * MOBA_BLOCK:(blk + 1) * MOBA_BLOCK]
        dvt_ref[blk] = vt[512:1024, blk * MOBA_BLOCK:(blk + 1) * MOBA_BLOCK]


def _project(x2, g1, wqk, wvt, gqk, gsum):
    T = x2.shape[0]
    nt = T // PROJ_TM
    nblk = PROJ_TM // MOBA_BLOCK
    row = lambda i: (i, 0)
    const = lambda i: (0, 0)
    qk_shape = jax.ShapeDtypeStruct((T, 512), jnp.bfloat16)
    vt_shape = jax.ShapeDtypeStruct((T // MOBA_BLOCK, 512, MOBA_BLOCK), jnp.bfloat16)
    return pl.pallas_call(
        _proj_kernel,
        grid=(nt,),
        in_specs=[
            pl.BlockSpec((PROJ_TM, 1024), row),
            pl.BlockSpec((1, 1024), const),
            pl.BlockSpec((1024, 2048), const),
            pl.BlockSpec((1024, 1024), const),
            pl.BlockSpec((1, 2048), const),
            pl.BlockSpec((LANES, LANES), const),
        ],
        out_specs=[
            pl.BlockSpec((PROJ_TM, 512), row),
            pl.BlockSpec((PROJ_TM, 512), row),
            pl.BlockSpec((PROJ_TM, 512), row),
            pl.BlockSpec((PROJ_TM, 512), row),
            pl.BlockSpec((nblk, 512, MOBA_BLOCK), lambda i: (i, 0, 0)),
            pl.BlockSpec((nblk, 512, MOBA_BLOCK), lambda i: (i, 0, 0)),
            pl.BlockSpec((SUBLANES, 512), row),
        ],
        out_shape=[qk_shape, qk_shape, qk_shape, qk_shape, vt_shape, vt_shape,
                   jax.ShapeDtypeStruct((nt * SUBLANES, 512), jnp.float32)],
        compiler_params=pltpu.CompilerParams(dimension_semantics=("arbitrary",), vmem_limit_bytes=VMEM_LIMIT),
        name="proj",
    )(x2, g1, wqk, wvt, gqk, gsum)


def _rel_qk():
    qc = lax.broadcasted_iota(jnp.int32, (MOBA_BLOCK, MOBA_BLOCK), 1)
    kr = lax.broadcasted_iota(jnp.int32, (MOBA_BLOCK, MOBA_BLOCK), 0)
    return qc - kr


def _attend(q_h, k_ref, vt_load, i, slope, rel, selbias):
    nb = rel.astype(jnp.float32) * (-slope)
    ts, rowbs, colmax = [], [], []
    for j in range(i + 1):
        t = _dot_nt(k_ref[j * MOBA_BLOCK:(j + 1) * MOBA_BLOCK, :], q_h) + nb
        if j == i:
            t = jnp.where(rel >= 0, t, NEG_INF)
            rowb = jnp.zeros((1, MOBA_BLOCK), jnp.float32)
        else:
            rowb = jnp.full((1, MOBA_BLOCK), (i - j) * MOBA_BLOCK, jnp.float32) * (-slope)
            if selbias is not None:
                rowb = rowb + selbias[j:j + 1, :]
        ts.append(t)
        rowbs.append(rowb)
        colmax.append(jnp.max(t, axis=0, keepdims=True) + rowb)
    m = functools.reduce(jnp.maximum, colmax)
    l, acc = None, None
    for j in range(i + 1):
        p = jnp.exp(ts[j] - (m - rowbs[j]))
        lj = jnp.sum(p, axis=0, keepdims=True)
        aj = _dot(vt_load(j), p.astype(jnp.bfloat16))
        l = lj if l is None else l + lj
        acc = aj if acc is None else acc + aj
    return acc / l


def _head_queries(q2):
    lane = lax.broadcasted_iota(jnp.int32, q2.shape, 1)
    return [jnp.where((lane >= hh * HEAD_DIM) & (lane < (hh + 1) * HEAD_DIM), q2, jnp.zeros_like(q2))
            for hh in range(2)]


def _block_selection(gate, i):
    blk = lax.broadcasted_iota(jnp.int32, gate.shape, 0)
    cnt = jnp.zeros(gate.shape, jnp.int32)
    for mblk in range(i):
        gm = gate[mblk:mblk + 1, :]
        beats = (gm > gate) | ((gm == gate) & (mblk < blk))
        cnt = cnt + jnp.where(beats, 1, 0)
    return jnp.where((blk < i) & (cnt < MOBA_TOPK), 0.0, NEG_INF)


def _moba_kernel(slopes_ref, q_ref, k_ref, vt_ref, kmean_ref, o_ref):
    pair = pl.program_id(1)
    step = pl.program_id(2)
    nq = vt_ref.shape[0]
    rel = _rel_qk()
    km_hi, km_lo = _split_bf16(kmean_ref[...])
    for st in range(nq // ATT_QBLOCKS):
        @pl.when(step == st)
        def _(st=st):
            for u in range(ATT_QBLOCKS):
                i = st * ATT_QBLOCKS + u
                rows = slice(u * MOBA_BLOCK, (u + 1) * MOBA_BLOCK)
                outs = []
                for hh, q_h in enumerate(_head_queries(q_ref[rows, :])):
                    gate = _dot_nt(km_hi, q_h) + _dot_nt(km_lo, q_h)
                    vt_load = lambda j, hh=hh: vt_ref[j, hh * HEAD_DIM:(hh + 1) * HEAD_DIM, :]
                    outs.append(_attend(q_h, k_ref, vt_load, i, slopes_ref[pair, hh], rel,
                                        _block_selection(gate, i)))
                o_t = jnp.concatenate(outs, axis=0)
                o_ref[rows, :] = o_t.T.astype(o_ref.dtype)


def _diff_kernel(slopes_ref, lam_ref, q_ref, k_ref, vt_ref, gsub_ref, o_ref, *, lam_init):
    head = pl.program_id(1)
    step = pl.program_id(2)
    nq = vt_ref.shape[0]
    rel = _rel_qk()
    slope = slopes_ref[head, 0]
    lp = lam_ref[...]
    lam = (jnp.exp(jnp.sum(lp[0:1] * lp[1:2], axis=-1, keepdims=True))
           - jnp.exp(jnp.sum(lp[2:3] * lp[3:4], axis=-1, keepdims=True)) + lam_init)
    vt_load = lambda j: vt_ref[j]
    for st in range(nq // ATT_QBLOCKS):
        @pl.when(step == st)
        def _(st=st):
            for u in range(ATT_QBLOCKS):
                i = st * ATT_QBLOCKS + u
                rows = slice(u * MOBA_BLOCK, (u + 1) * MOBA_BLOCK)
                parts = [_attend(q_h, k_ref, vt_load, i, slope, rel, None)
                         for q_h in _head_queries(q_ref[rows, :])]
                o_t = parts[0] - lam * parts[1]
                ms = jnp.mean(o_t * o_t, axis=0, keepdims=True)
                o_t = o_t * lax.rsqrt(ms + RMS_EPS) * gsub_ref[...] * (1.0 - lam_init)
                o_ref[rows, :] = o_t.T.astype(o_ref.dtype)


def _attention_specs(nq):
    nst = nq // ATT_QBLOCKS
    q_spec = pl.BlockSpec((ATT_QBLOCKS * MOBA_BLOCK, LANES), lambda b, p, i: (b * nst + i, p))
    k_spec = pl.BlockSpec((nq * MOBA_BLOCK, LANES), lambda b, p, i: (b, p))
    vt_spec = pl.BlockSpec((nq, LANES, MOBA_BLOCK), lambda b, p, i: (b, p, 0))
    return q_spec, k_spec, vt_spec


def _moba_attention(q, k, vt, kmean, slopes, batch, seq):
    nq = seq // MOBA_BLOCK
    q_spec, k_spec, vt_spec = _attention_specs(nq)
    return pl.pallas_call(
        _moba_kernel,
        grid=(batch, MOBA_HEADS // 2, nq // ATT_QBLOCKS),
        in_specs=[
            pl.BlockSpec(memory_space=pltpu.SMEM),
            q_spec, k_spec, vt_spec,
            pl.BlockSpec((SUBLANES, LANES), lambda b, p, i: (b, p)),
        ],
        out_specs=q_spec,
        out_shape=jax.ShapeDtypeStruct(q.shape, jnp.bfloat16),
        compiler_params=pltpu.CompilerParams(
            dimension_semantics=("arbitrary", "arbitrary", "arbitrary"), vmem_limit_bytes=VMEM_LIMIT),
        name="moba",
    )(slopes, q, k, vt, kmean)


def _diff_attention(q, k, vt, lam_p, gsub, slopes, lam_init, batch, seq):
    nq = seq // MOBA_BLOCK
    q_spec, k_spec, vt_spec = _attention_specs(nq)
    return pl.pallas_call(
        functools.partial(_diff_kernel, lam_init=lam_init),
        grid=(batch, DIFF_HEADS, nq // ATT_QBLOCKS),
        in_specs=[
            pl.BlockSpec(memory_space=pltpu.SMEM),
            pl.BlockSpec((4, HEAD_DIM), lambda b, p, i: (0, 0)),
            q_spec, k_spec, vt_spec,
            pl.BlockSpec((LANES, MOBA_BLOCK), lambda b, p, i: (0, 0)),
        ],
        out_specs=q_spec,
        out_shape=jax.ShapeDtypeStruct(q.shape, jnp.bfloat16),
        compiler_params=pltpu.CompilerParams(
            dimension_semantics=("arbitrary", "arbitrary", "arbitrary"), vmem_limit_bytes=VMEM_LIMIT),
        name="diff",
    )(slopes, lam_p, q, k, vt, gsub)


def _knockout_topk(sc, k, payload=None):
    n = sc.shape[0]
    row = lax.broadcasted_iota(jnp.int32, sc.shape, 0).astype(jnp.float32)
    vals, picks = [], []
    for _ in range(k):
        m = jnp.max(sc, axis=0, keepdims=True)
        idx = jnp.min(jnp.where(sc == m, row, float(n)), axis=0, keepdims=True)
        hit = row == idx
        vals.append(m)
        picks.append(idx if payload is None else jnp.sum(jnp.where(hit, payload, 0.0), axis=0, keepdims=True))
        sc = jnp.where(hit, NEG_INF, sc)
    return vals, picks


def _mix_kernel(x_ref, am_ref, ad_ref, wo_ref, g2_ref, wq_ref, keys_ref,
                x1_ref, xn_ref, exp_ref, gate_ref):
    x1 = x_ref[...] + _dot(am_ref[...], wo_ref[0:512, :]) + _dot(ad_ref[...], wo_ref[512:1024, :])
    x1_ref[...] = x1
    ms = jnp.mean(x1 * x1, axis=-1, keepdims=True)
    xn = x1 * lax.rsqrt(ms + RMS_EPS) * g2_ref[...]
    xn_ref[...] = xn
    qq = _dot(xn.astype(jnp.bfloat16), wq_ref[...]).astype(jnp.bfloat16)
    tm = qq.shape[0]
    exp_rows, gate_rows = [], []
    for h in range(PEER_HEADS):
        tops = []
        for half in range(2):
            s = h * 2 + half
            sc = _dot_nt(keys_ref[s], qq[:, s * LANES:(s + 1) * LANES])
            tops.append(_knockout_topk(sc, PEER_TOPK))
        (s1, i1), (s2, i2) = tops
        s2a = jnp.concatenate(s2, axis=0)
        i2a = jnp.concatenate(i2, axis=0)
        cand, cidx = [], []
        for a in range(PEER_TOPK):
            nb = PEER_TOPK // (a + 1)
            cand.append(s1[a] + s2a[0:nb])
            cidx.append(i1[a] * float(PEER_N_KEYS) + i2a[0:nb])
        n_cand = sum(c.shape[0] for c in cand)
        pad = -n_cand % SUBLANES
        cand.append(jnp.full((pad, tm), NEG_INF, jnp.float32))
        cidx.append(jnp.zeros((pad, tm), jnp.float32))
        best, experts = _knockout_topk(jnp.concatenate(cand, axis=0), PEER_TOPK,
                                       payload=jnp.concatenate(cidx, axis=0))
        b = jnp.concatenate(best, axis=0)
        e = jnp.exp(b - b[0:1])
        gate_rows.append(e / jnp.sum(e, axis=0, keepdims=True))
        exp_rows.append(jnp.concatenate(experts, axis=0))
    exp_ref[...] = jnp.concatenate(exp_rows, axis=0).T.astype(jnp.int32)
    gate_ref[...] = jnp.concatenate(gate_rows, axis=0).T


def _mix(x2, am, ad, wo, g2, wq, keys):
    T = x2.shape[0]
    row = lambda i: (i, 0)
    const = lambda i: (0, 0)
    return pl.pallas_call(
        _mix_kernel,
        grid=(T // MIX_TM,),
        in_specs=[
            pl.BlockSpec((MIX_TM, 1024), row),
            pl.BlockSpec((MIX_TM, 512), row),
            pl.BlockSpec((MIX_TM, 512), row),
            pl.BlockSpec((1024, 1024), const),
            pl.BlockSpec((1, 1024), const),
            pl.BlockSpec((1024, 2048), const),
            pl.BlockSpec((2 * PEER_HEADS, PEER_N_KEYS, LANES), lambda i: (0, 0, 0)),
        ],
        out_specs=[
            pl.BlockSpec((MIX_TM, 1024), row),
            pl.BlockSpec((MIX_TM, 1024), row),
            pl.BlockSpec((MIX_TM, LANES), row),
            pl.BlockSpec((MIX_TM, LANES), row),
        ],
        out_shape=[
            jax.ShapeDtypeStruct((T, 1024), jnp.float32),
            jax.ShapeDtypeStruct((T, 1024), jnp.float32),
            jax.ShapeDtypeStruct((T, LANES), jnp.int32),
            jax.ShapeDtypeStruct((T, LANES), jnp.float32),
        ],
        compiler_params=pltpu.CompilerParams(dimension_semantics=("arbitrary",), vmem_limit_bytes=VMEM_LIMIT),
        name="mix",
    )(x2, am, ad, wo, g2, wq, keys)


def _pack_kernel(t_ref, o_ref):
    t = t_ref[...]
    lo = pltpu.bitcast(t[:, 0:512].astype(jnp.bfloat16).astype(jnp.float32), jnp.int32)
    hi = pltpu.bitcast(t[:, 512:1024].astype(jnp.bfloat16).astype(jnp.float32), jnp.int32)
    o_ref[...] = lax.shift_right_logical(lo, 16) | (hi & jnp.int32(-65536))


def _pack_table(tab):
    n = tab.shape[0]
    return pl.pallas_call(
        _pack_kernel,
        grid=(n // PACK_ROWS,),
        in_specs=[pl.BlockSpec((PACK_ROWS, 1024), lambda i: (i, 0))],
        out_specs=pl.BlockSpec((PACK_ROWS, 512), lambda i: (i, 0)),
        out_shape=jax.ShapeDtypeStruct((n, 512), jnp.int32),
        compiler_params=pltpu.CompilerParams(dimension_semantics=("arbitrary",)),
        name="pack",
    )(tab)


def _decode(w):
    lo = pltpu.bitcast(w << 16, jnp.float32)
    hi = pltpu.bitcast(w & jnp.int32(-65536), jnp.float32)
    return lo, hi


def _peer_u_kernel(exp_ref, tab_ref, x_ref, g_ref, act_ref, q0_ref, q1_ref, qs_ref):
    ntok, nsel = exp_ref.shape
    ones = jnp.ones((SUBLANES, LANES), jnp.bfloat16)

    def products(t, q_ref):
        x = x_ref[t]
        xlo = x[0:4]
        xhi = x[4:8]
        for e in range(nsel):
            lo, hi = _decode(tab_ref[exp_ref[t, e]])
            q_ref[pl.ds(4 * e, 4), :] = lo * xlo + hi * xhi

    def fold(t, q_ref):
        q = q_ref[pl.ds(0, nsel, stride=4), :]
        for s in range(1, 4):
            q = q + q_ref[pl.ds(s, nsel, stride=4), :]
        qs_ref[pl.ds(pl.multiple_of(t * nsel, nsel), nsel), :] = q

    def gather_group(g):
        for k in range(PEER_GROUP):
            q_ref = (q0_ref, q1_ref)[k % 2]
            products(g * PEER_GROUP + k, q_ref)
            fold(g * PEER_GROUP + k, q_ref)

    def finish_group(g):
        rows8 = PEER_GROUP * nsel
        q_hi, q_lo = _split_bf16(qs_ref[pl.ds(pl.multiple_of(g * rows8, rows8), rows8), :])
        row = (_dot_nt(ones, q_hi) + _dot_nt(ones, q_lo))[0:1]
        h = jnp.concatenate([row[:, k * nsel:(k + 1) * nsel] for k in range(PEER_GROUP)], axis=0)
        gelu = 0.5 * h * (1.0 + lax.erf(h * (1.0 / math.sqrt(2.0))))
        rows = pl.ds(pl.multiple_of(g * PEER_GROUP, PEER_GROUP), PEER_GROUP)
        act_ref[rows, :] = gelu * g_ref[rows, :]

    ngroups = ntok // PEER_GROUP
    gather_group(0)

    def body(g, carry):
        finish_group(g - 1)
        gather_group(g)
        return carry

    lax.fori_loop(1, ngroups, body, 0)
    finish_group(ngroups - 1)


def _sc_peer_v(vtab, idx, act, x1, tok0, ntok):
    d = x1.shape[1]
    nsel = idx.shape[1]
    info = plsc.get_sparse_core_info()
    nw = info.num_cores * info.num_subcores
    per_w = ntok // nw
    nchunk = nsel // SC_ROWS
    nitem = SC_TB * nchunk
    assert info.num_lanes == SC_LANES and ntok % (nw * SC_TB) == 0 and nitem % SC_NBUF == 0
    mesh = plsc.VectorSubcoreMesh(core_axis_name="c", subcore_axis_name="s")

    @functools.partial(
        pl.kernel, mesh=mesh,
        out_type=jax.ShapeDtypeStruct((ntok, d), jnp.float32),
        scratch_types=[
            pltpu.VMEM((SC_TB, nsel), jnp.int32),
            pltpu.VMEM((SC_TB, nsel), jnp.float32),
            pltpu.VMEM((SC_TB, d), jnp.float32),
        ] + [pltpu.VMEM((SC_ROWS, d), jnp.float32)] * SC_NBUF + [pltpu.SemaphoreType.DMA] * SC_NBUF,
        name="sc_peer_v",
    )
    def k(vtab_hbm, idx_hbm, act_hbm, x1_hbm, out_hbm, idx_v, act_v, acc_v, *bufs_sems):
        bufs, sems = bufs_sems[:SC_NBUF], bufs_sems[SC_NBUF:]
        wid = lax.axis_index("s") * info.num_cores + lax.axis_index("c")

        def gather(item, rows, sem):
            tl = item // nchunk
            ch = item % nchunk
            return pltpu.make_async_copy(vtab_hbm.at[idx_v.at[tl, pl.ds(ch * SC_ROWS, SC_ROWS)]], rows, sem)

        def accumulate(item, rows):
            tl = item // nchunk
            r0 = (item % nchunk) * SC_ROWS
            a = []
            for e0 in range(0, SC_ROWS, SC_LANES):
                w16 = act_v[tl, pl.ds(r0 + e0, SC_LANES)]
                a += [jnp.full((SC_LANES,), w16[e], jnp.float32) for e in range(SC_LANES)]

            @plsc.parallel_loop(0, d // SC_LANES, 1, unroll=2)
            def _(c):
                col = pl.ds(c * SC_LANES, SC_LANES)
                terms = [a[e] * rows[e, col] for e in range(SC_ROWS)]
                while len(terms) > 1:
                    terms = [terms[j] + terms[j + 1] for j in range(0, len(terms), 2)]
                acc_v[tl, col] = acc_v[tl, col] + terms[0]

        @pl.loop(0, per_w // SC_TB)
        def _(bi):
            loc = pl.multiple_of(wid * per_w + bi * SC_TB, SC_TB)
            src = pl.ds(tok0 + loc, SC_TB)
            pltpu.sync_copy(idx_hbm.at[src], idx_v)
            pltpu.sync_copy(act_hbm.at[pl.ds(loc, SC_TB)], act_v)
            pltpu.sync_copy(x1_hbm.at[src], acc_v)
            for j in range(SC_NBUF - 1):
                gather(j, bufs[j], sems[j]).start()

            @pl.loop(0, nitem // SC_NBUF)
            def _(p):
                for j in range(SC_NBUF):
                    item = p * SC_NBUF + j
                    nxt = (j + SC_NBUF - 1) % SC_NBUF

                    @pl.when(item + SC_NBUF - 1 < nitem)
                    def _():
                        gather(item + SC_NBUF - 1, bufs[nxt], sems[nxt]).start()

                    gather(item, bufs[j], sems[j]).wait()
                    accumulate(item, bufs[j])

            pltpu.sync_copy(acc_v, out_hbm.at[pl.ds(loc, SC_TB)])

    return k(vtab, idx, act, x1)


def _peer_v_kernel(exp_ref, act_ref, tab_ref, x_ref, o_ref):
    nsel = exp_ref.shape[1]

    def tok(t, carry):
        acc_lo = jnp.zeros((4, LANES), jnp.float32)
        acc_hi = jnp.zeros((4, LANES), jnp.float32)
        for e in range(nsel):
            a = act_ref[t, e]
            lo, hi = _decode(tab_ref[exp_ref[t, e]])
            acc_lo = acc_lo + a * lo
            acc_hi = acc_hi + a * hi
        o_ref[t] = x_ref[t] + jnp.concatenate([acc_lo, acc_hi], axis=0)
        return carry

    lax.fori_loop(0, x_ref.shape[0], tok, 0)


def _peer_experts(xn3, x1, experts, gates, u_packed, v_table, v_packed):
    T, nsel = experts.shape
    n_exp = u_packed.shape[0]
    d = x1.shape[1]
    unit = T // sum(u for u, _ in PEER_PLAN)
    assert nsel == LANES and PEER_TT % PEER_GROUP == 0 and unit % PEER_TT == 0
    tab_spec = pl.BlockSpec((n_exp, 4, LANES), lambda i: (0, 0, 0), pipeline_mode=pl.Buffered(1))
    cp = pltpu.CompilerParams(dimension_semantics=("arbitrary",), vmem_limit_bytes=VMEM_LIMIT)
    x13 = x1.reshape(T, SUBLANES, LANES)
    outs = []
    tok0 = 0
    for units, on_sc in PEER_PLAN:
        tc = units * unit
        steps = tc // PEER_TT
        off = tok0 // PEER_TT
        row = lambda i, off=off: (i + off, 0)
        row3 = lambda i, off=off: (i + off, 0, 0)
        act = pl.pallas_call(
            _peer_u_kernel,
            grid=(steps,),
            in_specs=[pl.BlockSpec((PEER_TT, nsel), row, memory_space=pltpu.SMEM), tab_spec,
                      pl.BlockSpec((PEER_TT, SUBLANES, LANES), row3), pl.BlockSpec((PEER_TT, nsel), row)],
            out_specs=pl.BlockSpec((PEER_TT, nsel), lambda i: (i, 0)),
            out_shape=jax.ShapeDtypeStruct((tc, nsel), jnp.float32),
            scratch_shapes=[pltpu.VMEM((4 * nsel, LANES), jnp.float32), pltpu.VMEM((4 * nsel, LANES), jnp.float32),
                            pltpu.VMEM((PEER_TT * nsel, LANES), jnp.float32)],
            compiler_params=cp,
            name="peer_u",
        )(experts, u_packed, xn3, gates)
        if on_sc:
            outs.append(_sc_peer_v(v_table, experts, act, x1, tok0, tc))
        else:
            out3 = pl.pallas_call(
                _peer_v_kernel,
                grid=(steps,),
                in_specs=[pl.BlockSpec((PEER_TT, nsel), row, memory_space=pltpu.SMEM),
                          pl.BlockSpec((PEER_TT, nsel), lambda i: (i, 0), memory_space=pltpu.SMEM),
                          tab_spec, pl.BlockSpec((PEER_TT, SUBLANES, LANES), row3)],
                out_specs=pl.BlockSpec((PEER_TT, SUBLANES, LANES), lambda i: (i, 0, 0)),
                out_shape=jax.ShapeDtypeStruct((tc, SUBLANES, LANES), jnp.float32),
                compiler_params=cp,
                name="peer_v",
            )(experts, act, v_packed, x13)
            outs.append(out3.reshape(tc, d))
        tok0 += tc
    return jnp.concatenate(outs, axis=0)


def _alibi_slopes(n):
    return 2.0 ** (-8.0 * np.arange(1, n + 1) / n)


def _lambda_init(layer):
    return 0.8 - 0.6 * math.exp(-0.3 * layer)


def kernel(x, norm1_g, w_in, moba_qk_norm_g, diff_qk_norm_g, diff_lambda, diff_subln_g, w_out, norm2_g,
           peer_w_query, peer_sub_keys, peer_u, peer_v):
    b, s, d = x.shape
    T = b * s
    depth = w_in.shape[0]
    mw = MOBA_HEADS * HEAD_DIM
    assert d == 1024 and s % PROJ_TM == 0 and mw == 512 and DIFF_HEADS * 2 * HEAD_DIM == 512
    f32, bf16 = jnp.float32, jnp.bfloat16
    scale = HEAD_DIM ** -0.5
    grp = lax.broadcasted_iota(jnp.int32, (LANES, LANES), 0) // HEAD_DIM
    gsum = (grp == grp.T).astype(bf16)
    moba_slopes = jnp.asarray(_alibi_slopes(MOBA_HEADS).reshape(MOBA_HEADS // 2, 2), f32)
    diff_slopes = jnp.asarray(np.repeat(_alibi_slopes(DIFF_HEADS)[:, None], 2, axis=1), f32)
    nblk = PROJ_TM // MOBA_BLOCK
    x2 = x.reshape(T, d)
    for l in range(depth):
        w = w_in[l]
        wqk = jnp.concatenate([w[:, 0:2 * mw], w[:, 3 * mw:3 * mw + 1024]], axis=1).astype(bf16)
        wvt = jnp.concatenate([w[:, 2 * mw:3 * mw], w[:, 3 * mw + 1024:]], axis=1).T.astype(bf16)
        mg, dg = moba_qk_norm_g[l].astype(f32), diff_qk_norm_g[l].astype(f32)
        gqk = jnp.concatenate([jnp.tile(mg[0] * scale, 8), jnp.tile(mg[1], 8),
                               jnp.tile(dg[0] * scale, 8), jnp.tile(dg[1], 8)])[None, :]
        g1 = norm1_g[l][None, :].astype(f32)
        g2 = norm2_g[l][None, :].astype(f32)
        gsub = jnp.broadcast_to(diff_subln_g[l].astype(f32)[:, None], (LANES, MOBA_BLOCK))
        lam_p = diff_lambda[l].astype(f32)
        wo, wq = w_out[l].astype(bf16), peer_w_query[l].astype(bf16)
        keys = peer_sub_keys[l].reshape(2 * PEER_HEADS, PEER_N_KEYS, LANES).astype(bf16)
        u_packed = _pack_table(peer_u[l]).reshape(peer_u.shape[1], 4, LANES)
        v_packed = _pack_table(peer_v[l]).reshape(peer_v.shape[1], 4, LANES)
        v_table = peer_v[l].astype(f32)
        mq, mk, dq, dk, mvt, dvt, kmean = _project(x2, g1, wqk, wvt, gqk, gsum)
        kmean = kmean.reshape(T // PROJ_TM, SUBLANES, 512)[:, :nblk].reshape(T // MOBA_BLOCK, 512)
        moba_o = _moba_attention(mq, mk, mvt, kmean, moba_slopes, b, s)
        diff_o = _diff_attention(dq, dk, dvt, lam_p, gsub, diff_slopes, _lambda_init(l), b, s)
        x1, xn, experts, gates = _mix(x2, moba_o, diff_o, wo, g2, wq, keys)
        x2 = _peer_experts(xn.reshape(T, SUBLANES, LANES), x1, experts, gates, u_packed, v_table, v_packed)
    return x2.reshape(b, s, d)
```

```python
import functools
import math

import jax
import jax.numpy as jnp
import numpy as np
from jax import lax
from jax.experimental import pallas as pl
from jax.experimental.pallas import tpu as pltpu
from jax.experimental.pallas import tpu_sc as plsc

HEAD_DIM = 64
MOBA_HEADS = 8
MOBA_BLOCK = 256
MOBA_TOPK = 3
DIFF_HEADS = 4
PEER_HEADS = 8
PEER_N_KEYS = 128
PEER_TOPK = 16
RMS_EPS = 1e-6

LANES = 128
SUBLANES = 8
VMEM_LIMIT = 48 * 1024 * 1024

PROJ_TM = 512
MIX_TM = 256
PEER_TT = 64
PACK_ROWS = 512
ATT_QBLOCKS = 4
PEER_GROUP = 8
PEER_PLAN = ((1, True), (2, True), (2, True), (2, True), (2, True), (2, True), (2, True), (2, True), (1, False))
SC_LANES = 16
SC_ROWS = 16
SC_NBUF = 4
SC_TB = 32

NEG_INF = float("-inf")
_NT = (((1,), (1,)), ((), ()))


def _dot(a, b):
    return jnp.dot(a, b, preferred_element_type=jnp.float32)


def _dot_nt(a, b):
    return lax.dot_general(a, b, _NT, preferred_element_type=jnp.float32)


def _split_bf16(x):
    hi = x.astype(jnp.bfloat16)
    lo = (x - hi.astype(jnp.float32)).astype(jnp.bfloat16)
    return hi, lo


def _proj_kernel(x_ref, g1_ref, wqk_ref, wvt_ref, gqk_ref, gsum_ref,
                 mq_ref, mk_ref, dq_ref, dk_ref, mvt_ref, dvt_ref, kmean_ref):
    x = x_ref[...]
    ms = jnp.mean(x * x, axis=-1, keepdims=True)
    h = (x * lax.rsqrt(ms + RMS_EPS) * g1_ref[...]).astype(jnp.bfloat16)
    gsum = gsum_ref[...]
    outs = (mq_ref, mk_ref, dq_ref, dk_ref)
    for sec in range(4):
        for half in range(2):
            c0 = sec * 512 + half * 256
            p = _dot(h, wqk_ref[:, c0:c0 + 256])
            for s in range(2):
                ps = p[:, s * LANES:(s + 1) * LANES]
                sq_hi, sq_lo = _split_bf16(ps * ps)
                ssq = _dot(sq_hi, gsum) + _dot(sq_lo, gsum)
                g = gqk_ref[:, c0 + s * LANES:c0 + (s + 1) * LANES]
                y = ps * lax.rsqrt(ssq * (1.0 / HEAD_DIM) + RMS_EPS) * g
                l0 = half * 256 + s * LANES
                outs[sec][:, l0:l0 + LANES] = y.astype(jnp.bfloat16)
                if sec == 1:
                    for blk in range(PROJ_TM // MOBA_BLOCK):
                        kmean_ref[blk:blk + 1, l0:l0 + LANES] = jnp.mean(
                            y[blk * MOBA_BLOCK:(blk + 1) * MOBA_BLOCK], axis=0, keepdims=True)
    kmean_ref[PROJ_TM // MOBA_BLOCK:, :] = jnp.zeros((SUBLANES - PROJ_TM // MOBA_BLOCK, 512), jnp.float32)
    vt = _dot_nt(wvt_ref[...], h).astype(jnp.bfloat16)
    for blk in range(PROJ_TM // MOBA_BLOCK):
        mvt_ref[blk] = vt[0:512, blk * MOBA_BLOCK:(blk + 1) * MOBA_BLOCK]
        dvt_ref[blk] = vt[512:1024, blk * MOBA_BLOCK:(blk + 1) * MOBA_BLOCK]


def _project(x2, g1, wqk, wvt, gqk, gsum):
    T = x2.shape[0]
    nt = T // PROJ_TM
    nblk = PROJ_TM // MOBA_BLOCK
    row = lambda i: (i, 0)
    const = lambda i: (0, 0)
    qk_shape = jax.ShapeDtypeStruct((T, 512), jnp.bfloat16)
    vt_shape = jax.ShapeDtypeStruct((T // MOBA_BLOCK, 512, MOBA_BLOCK), jnp.bfloat16)
    return pl.pallas_call(
        _proj_kernel,
        grid=(nt,),
        in_specs=[
            pl.BlockSpec((PROJ_TM, 1024), row),
            pl.BlockSpec((1, 1024), const),
            pl.BlockSpec((1024, 2048), const),
            pl.BlockSpec((1024, 1024), const),
            pl.BlockSpec((1, 2048), const),
            pl.BlockSpec((LANES, LANES), const),
        ],
        out_specs=[
            pl.BlockSpec((PROJ_TM, 512), row),
            pl.BlockSpec((PROJ_TM, 512), row),
            pl.BlockSpec((PROJ_TM, 512), row),
            pl.BlockSpec((PROJ_TM, 512), row),
            pl.BlockSpec((nblk, 512, MOBA_BLOCK), lambda i: (i, 0, 0)),
            pl.BlockSpec((nblk, 512, MOBA_BLOCK), lambda i: (i, 0, 0)),
            pl.BlockSpec((SUBLANES, 512), row),
        ],
        out_shape=[qk_shape, qk_shape, qk_shape, qk_shape, vt_shape, vt_shape,
                   jax.ShapeDtypeStruct((nt * SUBLANES, 512), jnp.float32)],
        compiler_params=pltpu.CompilerParams(dimension_semantics=("arbitrary",), vmem_limit_bytes=VMEM_LIMIT),
        name="proj",
    )(x2, g1, wqk, wvt, gqk, gsum)


def _rel_qk():
    qc = lax.broadcasted_iota(jnp.int32, (MOBA_BLOCK, MOBA_BLOCK), 1)
    kr = lax.broadcasted_iota(jnp.int32, (MOBA_BLOCK, MOBA_BLOCK), 0)
    return qc - kr


def _attend(q_h, k_ref, vt_load, i, slope, rel, selbias):
    nb = rel.astype(jnp.float32) * (-slope)
    ts, rowbs, colmax = [], [], []
    for j in range(i + 1):
        t = _dot_nt(k_ref[j * MOBA_BLOCK:(j + 1) * MOBA_BLOCK, :], q_h) + nb
        if j == i:
            t = jnp.where(rel >= 0, t, NEG_INF)
            rowb = jnp.zeros((1, MOBA_BLOCK), jnp.float32)
        else:
            rowb = jnp.full((1, MOBA_BLOCK), (i - j) * MOBA_BLOCK, jnp.float32) * (-slope)
            if selbias is not None:
                rowb = rowb + selbias[j:j + 1, :]
        ts.append(t)
        rowbs.append(rowb)
        colmax.append(jnp.max(t, axis=0, keepdims=True) + rowb)
    m = functools.reduce(jnp.maximum, colmax)
    l, acc = None, None
    for j in range(i + 1):
        p = jnp.exp(ts[j] - (m - rowbs[j]))
        lj = jnp.sum(p, axis=0, keepdims=True)
        aj = _dot(vt_load(j), p.astype(jnp.bfloat16))
        l = lj if l is None else l + lj
        acc = aj if acc is None else acc + aj
    return acc / l


def _head_queries(q2):
    lane = lax.broadcasted_iota(jnp.int32, q2.shape, 1)
    return [jnp.where((lane >= hh * HEAD_DIM) & (lane < (hh + 1) * HEAD_DIM), q2, jnp.zeros_like(q2))
            for hh in range(2)]


def _block_selection(gate, i):
    blk = lax.broadcasted_iota(jnp.int32, gate.shape, 0)
    cnt = jnp.zeros(gate.shape, jnp.int32)
    for mblk in range(i):
        gm = gate[mblk:mblk + 1, :]
        beats = (gm > gate) | ((gm == gate) & (mblk < blk))
        cnt = cnt + jnp.where(beats, 1, 0)
    return jnp.where((blk < i) & (cnt < MOBA_TOPK), 0.0, NEG_INF)


def _moba_kernel(slopes_ref, q_ref, k_ref, vt_ref, kmean_ref, o_ref):
    pair = pl.program_id(1)
    step = pl.program_id(2)
    nq = vt_ref.shape[0]
    rel = _rel_qk()
    km_hi, km_lo = _split_bf16(kmean_ref[...])
    for st in range(nq // ATT_QBLOCKS):
        @pl.when(step == st)
        def _(st=st):
            for u in range(ATT_QBLOCKS):
                i = st * ATT_QBLOCKS + u
                rows = slice(u * MOBA_BLOCK, (u + 1) * MOBA_BLOCK)
                outs = []
                for hh, q_h in enumerate(_head_queries(q_ref[rows, :])):
                    gate = _dot_nt(km_hi, q_h) + _dot_nt(km_lo, q_h)
                    vt_load = lambda j, hh=hh: vt_ref[j, hh * HEAD_DIM:(hh + 1) * HEAD_DIM, :]
                    outs.append(_attend(q_h, k_ref, vt_load, i, slopes_ref[pair, hh], rel,
                                        _block_selection(gate, i)))
                o_t = jnp.concatenate(outs, axis=0)
                o_ref[rows, :] = o_t.T.astype(o_ref.dtype)


def _diff_kernel(slopes_ref, lam_ref, q_ref, k_ref, vt_ref, gsub_ref, o_ref, *, lam_init):
    head = pl.program_id(1)
    step = pl.program_id(2)
    nq = vt_ref.shape[0]
    rel = _rel_qk()
    slope = slopes_ref[head, 0]
    lp = lam_ref[...]
    lam = (jnp.exp(jnp.sum(lp[0:1] * lp[1:2], axis=-1, keepdims=True))
           - jnp.exp(jnp.sum(lp[2:3] * lp[3:4], axis=-1, keepdims=True)) + lam_init)
    vt_load = lambda j: vt_ref[j]
    for st in range(nq // ATT_QBLOCKS):
        @pl.when(step == st)
        def _(st=st):
            for u in range(ATT_QBLOCKS):
                i = st * ATT_QBLOCKS + u
                rows = slice(u * MOBA_BLOCK, (u + 1) * MOBA_BLOCK)
                parts = [_attend(q_h, k_ref, vt_load, i, slope, rel, None)
                         for q_h in _head_queries(q_ref[rows, :])]
                o_t = parts[0] - lam * parts[1]
                ms = jnp.mean(o_t * o_t, axis=0, keepdims=True)
                o_t = o_t * lax.rsqrt(ms + RMS_EPS) * gsub_ref[...] * (1.0 - lam_init)
                o_ref[rows, :] = o_t.T.astype(o_ref.dtype)


def _attention_specs(nq):
    nst = nq // ATT_QBLOCKS
    q_spec = pl.BlockSpec((ATT_QBLOCKS * MOBA_BLOCK, LANES), lambda b, p, i: (b * nst + i, p))
    k_spec = pl.BlockSpec((nq * MOBA_BLOCK, LANES), lambda b, p, i: (b, p))
    vt_spec = pl.BlockSpec((nq, LANES, MOBA_BLOCK), lambda b, p, i: (b, p, 0))
    return q_spec, k_spec, vt_spec


def _moba_attention(q, k, vt, kmean, slopes, batch, seq):
    nq = seq // MOBA_BLOCK
    q_spec, k_spec, vt_spec = _attention_specs(nq)
    return pl.pallas_call(
        _moba_kernel,
        grid=(batch, MOBA_HEADS // 2, nq // ATT_QBLOCKS),
        in_specs=[
            pl.BlockSpec(memory_space=pltpu.SMEM),
            q_spec, k_spec, vt_spec,
            pl.BlockSpec((SUBLANES, LANES), lambda b, p, i: (b, p)),
        ],
        out_specs=q_spec,
        out_shape=jax.ShapeDtypeStruct(q.shape, jnp.bfloat16),
        compiler_params=pltpu.CompilerParams(
            dimension_semantics=("arbitrary", "arbitrary", "arbitrary"), vmem_limit_bytes=VMEM_LIMIT),
        name="moba",
    )(slopes, q, k, vt, kmean)


def _diff_attention(q, k, vt, lam_p, gsub, slopes, lam_init, batch, seq):
    nq = seq // MOBA_BLOCK
    q_spec, k_spec, vt_spec = _attention_specs(nq)
    return pl.pallas_call(
        functools.partial(_diff_kernel, lam_init=lam_init),
        grid=(batch, DIFF_HEADS, nq // ATT_QBLOCKS),
        in_specs=[
            pl.BlockSpec(memory_space=pltpu.SMEM),
            pl.BlockSpec((4, HEAD_DIM), lambda b, p, i: (0, 0)),
            q_spec, k_spec, vt_spec,
            pl.BlockSpec((LANES, MOBA_BLOCK), lambda b, p, i: (0, 0)),
        ],
        out_specs=q_spec,
        out_shape=jax.ShapeDtypeStruct(q.shape, jnp.bfloat16),
        compiler_params=pltpu.CompilerParams(
            dimension_semantics=("arbitrary", "arbitrary", "arbitrary"), vmem_limit_bytes=VMEM_LIMIT),
        name="diff",
    )(slopes, lam_p, q, k, vt, gsub)


def _knockout_topk(sc, k, payload=None):
    n = sc.shape[0]
    row = lax.broadcasted_iota(jnp.int32, sc.shape, 0).astype(jnp.float32)
    vals, picks = [], []
    for _ in range(k):
        m = jnp.max(sc, axis=0, keepdims=True)
        idx = jnp.min(jnp.where(sc == m, row, float(n)), axis=0, keepdims=True)
        hit = row == idx
        vals.append(m)
        picks.append(idx if payload is None else jnp.sum(jnp.where(hit, payload, 0.0), axis=0, keepdims=True))
        sc = jnp.where(hit, NEG_INF, sc)
    return vals, picks


def _mix_kernel(x_ref, am_ref, ad_ref, wo_ref, g2_ref, wq_ref, keys_ref,
                x1_ref, xn_ref, exp_ref, gate_ref):
    x1 = x_ref[...] + _dot(am_ref[...], wo_ref[0:512, :]) + _dot(ad_ref[...], wo_ref[512:1024, :])
    x1_ref[...] = x1
    ms = jnp.mean(x1 * x1, axis=-1, keepdims=True)
    xn = x1 * lax.rsqrt(ms + RMS_EPS) * g2_ref[...]
    xn_ref[...] = xn
    qq = _dot(xn.astype(jnp.bfloat16), wq_ref[...]).astype(jnp.bfloat16)
    tm = qq.shape[0]
    exp_rows, gate_rows = [], []
    for h in range(PEER_HEADS):
        tops = []
        for half in range(2):
            s = h * 2 + half
            sc = _dot_nt(keys_ref[s], qq[:, s * LANES:(s + 1) * LANES])
            tops.append(_knockout_topk(sc, PEER_TOPK))
        (s1, i1), (s2, i2) = tops
        s2a = jnp.concatenate(s2, axis=0)
        i2a = jnp.concatenate(i2, axis=0)
        cand, cidx = [], []
        for a in range(PEER_TOPK):
            nb = PEER_TOPK // (a + 1)
            cand.append(s1[a] + s2a[0:nb])
            cidx.append(i1[a] * float(PEER_N_KEYS) + i2a[0:nb])
        n_cand = sum(c.shape[0] for c in cand)
        pad = -n_cand % SUBLANES
        cand.append(jnp.full((pad, tm), NEG_INF, jnp.float32))
        cidx.append(jnp.zeros((pad, tm), jnp.float32))
        best, experts = _knockout_topk(jnp.concatenate(cand, axis=0), PEER_TOPK,
                                       payload=jnp.concatenate(cidx, axis=0))
        b = jnp.concatenate(best, axis=0)
        e = jnp.exp(b - b[0:1])
        gate_rows.append(e / jnp.sum(e, axis=0, keepdims=True))
        exp_rows.append(jnp.concatenate(experts, axis=0))
    exp_ref[...] = jnp.concatenate(exp_rows, axis=0).T.astype(jnp.int32)
    gate_ref[...] = jnp.concatenate(gate_rows, axis=0).T


def _mix(x2, am, ad, wo, g2, wq, keys):
    T = x2.shape[0]
    row = lambda i: (i, 0)
    const = lambda i: (0, 0)
    return pl.pallas_call(
        _mix_kernel,
        grid=(T // MIX_TM,),
        in_specs=[
            pl.BlockSpec((MIX_TM, 1024), row),
            pl.BlockSpec((MIX_TM, 512), row),
            pl.BlockSpec((MIX_TM, 512), row),
            pl.BlockSpec((1024, 1024), const),
            pl.BlockSpec((1, 1024), const),
            pl.BlockSpec((1024, 2048), const),
            pl.BlockSpec((2 * PEER_HEADS, PEER_N_KEYS, LANES), lambda i: (0, 0, 0)),
        ],
        out_specs=[
            pl.BlockSpec((MIX_TM, 1024), row),
            pl.BlockSpec((MIX_TM, 1024), row),
            pl.BlockSpec((MIX_TM, LANES), row),
            pl.BlockSpec((MIX_TM, LANES), row),
        ],
        out_shape=[
            jax.ShapeDtypeStruct((T, 1024), jnp.float32),
            jax.ShapeDtypeStruct((T, 1024), jnp.float32),
            jax.ShapeDtypeStruct((T, LANES), jnp.int32),
            jax.ShapeDtypeStruct((T, LANES), jnp.float32),
        ],
        compiler_params=pltpu.CompilerParams(dimension_semantics=("arbitrary",), vmem_limit_bytes=VMEM_LIMIT),
        name="mix",
    )(x2, am, ad, wo, g2, wq, keys)


def _pack_kernel(t_ref, o_ref):
    t = t_ref[...]
    lo = pltpu.bitcast(t[:, 0:512].astype(jnp.bfloat16).astype(jnp.float32), jnp.int32)
    hi = pltpu.bitcast(t[:, 512:1024].astype(jnp.bfloat16).astype(jnp.float32), jnp.int32)
    o_ref[...] = lax.shift_right_logical(lo, 16) | (hi & jnp.int32(-65536))


def _pack_table(tab):
    n = tab.shape[0]
    return pl.pallas_call(
        _pack_kernel,
        grid=(n // PACK_ROWS,),
        in_specs=[pl.BlockSpec((PACK_ROWS, 1024), lambda i: (i, 0))],
        out_specs=pl.BlockSpec((PACK_ROWS, 512), lambda i: (i, 0)),
        out_shape=jax.ShapeDtypeStruct((n, 512), jnp.int32),
        compiler_params=pltpu.CompilerParams(dimension_semantics=("arbitrary",)),
        name="pack",
    )(tab)


def _decode(w):
    lo = pltpu.bitcast(w << 16, jnp.float32)
    hi = pltpu.bitcast(w & jnp.int32(-65536), jnp.float32)
    return lo, hi


def _peer_u_kernel(exp_ref, tab_ref, x_ref, g_ref, act_ref, q0_ref, q1_ref, qs_ref):
    ntok, nsel = exp_ref.shape
    ones = jnp.ones((SUBLANES, LANES), jnp.bfloat16)

    def products(t, q_ref):
        x = x_ref[t]
        xlo = x[0:4]
        xhi = x[4:8]
        for e in range(nsel):
            lo, hi = _decode(tab_ref[exp_ref[t, e]])
            q_ref[pl.ds(4 * e, 4), :] = lo * xlo + hi * xhi

    def fold(t, q_ref):
        q = q_ref[pl.ds(0, nsel, stride=4), :]
        for s in range(1, 4):
            q = q + q_ref[pl.ds(s, nsel, stride=4), :]
        qs_ref[pl.ds(pl.multiple_of(t * nsel, nsel), nsel), :] = q

    def gather_group(g):
        for k in range(PEER_GROUP):
            q_ref = (q0_ref, q1_ref)[k % 2]
            products(g * PEER_GROUP + k, q_ref)
            fold(g * PEER_GROUP + k, q_ref)

    def finish_group(g):
        rows8 = PEER_GROUP * nsel
        q_hi, q_lo = _split_bf16(qs_ref[pl.ds(pl.multiple_of(g * rows8, rows8), rows8), :])
        row = (_dot_nt(ones, q_hi) + _dot_nt(ones, q_lo))[0:1]
        h = jnp.concatenate([row[:, k * nsel:(k + 1) * nsel] for k in range(PEER_GROUP)], axis=0)
        gelu = 0.5 * h * (1.0 + lax.erf(h * (1.0 / math.sqrt(2.0))))
        rows = pl.ds(pl.multiple_of(g * PEER_GROUP, PEER_GROUP), PEER_GROUP)
        act_ref[rows, :] = gelu * g_ref[rows, :]

    ngroups = ntok // PEER_GROUP
    gather_group(0)

    def body(g, carry):
        finish_group(g - 1)
        gather_group(g)
        return carry

    lax.fori_loop(1, ngroups, body, 0)
    finish_group(ngroups - 1)


def _sc_peer_v(vtab, idx, act, x1, tok0, ntok):
    d = x1.shape[1]
    nsel = idx.shape[1]
    info = plsc.get_sparse_core_info()
    nw = info.num_cores * info.num_subcores
    per_w = ntok // nw
    nchunk = nsel // SC_ROWS
    nitem = SC_TB * nchunk
    assert info.num_lanes == SC_LANES and ntok % (nw * SC_TB) == 0 and nitem % SC_NBUF == 0
    mesh = plsc.VectorSubcoreMesh(core_axis_name="c", subcore_axis_name="s")

    @functools.partial(
        pl.kernel, mesh=mesh,
        out_type=jax.ShapeDtypeStruct((ntok, d), jnp.float32),
        scratch_types=[
            pltpu.VMEM((SC_TB, nsel), jnp.int32),
            pltpu.VMEM((SC_TB, nsel), jnp.float32),
            pltpu.VMEM((SC_TB, d), jnp.float32),
        ] + [pltpu.VMEM((SC_ROWS, d), jnp.float32)] * SC_NBUF + [pltpu.SemaphoreType.DMA] * SC_NBUF,
        name="sc_peer_v",
    )
    def k(vtab_hbm, idx_hbm, act_hbm, x1_hbm, out_hbm, idx_v, act_v, acc_v, *bufs_sems):
        bufs, sems = bufs_sems[:SC_NBUF], bufs_sems[SC_NBUF:]
        wid = lax.axis_index("s") * info.num_cores + lax.axis_index("c")

        def gather(item, rows, sem):
            tl = item // nchunk
            ch = item % nchunk
            return pltpu.make_async_copy(vtab_hbm.at[idx_v.at[tl, pl.ds(ch * SC_ROWS, SC_ROWS)]], rows, sem)

        def accumulate(item, rows):
            tl = item // nchunk
            r0 = (item % nchunk) * SC_ROWS
            a = []
            for e0 in range(0, SC_ROWS, SC_LANES):
                w16 = act_v[tl, pl.ds(r0 + e0, SC_LANES)]
                a += [jnp.full((SC_LANES,), w16[e], jnp.float32) for e in range(SC_LANES)]

            @plsc.parallel_loop(0, d // SC_LANES, 1, unroll=2)
            def _(c):
                col = pl.ds(c * SC_LANES, SC_LANES)
                terms = [a[e] * rows[e, col] for e in range(SC_ROWS)]
                while len(terms) > 1:
                    terms = [terms[j] + terms[j + 1] for j in range(0, len(terms), 2)]
                acc_v[tl, col] = acc_v[tl, col] + terms[0]

        @pl.loop(0, per_w // SC_TB)
        def _(bi):
            loc = pl.multiple_of(wid * per_w + bi * SC_TB, SC_TB)
            src = pl.ds(tok0 + loc, SC_TB)
            pltpu.sync_copy(idx_hbm.at[src], idx_v)
            pltpu.sync_copy(act_hbm.at[pl.ds(loc, SC_TB)], act_v)
            pltpu.sync_copy(x1_hbm.at[src], acc_v)
            for j in range(SC_NBUF - 1):
                gather(j, bufs[j], sems[j]).start()

            @pl.loop(0, nitem // SC_NBUF)
            def _(p):
                for j in range(SC_NBUF):
                    item = p * SC_NBUF + j
                    nxt = (j + SC_NBUF - 1) % SC_NBUF

                    @pl.when(item + SC_NBUF - 1 < nitem)
                    def _():
                        gather(item + SC_NBUF - 1, bufs[nxt], sems[nxt]).start()

                    gather(item, bufs[j], sems[j]).wait()
                    accumulate(item, bufs[j])

            pltpu.sync_copy(acc_v, out_hbm.at[pl.ds(loc, SC_TB)])

    return k(vtab, idx, act, x1)


def _peer_v_kernel(exp_ref, act_ref, tab_ref, x_ref, o_ref):
    nsel = exp_ref.shape[1]

    def tok(t, carry):
        acc_lo = jnp.zeros((4, LANES), jnp.float32)
        acc_hi = jnp.zeros((4, LANES), jnp.float32)
        for e in range(nsel):
            a = act_ref[t, e]
            lo, hi = _decode(tab_ref[exp_ref[t, e]])
            acc_lo = acc_lo + a * lo
            acc_hi = acc_hi + a * hi
        o_ref[t] = x_ref[t] + jnp.concatenate([acc_lo, acc_hi], axis=0)
        return carry

    lax.fori_loop(0, x_ref.shape[0], tok, 0)


def _peer_experts(xn3, x1, experts, gates, u_packed, v_table, v_packed):
    T, nsel = experts.shape
    n_exp = u_packed.shape[0]
    d = x1.shape[1]
    unit = T // sum(u for u, _ in PEER_PLAN)
    assert nsel == LANES and PEER_TT % PEER_GROUP == 0 and unit % PEER_TT == 0
    tab_spec = pl.BlockSpec((n_exp, 4, LANES), lambda i: (0, 0, 0), pipeline_mode=pl.Buffered(1))
    cp = pltpu.CompilerParams(dimension_semantics=("arbitrary",), vmem_limit_bytes=VMEM_LIMIT)
    x13 = x1.reshape(T, SUBLANES, LANES)
    outs = []
    tok0 = 0
    for units, on_sc in PEER_PLAN:
        tc = units * unit
        steps = tc // PEER_TT
        off = tok0 // PEER_TT
        row = lambda i, off=off: (i + off, 0)
        row3 = lambda i, off=off: (i + off, 0, 0)
        act = pl.pallas_call(
            _peer_u_kernel,
            grid=(steps,),
            in_specs=[pl.BlockSpec((PEER_TT, nsel), row, memory_space=pltpu.SMEM), tab_spec,
                      pl.BlockSpec((PEER_TT, SUBLANES, LANES), row3), pl.BlockSpec((PEER_TT, nsel), row)],
            out_specs=pl.BlockSpec((PEER_TT, nsel), lambda i: (i, 0)),
            out_shape=jax.ShapeDtypeStruct((tc, nsel), jnp.float32),
            scratch_shapes=[pltpu.VMEM((4 * nsel, LANES), jnp.float32), pltpu.VMEM((4 * nsel, LANES), jnp.float32),
                            pltpu.VMEM((PEER_TT * nsel, LANES), jnp.float32)],
            compiler_params=cp,
            name="peer_u",
        )(experts, u_packed, xn3, gates)
        if on_sc:
            outs.append(_sc_peer_v(v_table, experts, act, x1, tok0, tc))
        else:
            out3 = pl.pallas_call(
                _peer_v_kernel,
                grid=(steps,),
                in_specs=[pl.BlockSpec((PEER_TT, nsel), row, memory_space=pltpu.SMEM),
                          pl.BlockSpec((PEER_TT, nsel), lambda i: (i, 0), memory_space=pltpu.SMEM),
                          tab_spec, pl.BlockSpec((PEER_TT, SUBLANES, LANES), row3)],
                out_specs=pl.BlockSpec((PEER_TT, SUBLANES, LANES), lambda i: (i, 0, 0)),
                out_shape=jax.ShapeDtypeStruct((tc, SUBLANES, LANES), jnp.float32),
                compiler_params=cp,
                name="peer_v",
            )(experts, act, v_packed, x13)
            outs.append(out3.reshape(tc, d))
        tok0 += tc
    return jnp.concatenate(outs, axis=0)


def _alibi_slopes(n):
    return 2.0 ** (-8.0 * np.arange(1, n + 1) / n)


def _lambda_init(layer):
    return 0.8 - 0.6 * math.exp(-0.3 * layer)


def kernel(x, norm1_g, w_in, moba_qk_norm_g, diff_qk_norm_g, diff_lambda, diff_subln_g, w_out, norm2_g,
           peer_w_query, peer_sub_keys, peer_u, peer_v):
    b, s, d = x.shape
    T = b * s
    depth = w_in.shape[0]
    mw = MOBA_HEADS * HEAD_DIM
    assert d == 1024 and s % PROJ_TM == 0 and mw == 512 and DIFF_HEADS * 2 * HEAD_DIM == 512
    f32, bf16 = jnp.float32, jnp.bfloat16
    scale = HEAD_DIM ** -0.5
    grp = lax.broadcasted_iota(jnp.int32, (LANES, LANES), 0) // HEAD_DIM
    gsum = (grp == grp.T).astype(bf16)
    moba_slopes = jnp.asarray(_alibi_slopes(MOBA_HEADS).reshape(MOBA_HEADS // 2, 2), f32)
    diff_slopes = jnp.asarray(np.repeat(_alibi_slopes(DIFF_HEADS)[:, None], 2, axis=1), f32)
    nblk = PROJ_TM // MOBA_BLOCK
    x2 = x.reshape(T, d)
    for l in range(depth):
        w = w_in[l]
        wqk = jnp.concatenate([w[:, 0:2 * mw], w[:, 3 * mw:3 * mw + 1024]], axis=1).astype(bf16)
        wvt = jnp.concatenate([w[:, 2 * mw:3 * mw], w[:, 3 * mw + 1024:]], axis=1).T.astype(bf16)
        mg, dg = moba_qk_norm_g[l].astype(f32), diff_qk_norm_g[l].astype(f32)
        gqk = jnp.concatenate([jnp.tile(mg[0] * scale, 8), jnp.tile(mg[1], 8),
                               jnp.tile(dg[0] * scale, 8), jnp.tile(dg[1], 8)])[None, :]
        g1 = norm1_g[l][None, :].astype(f32)
        g2 = norm2_g[l][None, :].astype(f32)
        gsub = jnp.broadcast_to(diff_subln_g[l].astype(f32)[:, None], (LANES, MOBA_BLOCK))
        lam_p = diff_lambda[l].astype(f32)
        wo, wq = w_out[l].astype(bf16), peer_w_query[l].astype(bf16)
        keys = peer_sub_keys[l].reshape(2 * PEER_HEADS, PEER_N_KEYS, LANES).astype(bf16)
        u_packed = _pack_table(peer_u[l]).reshape(peer_u.shape[1], 4, LANES)
        v_packed = _pack_table(peer_v[l]).reshape(peer_v.shape[1], 4, LANES)
        v_table = peer_v[l].astype(f32)
        mq, mk, dq, dk, mvt, dvt, kmean = _project(x2, g1, wqk, wvt, gqk, gsum)
        kmean = kmean.reshape(T // PROJ_TM, SUBLANES, 512)[:, :nblk].reshape(T // MOBA_BLOCK, 512)
        moba_o = _moba_attention(mq, mk, mvt, kmean, moba_slopes, b, s)
        diff_o = _diff_attention(dq, dk, dvt, lam_p, gsub, diff_slopes, _lambda_init(l), b, s)
        x1, xn, experts, gates = _mix(x2, moba_o, diff_o, wo, g2, wq, keys)
        x2 = _peer_experts(xn.reshape(T, SUBLANES, LANES), x1, experts, gates, u_packed, v_table, v_packed)
    return x2.reshape(b, s, d)
```
